```python
import math
import jax, jax.numpy as jnp
from jax import lax
import numpy as np

D_MODEL = 1024
BATCH = 8
SEQ = 4096
DEPTH = 2

N_MIXERS = 4
D_MIX = D_MODEL
GW = D_MIX // N_MIXERS
HEAD_DIM = 64
RW_HEADS = GW // HEAD_DIM
W_LORA = 64
A_LORA = 64
G_LORA = 128
RW_GN_EPS = 64e-5
S5_GROUP = 16
S5_NGROUPS = GW // S5_GROUP
S5_STATE = 64
S5_DT_MIN = 1e-3
S5_DT_MAX = 1e-1
M_HEADS = GW // HEAD_DIM
M_HEAD_DIM = HEAD_DIM
M_NGROUPS = 2
M_DSTATE = 64
M_CONV = 4
M_CHUNK = 128
M_CONV_CH = GW + 2 * M_NGROUPS * M_DSTATE
HG_HEADS = GW // HEAD_DIM
HG_KDIM = GW // HG_HEADS
HG_VDIM = GW // HG_HEADS
HG_CHUNK = 64
HG_F_FLOOR = 1e-20
RW_COLS = 3 * GW + W_LORA + A_LORA + G_LORA
S5_COLS = GW
M_COLS = GW + M_CONV_CH + M_HEADS
HG_COLS = 4 * GW
PROJ_COLS = RW_COLS + S5_COLS + M_COLS + HG_COLS
N_EXPERT_GROUPS = 4
EXPERTS_PER_GROUP = 8
TOP_K_INNER = 2
D_EXPERT = D_MODEL // 4
NORM_EPS = 1e-6

kernel_name = 'hymba_hybrid_rwkv7_s5_mamba2_hgrn2_hmoe'


def rms_norm(x, w, eps=NORM_EPS):
    xf = x.astype(jnp.float32)
    y = xf * lax.rsqrt(jnp.mean(xf * xf, axis=-1, keepdims=True) + eps)
    return (y * w.astype(jnp.float32)).astype(x.dtype)


def token_shift(t):
    return jnp.pad(t, ((0, 0), (1, 0), (0, 0)))[:, :-1]


def split_heads(t, n_heads):
    return t.reshape(t.shape[0], t.shape[1], n_heads, -1)


def rwkv7_recurrence(r, w, k, v, kk, a):
    bsz, _, nh, hd = r.shape

    def step(s, inp):
        r_t, w_t, k_t, v_t, kk_t, a_t = inp
        sa = jnp.einsum('bhvk,bhk->bhv', s, -kk_t)
        s = (s * w_t[:, :, None, :] + sa[..., None] * (kk_t * a_t)[:, :, None, :]
             + v_t[..., None] * k_t[:, :, None, :])
        return s, jnp.einsum('bhvk,bhk->bhv', s, r_t)

    xs = tuple(jnp.moveaxis(t, 1, 0) for t in (r, w, k, v, kk, a))
    s0 = jnp.zeros((bsz, nh, hd, hd), jnp.float32)
    _, y = lax.scan(step, s0, xs)
    return jnp.moveaxis(y, 0, 1)


def rwkv7_mixer(p, mu, w0, w2, a0, a2, g2, k_k, k_a, r_k, lnx_w, lnx_b):
    bsz, seq, _ = p.shape
    p = p.astype(jnp.float32)
    p = p + (token_shift(p) - p) * mu
    r, k, v, wl, al, gl = jnp.split(
        p, [GW, 2 * GW, 3 * GW, 3 * GW + W_LORA, 3 * GW + W_LORA + A_LORA], axis=-1)
    w_log = -jnp.exp(-jax.nn.softplus(-(w0 + jnp.tanh(wl) @ w2)) - 0.5)
    a = jax.nn.sigmoid(a0 + al @ a2)
    g = jax.nn.sigmoid(gl) @ g2
    kk = split_heads(k * k_k, RW_HEADS)
    kk = kk / jnp.maximum(jnp.linalg.norm(kk, axis=-1, keepdims=True), 1e-12)
    k = k * (1.0 + (a - 1.0) * k_a)
    rh, kh, vh, ah, wh = (split_heads(t, RW_HEADS) for t in (r, k, v, a, w_log))
    y = rwkv7_recurrence(rh, jnp.exp(wh), kh, vh, kk, ah)
    m_y = jnp.mean(y, axis=-1, keepdims=True)
    var = jnp.mean(jnp.square(y - m_y), axis=-1, keepdims=True)
    y = ((y - m_y) * lax.rsqrt(var + RW_GN_EPS)).reshape(bsz, seq, GW) * lnx_w + lnx_b
    bonus = jnp.sum(rh * kh * r_k, axis=-1, keepdims=True) * vh
    return (y + bonus.reshape(bsz, seq, GW)) * g


def complex_affine_combine(e1, e2):
    a1r, a1i, b1r, b1i = e1
    a2r, a2i, b2r, b2i = e2
    return (a2r * a1r - a2i * a1i, a2r * a1i + a2i * a1r,
            a2r * b1r - a2i * b1i + b2r, a2r * b1i + a2i * b1r + b2i)


def s5_mixer(u, lam_re, lam_im, log_dt, b_re, b_im, c_re, c_im, d_skip, w_glu, b_glu):
    bsz, seq, _ = u.shape
    f32 = jnp.float32
    uf = u.astype(f32).reshape(bsz, seq, S5_NGROUPS, S5_GROUP)
    lr = jnp.minimum(lam_re.astype(f32), -1e-4)
    li = lam_im.astype(f32)
    dt = jnp.exp(log_dt.astype(f32))[:, None]
    mag = jnp.exp(lr * dt)
    ar, ai = mag * jnp.cos(li * dt), mag * jnp.sin(li * dt)
    den = lr * lr + li * li
    nr = ar - 1.0
    er, ei = (nr * lr + ai * li) / den, (ai * lr - nr * li) / den
    bbr = er[..., None] * b_re - ei[..., None] * b_im
    bbi = er[..., None] * b_im + ei[..., None] * b_re
    bu_r = jnp.einsum('blgh,gph->blgp', uf, bbr)
    bu_i = jnp.einsum('blgh,gph->blgp', uf, bbi)
    a_r = jnp.broadcast_to(ar, (1, seq) + ar.shape)
    a_i = jnp.broadcast_to(ai, (1, seq) + ai.shape)
    _, _, xr, xi = lax.associative_scan(complex_affine_combine, (a_r, a_i, bu_r, bu_i), axis=1)
    y = (jnp.einsum('blgp,ghp->blgh', xr, c_re) - jnp.einsum('blgp,ghp->blgh', xi, c_im)
         + d_skip.reshape(S5_NGROUPS, S5_GROUP) * uf)
    y = jax.nn.gelu(y.reshape(bsz, seq, GW))
    return y * jax.nn.sigmoid(y @ w_glu + b_glu)


def segsum_exp(cs):
    t = cs.shape[-1]
    mask = jnp.tril(jnp.ones((t, t), bool))
    diff = jnp.where(mask, cs[..., :, None] - cs[..., None, :], 0.0)
    return jnp.where(mask, jnp.exp(diff), 0.0)


def ssd_chunked(x, a, b, c, chunk):
    bsz, seq, nh, hp = x.shape
    nc = seq // chunk
    xc = x.reshape(bsz, nc, chunk, nh, hp)
    bc = b.reshape(bsz, nc, chunk, nh, -1)
    cc = c.reshape(bsz, nc, chunk, nh, -1)
    a_cs = jnp.cumsum(a.reshape(bsz, nc, chunk, nh).transpose(0, 3, 1, 2), axis=-1)
    scores = jnp.einsum('bclhn,bcshn->bhcls', cc, bc) * segsum_exp(a_cs)
    y_diag = jnp.einsum('bhcls,bcshp->bclhp', scores, xc)
    decay_states = jnp.exp(a_cs[..., -1:] - a_cs)
    states = jnp.einsum('bclhn,bhcl,bclhp->bchpn', bc, decay_states, xc)
    states = jnp.concatenate([jnp.zeros_like(states[:, :1]), states], axis=1)
    chunk_cs = jnp.cumsum(jnp.pad(a_cs[..., -1], ((0, 0), (0, 0), (1, 0))), axis=-1)
    states = jnp.einsum('bhzc,bchpn->bzhpn', segsum_exp(chunk_cs), states)[:, :-1]
    y_off = jnp.einsum('bclhn,bchpn,bhcl->bclhp', cc, states, jnp.exp(a_cs))
    return (y_diag + y_off).reshape(bsz, seq, nh, hp)


def mamba2_mixer(z, xbc, dt_raw, conv_w, conv_b, dt_bias, a_log, d_skip, norm_w):
    bsz, seq, _ = z.shape
    f32 = jnp.float32
    xbc = lax.conv_general_dilated(
        xbc.astype(f32), conv_w.astype(f32)[:, None, :], (1,), [(M_CONV - 1, 0)],
        dimension_numbers=('NWC', 'WIO', 'NWC'), feature_group_count=M_CONV_CH)
    xbc = jax.nn.silu(xbc + conv_b)
    xs, bm, cm = jnp.split(xbc, [GW, GW + M_NGROUPS * M_DSTATE], axis=-1)
    rep = M_HEADS // M_NGROUPS
    xs = xs.reshape(bsz, seq, M_HEADS, M_HEAD_DIM)
    bm = jnp.repeat(bm.reshape(bsz, seq, M_NGROUPS, M_DSTATE), rep, axis=2)
    cm = jnp.repeat(cm.reshape(bsz, seq, M_NGROUPS, M_DSTATE), rep, axis=2)
    dt = jax.nn.softplus(dt_raw.astype(f32) + dt_bias)
    a = -jnp.exp(a_log.astype(f32))
    y = ssd_chunked(xs * dt[..., None], dt * a, bm, cm, M_CHUNK)
    y = y + d_skip[:, None] * xs
    y = y.reshape(bsz, seq, M_NGROUPS, GW // M_NGROUPS) * jax.nn.silu(
        z.astype(f32)).reshape(bsz, seq, M_NGROUPS, GW // M_NGROUPS)
    return rms_norm(y, norm_w.reshape(M_NGROUPS, -1)).reshape(bsz, seq, GW)


def gla_chunked(q, k, v, log_f, chunk):
    bsz, seq, nh, dk = q.shape
    dv = v.shape[-1]
    nc = seq // chunk

    def to_chunks(t):
        return t.reshape(bsz, nc, chunk, nh, t.shape[-1]).transpose(1, 0, 3, 2, 4)

    qc, kc, vc, gc = (to_chunks(t) for t in (q, k, v, log_f))
    gc = jnp.cumsum(gc, axis=3)
    causal = jnp.tril(jnp.ones((chunk, chunk), bool))[:, :, None]

    def step(s, inp):
        q_c, k_c, v_c, g_c = inp
        g_last = g_c[:, :, -1, :]
        o_inter = jnp.einsum('bhik,bhkv->bhiv', q_c * jnp.exp(g_c), s)
        rel = jnp.where(causal, g_c[:, :, :, None, :] - g_c[:, :, None, :, :], 0.0)
        dec = jnp.where(causal, jnp.exp(rel), 0.0)
        attn = jnp.einsum('bhik,bhjk,bhijk->bhij', q_c, k_c, dec)
        o_intra = jnp.einsum('bhij,bhjv->bhiv', attn, v_c)
        k_dec = k_c * jnp.exp(g_last[:, :, None, :] - g_c)
        s = jnp.exp(g_last)[..., None] * s + jnp.einsum('bhjk,bhjv->bhkv', k_dec, v_c)
        return s, o_inter + o_intra

    s0 = jnp.zeros((bsz, nh, dk, dv), jnp.float32)
    _, o = lax.scan(step, s0, (qc, kc, vc, gc))
    return o.transpose(1, 0, 3, 2, 4).reshape(bsz, seq, nh, dv)


def hgrn2_mixer(q, f, i, g, lb, norm_w):
    bsz, seq, _ = q.shape
    f32 = jnp.float32
    ff = f.astype(f32)
    fg = lb + (1.0 - lb) * jax.nn.sigmoid(ff)
    log_f = jnp.log(jnp.maximum(fg, HG_F_FLOOR))
    k = (1.0 - lb) * jax.nn.sigmoid(-ff)
    o = gla_chunked(split_heads(jax.nn.silu(q.astype(f32)), HG_HEADS),
                    split_heads(k, HG_HEADS),
                    split_heads(i.astype(f32), HG_HEADS),
                    split_heads(log_f, HG_HEADS), HG_CHUNK)
    o = rms_norm(o, norm_w.reshape(HG_HEADS, HG_VDIM))
    return o.reshape(bsz, seq, GW) * jax.nn.silu(g.astype(f32))


def hier_moe(h, w_rg, b_rg, w_re, b_re, w_gate, w_up, w_down):
    bsz, seq, d = h.shape
    f32 = jnp.float32
    t = h.reshape(bsz * seq, d)
    pg = jax.nn.softmax((t @ w_rg + b_rg).astype(f32), axis=-1)
    g_w, g_idx = lax.top_k(pg, 1)
    g_onehot = jax.nn.one_hot(g_idx[:, 0], N_EXPERT_GROUPS, dtype=f32)
    e_logits = (t @ w_re + b_re).astype(f32).reshape(-1, N_EXPERT_GROUPS, EXPERTS_PER_GROUP)
    sel = jnp.sum(e_logits * g_onehot[:, :, None], axis=1)
    pe = jax.nn.softmax(sel, axis=-1)
    e_w, e_idx = lax.top_k(pe, TOP_K_INNER)
    e_w = e_w / jnp.sum(e_w, axis=-1, keepdims=True)
    within = jnp.sum(jax.nn.one_hot(e_idx, EXPERTS_PER_GROUP, dtype=f32) * e_w[..., None], axis=1)
    comb = g_onehot[:, :, None] * within[:, None, :] * g_w[:, :, None]
    out = jnp.zeros(t.shape, f32)
    for gi in range(N_EXPERT_GROUPS):
        a = jnp.einsum('td,edf->tef', t, w_gate[gi])
        u = jnp.einsum('td,edf->tef', t, w_up[gi])
        act = jax.nn.silu(a) * u * comb[:, gi, :, None]
        out = out + jnp.einsum('tef,efd->td', act, w_down[gi])
    return out.reshape(bsz, seq, d).astype(h.dtype)


def setup_inputs(seed: int = 0) -> dict:
    key = jax.random.key(seed)
    ks = iter(jax.random.split(key, 64))
    f32 = jnp.float32
    L = DEPTH

    def nrm(shape, scale):
        return jax.random.normal(next(ks), shape, f32) * scale

    def gain(shape):
        return 1.0 + 0.05 * jax.random.normal(next(ks), shape, f32)

    def unif(shape, lo, hi):
        return jax.random.uniform(next(ks), shape, f32, lo, hi)

    x = nrm((BATCH, SEQ, D_MODEL), 1.0)
    ln1_w = gain((L, D_MODEL))
    w_in = nrm((L, D_MODEL, PROJ_COLS), D_MODEL ** -0.5)
    rw_mu = unif((L, RW_COLS), 0.0, 1.0)
    rw_w0 = nrm((L, GW), 0.5)
    rw_w2 = nrm((L, W_LORA, GW), 0.1 * W_LORA ** -0.5)
    rw_a0 = nrm((L, GW), 0.1)
    rw_a2 = nrm((L, A_LORA, GW), A_LORA ** -0.5)
    rw_g2 = nrm((L, G_LORA, GW), G_LORA ** -0.5)
    rw_k_k = 0.85 + nrm((L, GW), 0.05)
    rw_k_a = gain((L, GW))
    rw_r_k = nrm((L, RW_HEADS, HEAD_DIM), 0.1)
    rw_lnx_w = gain((L, GW))
    rw_lnx_b = nrm((L, GW), 0.01)
    s5_lam_re = -0.5 + nrm((L, S5_NGROUPS, S5_STATE), 0.01)
    s5_lam_im = (jnp.broadcast_to(jnp.pi * jnp.arange(S5_STATE, dtype=f32), (L, S5_NGROUPS, S5_STATE))
                 + nrm((L, S5_NGROUPS, S5_STATE), 0.01))
    s5_log_dt = unif((L, S5_NGROUPS), math.log(S5_DT_MIN), math.log(S5_DT_MAX))
    s5_b_re = nrm((L, S5_NGROUPS, S5_STATE, S5_GROUP), (2 * S5_GROUP) ** -0.5)
    s5_b_im = nrm((L, S5_NGROUPS, S5_STATE, S5_GROUP), (2 * S5_GROUP) ** -0.5)
    s5_c_re = nrm((L, S5_NGROUPS, S5_GROUP, S5_STATE), S5_STATE ** -0.5)
    s5_c_im = nrm((L, S5_NGROUPS, S5_GROUP, S5_STATE), S5_STATE ** -0.5)
    s5_d = nrm((L, GW), 1.0)
    s5_w_glu = nrm((L, GW, GW), GW ** -0.5)
    s5_b_glu = nrm((L, GW), 0.01)
    m_conv_w = nrm((L, M_CONV, M_CONV_CH), M_CONV ** -0.5)
    m_conv_b = nrm((L, M_CONV_CH), 0.01)
    dt0 = jnp.exp(unif((L, M_HEADS), math.log(1e-3), math.log(1e-1)))
    m_dt_bias = dt0 + jnp.log(-jnp.expm1(-dt0))
    m_a_log = jnp.log(unif((L, M_HEADS), 1.0, 16.0))
    m_d = gain((L, M_HEADS))
    m_norm_w = gain((L, GW))
    hg_lb_logits = nrm((L, GW), 0.1)
    hg_norm_w = gain((L, GW))
    w_out = nrm((L, D_MIX, D_MODEL), D_MIX ** -0.5)
    ln2_w = gain((L, D_MODEL))
    moe_w_rg = nrm((L, D_MODEL, N_EXPERT_GROUPS), D_MODEL ** -0.5)
    moe_b_rg = nrm((L, N_EXPERT_GROUPS), 0.01)
    moe_w_re = nrm((L, D_MODEL, N_EXPERT_GROUPS * EXPERTS_PER_GROUP), D_MODEL ** -0.5)
    moe_b_re = nrm((L, N_EXPERT_GROUPS * EXPERTS_PER_GROUP), 0.01)
    moe_w_gate = nrm((L, N_EXPERT_GROUPS, EXPERTS_PER_GROUP, D_MODEL, D_EXPERT), D_MODEL ** -0.5)
    moe_w_up = nrm((L, N_EXPERT_GROUPS, EXPERTS_PER_GROUP, D_MODEL, D_EXPERT), D_MODEL ** -0.5)
    moe_w_down = nrm((L, N_EXPERT_GROUPS, EXPERTS_PER_GROUP, D_EXPERT, D_MODEL), D_EXPERT ** -0.5)
    lnf_w = gain((D_MODEL,))
    return {'x': x, 'ln1_w': ln1_w, 'w_in': w_in,
            'rw_mu': rw_mu, 'rw_w0': rw_w0, 'rw_w2': rw_w2, 'rw_a0': rw_a0, 'rw_a2': rw_a2,
            'rw_g2': rw_g2, 'rw_k_k': rw_k_k, 'rw_k_a': rw_k_a, 'rw_r_k': rw_r_k,
            'rw_lnx_w': rw_lnx_w, 'rw_lnx_b': rw_lnx_b,
            's5_lam_re': s5_lam_re, 's5_lam_im': s5_lam_im, 's5_log_dt': s5_log_dt,
            's5_b_re': s5_b_re, 's5_b_im': s5_b_im, 's5_c_re': s5_c_re, 's5_c_im': s5_c_im,
            's5_d': s5_d, 's5_w_glu': s5_w_glu, 's5_b_glu': s5_b_glu,
            'm_conv_w': m_conv_w, 'm_conv_b': m_conv_b, 'm_dt_bias': m_dt_bias,
            'm_a_log': m_a_log, 'm_d': m_d, 'm_norm_w': m_norm_w,
            'hg_lb_logits': hg_lb_logits, 'hg_norm_w': hg_norm_w,
            'w_out': w_out, 'ln2_w': ln2_w,
            'moe_w_rg': moe_w_rg, 'moe_b_rg': moe_b_rg, 'moe_w_re': moe_w_re, 'moe_b_re': moe_b_re,
            'moe_w_gate': moe_w_gate, 'moe_w_up': moe_w_up, 'moe_w_down': moe_w_down,
            'lnf_w': lnf_w}


def reference(x, ln1_w, w_in, rw_mu, rw_w0, rw_w2, rw_a0, rw_a2, rw_g2, rw_k_k, rw_k_a, rw_r_k,
              rw_lnx_w, rw_lnx_b, s5_lam_re, s5_lam_im, s5_log_dt, s5_b_re, s5_b_im, s5_c_re,
              s5_c_im, s5_d, s5_w_glu, s5_b_glu, m_conv_w, m_conv_b, m_dt_bias, m_a_log, m_d,
              m_norm_w, hg_lb_logits, hg_norm_w, w_out, ln2_w, moe_w_rg, moe_b_rg, moe_w_re,
              moe_b_re, moe_w_gate, moe_w_up, moe_w_down, lnf_w):
    lbs = jax.nn.softmax(hg_lb_logits.astype(jnp.float32), axis=0)
    lbs = jnp.cumsum(lbs, axis=0) - lbs[0:1]
    splits = [RW_COLS, RW_COLS + S5_COLS, RW_COLS + S5_COLS + M_COLS]
    h = x
    for l in range(DEPTH):
        proj = rms_norm(h, ln1_w[l]) @ w_in[l]
        p_rw, p_s5, p_m, p_hg = jnp.split(proj, splits, axis=-1)
        y_rw = rwkv7_mixer(p_rw, rw_mu[l], rw_w0[l], rw_w2[l], rw_a0[l], rw_a2[l], rw_g2[l],
                           rw_k_k[l], rw_k_a[l], rw_r_k[l], rw_lnx_w[l], rw_lnx_b[l])
        y_s5 = s5_mixer(p_s5, s5_lam_re[l], s5_lam_im[l], s5_log_dt[l], s5_b_re[l], s5_b_im[l],
                        s5_c_re[l], s5_c_im[l], s5_d[l], s5_w_glu[l], s5_b_glu[l])
        m_z, m_xbc, m_dt = jnp.split(p_m, [GW, GW + M_CONV_CH], axis=-1)
        y_m = mamba2_mixer(m_z, m_xbc, m_dt, m_conv_w[l], m_conv_b[l], m_dt_bias[l],
                           m_a_log[l], m_d[l], m_norm_w[l])
        hq, hf, hi, hg = jnp.split(p_hg, 4, axis=-1)
        y_hg = hgrn2_mixer(hq, hf, hi, hg, lbs[l], hg_norm_w[l])
        y = jnp.concatenate([y_rw, y_s5, y_m, y_hg], axis=-1).astype(h.dtype)
        h = h + y @ w_out[l]
        h = h + hier_moe(rms_norm(h, ln2_w[l]), moe_w_rg[l], moe_b_rg[l], moe_w_re[l],
                         moe_b_re[l], moe_w_gate[l], moe_w_up[l], moe_w_down[l])
    return rms_norm(h, lnf_w)
```

```python
import functools
import math

import jax
import jax.numpy as jnp
from jax import lax
from jax.experimental import pallas as pl
from jax.experimental.pallas import tpu as pltpu

F32 = jnp.float32
BF16 = jnp.bfloat16
HIGHEST = lax.Precision.HIGHEST

NORM_EPS = 1e-6
GW = 256
HEAD_DIM = 64
N_HEADS = GW // HEAD_DIM
RW_GN_EPS = 64e-5
HG_F_FLOOR = 1e-20
S5_STATE_W = 1024
M_CHUNK = 128
N_EXPERTS = 32
EXPERTS_PER_GROUP = 8
N_EXPERT_GROUPS = 4
ROUTE_LANES = 128
ROUTE_OFF = N_EXPERT_GROUPS

PROJ_PAD = 3200
VMEM_LIMIT = 56 * 1024 * 1024


def _mm(a, b):
    return jnp.dot(a.astype(BF16), b.astype(BF16), preferred_element_type=F32)


def _mm_nt(a, b):
    return lax.dot_general(a.astype(BF16), b.astype(BF16), (((1,), (1,)), ((), ())),
                           preferred_element_type=F32)


def _mm_tn(a, b):
    return lax.dot_general(a.astype(BF16), b.astype(BF16), (((0,), (0,)), ((), ())),
                           preferred_element_type=F32)


def _mm_hi(a, b):
    return jnp.dot(a, b, precision=HIGHEST, preferred_element_type=F32)


def _sigmoid(x):
    return 1.0 / (1.0 + jnp.exp(-x))


def _silu(x):
    return x * _sigmoid(x)


def _softplus(x):
    return jnp.maximum(x, 0.0) + jnp.log1p(jnp.exp(-jnp.abs(x)))


def _rms(x, w):
    ms = jnp.mean(x * x, axis=-1, keepdims=True)
    return x * lax.rsqrt(ms + NORM_EPS) * w


def _stack4(x, bd):
    return jnp.concatenate([x, x, x, x], axis=0) * bd


def _const(shape):
    return pl.BlockSpec(shape, lambda *_: (0,) * len(shape))


def _params(sem):
    return pltpu.CompilerParams(dimension_semantics=sem, vmem_limit_bytes=VMEM_LIMIT)


def _inproj_body(h_ref, lnw_ref, w_ref, o_ref):
    o_ref[...] = _mm(_rms(h_ref[...], lnw_ref[...]), w_ref[...])


def _inproj(h, lnw, w, tm=512):
    t, d = h.shape
    pc = w.shape[1]
    return pl.pallas_call(
        _inproj_body,
        out_shape=jax.ShapeDtypeStruct((t, pc), F32),
        grid=(t // tm,),
        in_specs=[pl.BlockSpec((tm, d), lambda i: (i, 0)), _const((1, d)), _const((d, pc))],
        out_specs=pl.BlockSpec((tm, pc), lambda i: (i, 0)),
        compiler_params=_params(("parallel",)),
        name="inproj",
    )(h, lnw, w)


RW_CHUNK = 64


def _rwkv_body(p_ref, mu_ref, w0_ref, w2_ref, a0_ref, a2_ref, g2_ref, kk_ref, ka_ref, rk_ref,
               lnw_ref, lnb_ref, bd_ref, hm_ref, tri_ref, lowi_ref, lows_ref, eye_ref,
               o_ref, carry_ref, st_ref, r_s, k_s, v_s, kn_s, kb_s, wl_s, y_s):
    tt = p_ref.shape[1]

    @pl.when(pl.program_id(1) == 0)
    def _():
        carry_ref[...] = jnp.zeros_like(carry_ref)
        st_ref[...] = jnp.zeros_like(st_ref)

    p = p_ref[0]
    rows = lax.broadcasted_iota(jnp.int32, p.shape, 0)
    prev = jnp.where(rows == 0, carry_ref[0:1, :], pltpu.roll(p, 1, axis=0))
    carry_ref[0:1, :] = p[tt - 1:tt, :]
    p = p + (prev - p) * mu_ref[...]
    r = p[:, 0:GW]
    k = p[:, GW:2 * GW]
    v = p[:, 2 * GW:3 * GW]
    wl = p[:, 3 * GW:3 * GW + 64]
    al = p[:, 3 * GW + 64:3 * GW + 128]
    gl = p[:, 3 * GW + 128:]
    hm = hm_ref[...]
    w_log = -jnp.exp(-_softplus(-(w0_ref[...] + _mm_hi(jnp.tanh(wl), w2_ref[...]))) - 0.5)
    a = _sigmoid(a0_ref[...] + _mm(al, a2_ref[...]))
    gate = _mm(_sigmoid(gl), g2_ref[...])
    kn = k * kk_ref[...]
    nrm = jnp.sqrt(_mm_hi(kn * kn, hm) * float(HEAD_DIM))
    kn = kn / jnp.maximum(nrm, 1e-12)
    k = k * (1.0 + (a - 1.0) * ka_ref[...])
    r_s[...] = r
    k_s[...] = k
    v_s[...] = v
    kn_s[...] = kn
    kb_s[...] = kn * a
    wl_s[...] = w_log

    def chunk(c, carry):
        sl = pl.ds(pl.multiple_of(c * RW_CHUNK, RW_CHUNK), RW_CHUNK)
        r_c, k_c, v_c, kn_c, kb_c, wl_c = (s[sl, :] for s in (r_s, k_s, v_s, kn_s, kb_s, wl_s))
        bd = bd_ref[...]
        lows = lows_ref[...]
        lowi = lowi_ref[...]
        gc = _mm_hi(tri_ref[...], wl_c)
        g_end = gc[RW_CHUNK - 1:RW_CHUNK, :]
        inv = jnp.exp(-gc)
        to_end = jnp.exp(g_end - gc)
        at = -kn_c * jnp.exp(gc - wl_c)
        rt = r_c * jnp.exp(gc)
        lhs = jnp.concatenate([at, rt], axis=0)
        rhs = jnp.concatenate([_stack4(kb_c * inv, bd), _stack4(k_c * inv, bd)], axis=0)
        gram = _mm_nt(lhs, rhs)
        a_ab = gram[:RW_CHUNK, :GW] * lows
        a_ak = gram[:RW_CHUNK, GW:] * lows
        a_rb = gram[RW_CHUNK:, :GW] * lowi
        a_rk = gram[RW_CHUNK:, GW:] * lowi
        state = st_ref[...]
        from_state = _mm_nt(lhs, state)
        pw = eye_ref[...] + a_ab
        q = a_ab
        for _ in range(5):
            q = _mm(q, _stack4(q, bd))
            pw = pw + _mm(pw, _stack4(q, bd))
        vst = _stack4(v_c, bd)
        u = _mm(pw, _stack4(from_state[:RW_CHUNK] + _mm(a_ak, vst), bd))
        y_s[sl, :] = from_state[RW_CHUNK:] + _mm(a_rb, _stack4(u, bd)) + _mm(a_rk, vst)
        upd = _mm_tn(jnp.concatenate([u, v_c], axis=0),
                     jnp.concatenate([kb_c * to_end, k_c * to_end], axis=0))
        st_ref[...] = state * jnp.exp(g_end) + upd * bd
        return carry

    lax.fori_loop(0, tt // RW_CHUNK, chunk, 0)

    y = y_s[...]
    mean = _mm_hi(y, hm)
    d = y - mean
    var = _mm_hi(d * d, hm)
    yn = d * lax.rsqrt(var + RW_GN_EPS) * lnw_ref[...] + lnb_ref[...]
    bonus = _mm_hi(r * k * rk_ref[...], hm) * float(HEAD_DIM) * v
    o_ref[0] = (yn + bonus) * gate


def _rwkv(proj3, prm, consts, tt=256):
    b, l, _ = proj3.shape
    vec = _const((1, GW))
    scr = lambda w: pltpu.VMEM((tt, w), F32)
    return pl.pallas_call(
        _rwkv_body,
        out_shape=jax.ShapeDtypeStruct((b, l, GW), F32),
        grid=(b, l // tt),
        in_specs=[pl.BlockSpec((1, tt, 4 * GW), lambda i, j: (i, j, 0)),
                  _const((1, 4 * GW)), vec, _const((64, GW)), vec, _const((64, GW)),
                  _const((128, GW)), vec, vec, vec, vec, vec,
                  _const((GW, GW)), _const((GW, GW)), _const((RW_CHUNK, RW_CHUNK)),
                  _const((RW_CHUNK, GW)), _const((RW_CHUNK, GW)), _const((RW_CHUNK, GW))],
        out_specs=pl.BlockSpec((1, tt, GW), lambda i, j: (i, j, 0)),
        scratch_shapes=[pltpu.VMEM((8, 4 * GW), F32), pltpu.VMEM((GW, GW), F32)] + [scr(GW)] * 7,
        compiler_params=_params(("parallel", "arbitrary")),
        name="rwkv7",
    )(proj3, prm["mu"], prm["w0"], prm["w2"], prm["a0"], prm["a2"], prm["g2"], prm["k_k"],
      prm["k_a"], prm["r_k"], prm["lnx_w"], prm["lnx_b"],
      consts["bd"], consts["hm"], consts["tri64"], consts["lowi"], consts["lows"], consts["eyew"])


def _s5_body(u_ref, bbd_ref, ar_ref, ai_ref, cbd_ref, d_ref, wglu_ref, bglu_ref, o_ref,
             xr_s, xi_s, st_ref):
    nb, tt, w = u_ref.shape
    ncb = S5_STATE_W // 128

    @pl.when(pl.program_id(0) == 0)
    def _():
        st_ref[...] = jnp.zeros_like(st_ref)

    u = u_ref[...].reshape(nb * tt, w)
    bu = _mm(u, bbd_ref[...])
    for cb in range(ncb):
        xr_s[cb] = bu[:, cb * 128:(cb + 1) * 128]
        xi_s[cb] = bu[:, S5_STATE_W + cb * 128:S5_STATE_W + (cb + 1) * 128]
    ar = [jnp.broadcast_to(ar_ref[:, cb * 128:(cb + 1) * 128], (nb, 128)) for cb in range(ncb)]
    ai = [jnp.broadcast_to(ai_ref[:, cb * 128:(cb + 1) * 128], (nb, 128)) for cb in range(ncb)]

    def step(t, carry):
        rows = pl.ds(t, nb, stride=tt)
        out = []
        for cb in range(ncb):
            xr, xi = carry[2 * cb], carry[2 * cb + 1]
            nr = ar[cb] * xr - ai[cb] * xi + xr_s[cb, rows, :]
            ni = ar[cb] * xi + ai[cb] * xr + xi_s[cb, rows, :]
            xr_s[cb, rows, :] = nr
            xi_s[cb, rows, :] = ni
            out += [nr, ni]
        return tuple(out)

    fin = lax.fori_loop(0, tt, step, tuple(st_ref[i] for i in range(2 * ncb)), unroll=4)
    for i in range(2 * ncb):
        st_ref[i] = fin[i]
    cbd = cbd_ref[...]
    xr_all = jnp.concatenate([xr_s[cb] for cb in range(ncb)], axis=-1)
    xi_all = jnp.concatenate([xi_s[cb] for cb in range(ncb)], axis=-1)
    y = _mm(xr_all, cbd[:S5_STATE_W]) + _mm(xi_all, cbd[S5_STATE_W:]) + d_ref[...] * u
    y = 0.5 * y * (1.0 + jnp.tanh(math.sqrt(2.0 / math.pi) * (y + 0.044715 * (y * y * y))))
    z = _mm(y, wglu_ref[...]) + bglu_ref[...]
    o_ref[...] = (y * _sigmoid(z)).reshape(nb, tt, w)


def _s5(proj3, prm, tt=128):
    b, l, _ = proj3.shape
    return pl.pallas_call(
        _s5_body,
        out_shape=jax.ShapeDtypeStruct((b, l, GW), F32),
        grid=(l // tt,),
        in_specs=[pl.BlockSpec((b, tt, GW), lambda j: (0, j, 4)),
                  _const((GW, 2 * S5_STATE_W)), _const((1, S5_STATE_W)), _const((1, S5_STATE_W)),
                  _const((2 * S5_STATE_W, GW)), _const((1, GW)), _const((GW, GW)), _const((1, GW))],
        out_specs=pl.BlockSpec((b, tt, GW), lambda j: (0, j, 0)),
        scratch_shapes=[pltpu.VMEM((S5_STATE_W // 128, b * tt, 128), F32),
                        pltpu.VMEM((S5_STATE_W // 128, b * tt, 128), F32),
                        pltpu.VMEM((2 * S5_STATE_W // 128, b, 128), F32)],
        compiler_params=_params(("arbitrary",)),
        name="s5",
    )(proj3, prm["bbd"], prm["ar"], prm["ai"], prm["cbd"], prm["d"], prm["w_glu"], prm["b_glu"])


def _mamba_body(z_ref, xbc_ref, dt_ref, cw_ref, cb_ref, dtb_ref, aneg_ref, dexp_ref, nw_ref,
                tri_ref, exp_ref, gsel_ref, o_ref, carry_ref, st_ref):
    @pl.when(pl.program_id(1) == 0)
    def _():
        carry_ref[...] = jnp.zeros_like(carry_ref)
        st_ref[...] = jnp.zeros_like(st_ref)

    xbc = xbc_ref[0]
    tt, cw = xbc.shape
    rows = lax.broadcasted_iota(jnp.int32, xbc.shape, 0)
    tail = carry_ref[...]
    pad = jnp.zeros((tt - 8, cw), F32)
    conv = xbc * cw_ref[3:4, :]
    for s in (1, 2, 3):
        head = jnp.concatenate([pltpu.roll(tail, s, axis=0), pad], axis=0)
        shifted = jnp.where(rows < s, head, pltpu.roll(xbc, s, axis=0))
        conv = conv + shifted * cw_ref[3 - s:4 - s, :]
    carry_ref[...] = xbc[tt - 8:, :]
    xc = _silu(conv + cb_ref[...])
    xs = xc[:, :GW]
    bm = xc[:, GW:GW + 128]
    cm = xc[:, GW + 128:]
    expand = exp_ref[...]
    dt = _softplus(dt_ref[0] + dtb_ref[...])
    cs = _mm_hi(tri_ref[...], dt * aneg_ref[...])
    cs_t = cs.T
    cs_end = cs[tt - 1:tt, :]
    xdt = xs * _mm_hi(dt, expand)
    lane = lax.broadcasted_iota(jnp.int32, (tt, 128), 1)
    tril = lax.broadcasted_iota(jnp.int32, (tt, tt), 0) >= lax.broadcasted_iota(jnp.int32, (tt, tt), 1)
    lane_w = lax.broadcasted_iota(jnp.int32, (tt, GW), 1)
    y = jnp.zeros((tt, GW), F32)
    for g in range(2):
        cb = _mm_nt(jnp.where(lane // 64 == g, cm, 0.0), bm)
        for h in (2 * g, 2 * g + 1):
            decay = jnp.where(tril, jnp.exp(jnp.minimum(cs[:, h:h + 1] - cs_t[h:h + 1, :], 0.0)), 0.0)
            y = y + jnp.where(lane_w // HEAD_DIM == h, _mm(cb * decay, xdt), 0.0)
    state = st_ref[...]
    y = y + _mm(cm, state) * jnp.exp(_mm_hi(cs, expand))
    y = y + dexp_ref[...] * xs
    to_end = jnp.exp(_mm_hi(cs_end - cs, expand))
    st_ref[...] = (state * jnp.exp(_mm_hi(cs_end, expand)) + _mm_tn(bm, xdt * to_end) * gsel_ref[...])
    y = y * _silu(z_ref[0])
    nw = nw_ref[...]
    half = GW // 2
    o_ref[0] = jnp.concatenate([_rms(y[:, :half], nw[:, :half]), _rms(y[:, half:], nw[:, half:])], axis=-1)


def _mamba(proj3, prm, consts):
    b, l, _ = proj3.shape
    tt = M_CHUNK
    return pl.pallas_call(
        _mamba_body,
        out_shape=jax.ShapeDtypeStruct((b, l, GW), F32),
        grid=(b, l // tt),
        in_specs=[pl.BlockSpec((1, tt, GW), lambda i, j: (i, j, 5)),
                  pl.BlockSpec((1, tt, 2 * GW), lambda i, j: (i, j, 3)),
                  pl.BlockSpec((1, tt, 128), lambda i, j: (i, j, 24)),
                  _const((4, 2 * GW)), _const((1, 2 * GW)), _const((1, 128)), _const((1, 128)),
                  _const((1, GW)), _const((1, GW)),
                  _const((tt, tt)), _const((128, GW)), _const((128, GW))],
        out_specs=pl.BlockSpec((1, tt, GW), lambda i, j: (i, j, 0)),
        scratch_shapes=[pltpu.VMEM((8, 2 * GW), F32), pltpu.VMEM((128, GW), F32)],
        compiler_params=_params(("parallel", "arbitrary")),
        name="mamba2",
    )(proj3, proj3, proj3, prm["conv_w"], prm["conv_b"], prm["dt_bias"], prm["a_neg"], prm["d_exp"],
      prm["norm_w"], consts["tri128"], consts["expand"], consts["gsel"])


HG_BLOCK = 16


def _hgrn_body(p_ref, lb_ref, nw_ref, bd_ref, hm_ref, hones_ref, tri_ref, o_ref,
               st_ref, q_s, k_s, v_s, lf_s, o_s):
    tt = p_ref.shape[1]

    @pl.when(pl.program_id(1) == 0)
    def _():
        st_ref[...] = jnp.zeros_like(st_ref)

    p = p_ref[0]
    lb = lb_ref[...]
    hf = p[:, GW:2 * GW]
    q_s[...] = _silu(p[:, :GW])
    k_s[...] = (1.0 - lb) * _sigmoid(-hf)
    v_s[...] = p[:, 2 * GW:3 * GW]
    lf_s[...] = jnp.log(jnp.maximum(lb + (1.0 - lb) * _sigmoid(hf), HG_F_FLOOR))

    def block(n, carry):
        sl = pl.ds(pl.multiple_of(n * HG_BLOCK, HG_BLOCK), HG_BLOCK)
        q_b, k_b, v_b, lf_b = (s[sl, :] for s in (q_s, k_s, v_s, lf_s))
        g = _mm_hi(tri_ref[...], lf_b)
        g_end = g[HG_BLOCK - 1:HG_BLOCK, :]
        state = st_ref[...]
        o = _mm_nt(q_b * jnp.exp(g), state)
        prods = [q_b * jnp.exp(jnp.minimum(g - g[j:j + 1, :], 0.0)) * k_b[j:j + 1, :]
                 for j in range(HG_BLOCK)]
        att = _mm(jnp.concatenate(prods, axis=0), hones_ref[...])
        row = lax.broadcasted_iota(jnp.int32, (HG_BLOCK, GW), 0)
        for j in range(HG_BLOCK):
            o = o + jnp.where(row >= j, att[HG_BLOCK * j:HG_BLOCK * (j + 1), :], 0.0) * v_b[j:j + 1, :]
        o_s[sl, :] = o
        st_ref[...] = state * jnp.exp(g_end) + _mm_tn(v_b, k_b * jnp.exp(g_end - g)) * bd_ref[...]
        return carry

    lax.fori_loop(0, tt // HG_BLOCK, block, 0)
    o = o_s[...]
    ms = _mm_hi(o * o, hm_ref[...])
    o_ref[0] = o * lax.rsqrt(ms + NORM_EPS) * nw_ref[...] * _silu(p[:, 3 * GW:])


def _hgrn(proj3, lb, nw, consts, tt=256):
    b, l, _ = proj3.shape
    return pl.pallas_call(
        _hgrn_body,
        out_shape=jax.ShapeDtypeStruct((b, l, GW), F32),
        grid=(b, l // tt),
        in_specs=[pl.BlockSpec((1, tt, 4 * GW), lambda i, j: (i, j, 2)),
                  _const((1, GW)), _const((1, GW)), _const((GW, GW)), _const((GW, GW)),
                  _const((GW, GW)), _const((HG_BLOCK, HG_BLOCK))],
        out_specs=pl.BlockSpec((1, tt, GW), lambda i, j: (i, j, 0)),
        scratch_shapes=[pltpu.VMEM((GW, GW), F32)] + [pltpu.VMEM((tt, GW), F32)] * 5,
        compiler_params=_params(("parallel", "arbitrary")),
        name="hgrn2",
    )(proj3, lb, nw, consts["bd"], consts["hm"], consts["hones"], consts["tri16"])


def _outproj_body(h_ref, y1_ref, y2_ref, y3_ref, y4_ref, wo_ref, ln2_ref, wr_ref, br_ref,
                  hn_ref, xn_ref, comb_ref):
    y = jnp.concatenate([y1_ref[...], y2_ref[...], y3_ref[...], y4_ref[...]], axis=-1)
    h = h_ref[...] + _mm(y, wo_ref[...])
    hn_ref[...] = h
    xn = _rms(h, ln2_ref[...])
    xn_ref[...] = xn.astype(BF16)
    logits = _mm_hi(xn, wr_ref[...]) + br_ref[...]
    lane = lax.broadcasted_iota(jnp.int32, logits.shape, 1)
    neg = -jnp.inf
    big = ROUTE_LANES
    glog = jnp.where(lane < N_EXPERT_GROUPS, logits, neg)
    gmax = jnp.max(glog, axis=-1, keepdims=True)
    g_w = 1.0 / jnp.sum(jnp.exp(glog - gmax), axis=-1, keepdims=True)
    g_idx = jnp.min(jnp.where(glog == gmax, lane, big), axis=-1, keepdims=True)
    lo = ROUTE_OFF + EXPERTS_PER_GROUP * g_idx
    elog = jnp.where((lane >= lo) & (lane < lo + EXPERTS_PER_GROUP), logits, neg)
    m1 = jnp.max(elog, axis=-1, keepdims=True)
    i1 = jnp.min(jnp.where(elog == m1, lane, big), axis=-1, keepdims=True)
    elog2 = jnp.where(lane == i1, neg, elog)
    m2 = jnp.max(elog2, axis=-1, keepdims=True)
    i2 = jnp.min(jnp.where(elog2 == m2, lane, big), axis=-1, keepdims=True)
    e2 = jnp.exp(m2 - m1)
    w1 = 1.0 / (1.0 + e2)
    w2 = e2 / (1.0 + e2)
    comb_ref[...] = g_w * (jnp.where(lane == i1, w1, 0.0) + jnp.where(lane == i2, w2, 0.0))


def _outproj(h, ys, wo, ln2, wr, br, tm=512):
    t, d = h.shape
    row = lambda w: pl.BlockSpec((tm, w), lambda i: (i, 0))
    return pl.pallas_call(
        _outproj_body,
        out_shape=(jax.ShapeDtypeStruct((t, d), F32), jax.ShapeDtypeStruct((t, d), BF16),
                   jax.ShapeDtypeStruct((t, ROUTE_LANES), F32)),
        grid=(t // tm,),
        in_specs=[row(d), row(GW), row(GW), row(GW), row(GW), _const((d, d)), _const((1, d)),
                  _const((d, ROUTE_LANES)), _const((1, ROUTE_LANES))],
        out_specs=(row(d), row(d), row(ROUTE_LANES)),
        compiler_params=_params(("parallel",)),
        name="outproj_router",
    )(h, *ys, wo, ln2, wr, br)


def _moe_body(x_ref, comb_ref, h_ref, wg_ref, wu_ref, wd_ref, o_ref, acc_ref):
    e = pl.program_id(1)

    @pl.when(e == 0)
    def _():
        acc_ref[...] = h_ref[...]

    x = x_ref[...]
    comb = comb_ref[...]
    lane = lax.broadcasted_iota(jnp.int32, comb.shape, 1)
    scale = jnp.sum(jnp.where(lane == e + ROUTE_OFF, comb, 0.0), axis=-1, keepdims=True)
    act = _silu(_mm(x, wg_ref[0])) * _mm(x, wu_ref[0]) * scale
    acc_ref[...] += _mm(act, wd_ref[0])

    @pl.when(e == N_EXPERTS - 1)
    def _():
        o_ref[...] = acc_ref[...]


def _moe(xn, comb, h, wg, wu, wd, tm=1024):
    t, d = h.shape
    de = wg.shape[-1]
    row = lambda w: pl.BlockSpec((tm, w), lambda i, e: (i, 0))
    return pl.pallas_call(
        _moe_body,
        out_shape=jax.ShapeDtypeStruct((t, d), F32),
        grid=(t // tm, N_EXPERTS),
        in_specs=[row(d), row(ROUTE_LANES), row(d),
                  pl.BlockSpec((1, d, de), lambda i, e: (e, 0, 0)),
                  pl.BlockSpec((1, d, de), lambda i, e: (e, 0, 0)),
                  pl.BlockSpec((1, de, d), lambda i, e: (e, 0, 0))],
        out_specs=row(d),
        scratch_shapes=[pltpu.VMEM((tm, d), F32)],
        compiler_params=_params(("parallel", "arbitrary")),
        name="moe",
    )(xn, comb, h, wg, wu, wd)


def _norm_body(h_ref, w_ref, o_ref):
    o_ref[...] = _rms(h_ref[...], w_ref[...])


def _final_norm(h, w, tm=1024):
    t, d = h.shape
    return pl.pallas_call(
        _norm_body,
        out_shape=jax.ShapeDtypeStruct((t, d), F32),
        grid=(t // tm,),
        in_specs=[pl.BlockSpec((tm, d), lambda i: (i, 0)), _const((1, d))],
        out_specs=pl.BlockSpec((tm, d), lambda i: (i, 0)),
        compiler_params=_params(("parallel",)),
        name="final_norm",
    )(h, w)


def _mask_consts():
    i256 = jnp.arange(GW)
    same_head = (i256[:, None] // HEAD_DIM) == (i256[None, :] // HEAD_DIM)
    t64 = jnp.arange(RW_CHUNK)
    s_w = i256 % RW_CHUNK
    h128 = jnp.arange(128)
    return {
        "bd": same_head.astype(F32),
        "hm": same_head.astype(F32) / HEAD_DIM,
        "hones": same_head.astype(BF16),
        "tri64": (t64[:, None] >= t64[None, :]).astype(F32),
        "lowi": (t64[:, None] >= s_w[None, :]).astype(F32),
        "lows": (t64[:, None] > s_w[None, :]).astype(F32),
        "eyew": (t64[:, None] == s_w[None, :]).astype(F32),
        "tri16": (jnp.arange(HG_BLOCK)[:, None] >= jnp.arange(HG_BLOCK)[None, :]).astype(F32),
        "tri128": (h128[:, None] >= h128[None, :]).astype(F32),
        "expand": (h128[:, None] == (i256[None, :] // HEAD_DIM)).astype(F32),
        "gsel": ((h128[:, None] // 64) == (i256[None, :] // 128)).astype(F32),
    }


def _s5_params(lam_re, lam_im, log_dt, b_re, b_im, c_re, c_im, d_skip, w_glu, b_glu):
    lr = jnp.minimum(lam_re, -1e-4)
    li = lam_im
    dt = jnp.exp(log_dt)[:, None]
    mag = jnp.exp(lr * dt)
    ar, ai = mag * jnp.cos(li * dt), mag * jnp.sin(li * dt)
    den = lr * lr + li * li
    nr = ar - 1.0
    er, ei = (nr * lr + ai * li) / den, (ai * lr - nr * li) / den
    bbr = er[..., None] * b_re - ei[..., None] * b_im
    bbi = er[..., None] * b_im + ei[..., None] * b_re
    eye = jnp.eye(lam_re.shape[0], dtype=F32)
    pack_b = lambda m: jnp.einsum("gph,gk->ghkp", m, eye).reshape(GW, S5_STATE_W)
    pack_c = lambda m: jnp.einsum("ghp,gk->gpkh", m, eye).reshape(S5_STATE_W, GW)
    return {
        "bbd": jnp.concatenate([pack_b(bbr), pack_b(bbi)], axis=1).astype(BF16),
        "cbd": jnp.concatenate([pack_c(c_re), -pack_c(c_im)], axis=0).astype(BF16),
        "ar": ar.reshape(1, S5_STATE_W), "ai": ai.reshape(1, S5_STATE_W),
        "d": d_skip.reshape(1, GW), "w_glu": w_glu.astype(BF16), "b_glu": b_glu.reshape(1, GW),
    }


def kernel(x, ln1_w, w_in, rw_mu, rw_w0, rw_w2, rw_a0, rw_a2, rw_g2, rw_k_k, rw_k_a, rw_r_k, rw_lnx_w, rw_lnx_b, s5_lam_re, s5_lam_im, s5_log_dt, s5_b_re, s5_b_im, s5_c_re, s5_c_im, s5_d, s5_w_glu, s5_b_glu, m_conv_w, m_conv_b, m_dt_bias, m_a_log, m_d, m_norm_w, hg_lb_logits, hg_norm_w, w_out, ln2_w, moe_w_rg, moe_b_rg, moe_w_re, moe_b_re, moe_w_gate, moe_w_up, moe_w_down, lnf_w):
    bsz, seq, d = x.shape
    depth = w_in.shape[0]
    consts = _mask_consts()
    lbs = jax.nn.softmax(hg_lb_logits.astype(F32), axis=0)
    lbs = jnp.cumsum(lbs, axis=0) - lbs[0:1]
    row = lambda v: v.reshape(1, -1).astype(F32)
    h = x.reshape(bsz * seq, d)
    for l in range(depth):
        wl = w_in[l]
        n_dt = N_HEADS
        dt_col = 3 * GW + 256 + GW + GW + 2 * GW
        w_perm = jnp.concatenate(
            [wl[:, :dt_col], wl[:, dt_col + n_dt:], wl[:, dt_col:dt_col + n_dt],
             jnp.zeros((d, PROJ_PAD - wl.shape[1]), F32)], axis=1).astype(BF16)
        proj = _inproj(h, row(ln1_w[l]), w_perm).reshape(bsz, seq, PROJ_PAD)
        rw = {"mu": row(rw_mu[l]), "w0": row(rw_w0[l]), "w2": rw_w2[l], "a0": row(rw_a0[l]),
              "a2": rw_a2[l], "g2": rw_g2[l], "k_k": row(rw_k_k[l]), "k_a": row(rw_k_a[l]),
              "r_k": row(rw_r_k[l]), "lnx_w": row(rw_lnx_w[l]), "lnx_b": row(rw_lnx_b[l])}
        y_rw = _rwkv(proj, rw, consts)
        y_s5 = _s5(proj, _s5_params(s5_lam_re[l], s5_lam_im[l], s5_log_dt[l], s5_b_re[l], s5_b_im[l],
                                    s5_c_re[l], s5_c_im[l], s5_d[l], s5_w_glu[l], s5_b_glu[l]))
        pad_h = lambda v: jnp.concatenate([v.astype(F32), jnp.zeros((128 - n_dt,), F32)]).reshape(1, 128)
        mp = {"conv_w": m_conv_w[l], "conv_b": row(m_conv_b[l]), "dt_bias": pad_h(m_dt_bias[l]),
              "a_neg": pad_h(-jnp.exp(m_a_log[l].astype(F32))),
              "d_exp": row(jnp.repeat(m_d[l], HEAD_DIM)), "norm_w": row(m_norm_w[l])}
        y_m = _mamba(proj, mp, consts)
        y_hg = _hgrn(proj, row(lbs[l]), row(hg_norm_w[l]), consts)
        ys = [y.reshape(bsz * seq, GW) for y in (y_rw, y_s5, y_m, y_hg)]
        n_route = N_EXPERT_GROUPS + N_EXPERTS
        wr = jnp.concatenate([moe_w_rg[l], moe_w_re[l], jnp.zeros((d, ROUTE_LANES - n_route), F32)], axis=1)
        br = jnp.concatenate([moe_b_rg[l], moe_b_re[l], jnp.zeros((ROUTE_LANES - n_route,), F32)]).reshape(1, -1)
        h, xn, comb = _outproj(h, ys, w_out[l].astype(BF16), row(ln2_w[l]), wr, br)
        de = moe_w_gate.shape[-1]
        h = _moe(xn, comb, h,
                 moe_w_gate[l].reshape(N_EXPERTS, d, de).astype(BF16),
                 moe_w_up[l].reshape(N_EXPERTS, d, de).astype(BF16),
                 moe_w_down[l].reshape(N_EXPERTS, de, d).astype(BF16))
    return _final_norm(h, row(lnf_w)).reshape(bsz, seq, d)
```

```python
import math

import jax
import jax.numpy as jnp
from jax import lax
from jax.experimental import pallas as pl
from jax.experimental.pallas import tpu as pltpu

F32 = jnp.float32
BF16 = jnp.bfloat16

NORM_EPS = 1e-6
GW = 256
HEAD_DIM = 64
N_HEADS = GW // HEAD_DIM
LANES = 128
SUBLANES = 8
RW_GN_EPS = 64e-5
HG_F_FLOOR = 1e-20
S5_STATE_W = 1024
M_CHUNK = 128
N_EXPERTS = 32
EXPERTS_PER_GROUP = 8
N_EXPERT_GROUPS = 4
ROUTE_LANES = LANES
ROUTE_OFF = N_EXPERT_GROUPS

PROJ_MAIN = 2048
PROJ_HG = 1024
PROJ_PAD = PROJ_MAIN + PROJ_HG + LANES
VMEM_LIMIT = 56 * 1024 * 1024


def _mm(a, b):
    return jnp.dot(a.astype(BF16), b.astype(BF16), preferred_element_type=F32)


def _mm_nt(a, b):
    return lax.dot_general(a.astype(BF16), b.astype(BF16), (((1,), (1,)), ((), ())),
                           preferred_element_type=F32)


def _mm_tn(a, b):
    return lax.dot_general(a.astype(BF16), b.astype(BF16), (((0,), (0,)), ((), ())),
                           preferred_element_type=F32)


def _parts(x, n):
    out, rem = [], x
    for i in range(n):
        p = rem.astype(BF16)
        out.append(p)
        if i + 1 < n:
            rem = rem - p.astype(F32)
    return out


def _mm_x(a, b, na, nb):
    pa, pb = _parts(a, na), _parts(b, nb)
    acc = None
    for i in range(na):
        for j in range(nb):
            if i + j < max(na, nb):
                t = jnp.dot(pa[i], pb[j], preferred_element_type=F32)
                acc = t if acc is None else acc + t
    return acc


def _sigmoid(x):
    return 1.0 / (1.0 + jnp.exp(-x))


def _silu(x):
    return x * _sigmoid(x)


def _softplus(x):
    return jnp.maximum(x, 0.0) + jnp.log1p(jnp.exp(-jnp.abs(x)))


def _rms(x, w):
    ms = jnp.mean(x * x, axis=-1, keepdims=True)
    return x * lax.rsqrt(ms + NORM_EPS) * w


def _stack4(x, bd16):
    xb = x.astype(BF16)
    return jnp.concatenate([xb, xb, xb, xb], axis=0) * bd16


def _const(shape):
    return pl.BlockSpec(shape, lambda *_: (0,) * len(shape))


def _params(sem):
    return pltpu.CompilerParams(dimension_semantics=sem, vmem_limit_bytes=VMEM_LIMIT)


def _inproj_body(h_ref, lnw_ref, wa_ref, wb_ref, wc_ref, o_ref):
    xn = _rms(h_ref[...], lnw_ref[...]).astype(BF16)
    o_ref[:, :PROJ_MAIN] = jnp.dot(xn, wa_ref[...], preferred_element_type=F32)
    o_ref[:, PROJ_MAIN:PROJ_MAIN + PROJ_HG] = jnp.dot(xn, wb_ref[...], preferred_element_type=F32)
    o_ref[:, PROJ_MAIN + PROJ_HG:] = jnp.dot(xn, wc_ref[...], preferred_element_type=F32)


def _inproj(h, lnw, wa, wb, wc, tm=512):
    t, d = h.shape
    return pl.pallas_call(
        _inproj_body,
        out_shape=jax.ShapeDtypeStruct((t, PROJ_PAD), F32),
        grid=(t // tm,),
        in_specs=[pl.BlockSpec((tm, d), lambda i: (i, 0)), _const((1, d)),
                  _const((d, PROJ_MAIN)), _const((d, PROJ_HG)), _const((d, LANES))],
        out_specs=pl.BlockSpec((tm, PROJ_PAD), lambda i: (i, 0)),
        compiler_params=_params(("parallel",)),
        name="inproj",
    )(h, lnw, wa, wb, wc)


RW_CHUNK = 64


def _rwkv_body(p_ref, mu_ref, w0_ref, w2_ref, a0_ref, a2_ref, g2_ref, kk_ref, ka_ref, rk_ref,
               lnw_ref, lnb_ref, bd_ref, bd16_ref, hm_ref, tri_ref, lowi_ref, lows_ref, eye_ref,
               o_ref, carry_ref, st_ref, r_s, k_s, v_s, kn_s, kb_s, wl_s, y_s):
    nb, tt, _ = p_ref.shape

    @pl.when(pl.program_id(1) == 0)
    def _():
        carry_ref[...] = jnp.zeros_like(carry_ref)
        st_ref[...] = jnp.zeros_like(st_ref)

    rows = lax.broadcasted_iota(jnp.int32, (tt, 4 * GW), 0)
    mixed = []
    for b in range(nb):
        p = p_ref[b]
        prev = jnp.where(rows == 0, carry_ref[b, 0:1, :], pltpu.roll(p, 1, axis=0))
        carry_ref[b, 0:1, :] = p[tt - 1:tt, :]
        mixed.append(p + (prev - p) * mu_ref[...])
    p = jnp.concatenate(mixed, axis=0)
    r = p[:, 0:GW]
    k = p[:, GW:2 * GW]
    v = p[:, 2 * GW:3 * GW]
    wl = p[:, 3 * GW:3 * GW + 64]
    al = p[:, 3 * GW + 64:3 * GW + 128]
    gl = p[:, 3 * GW + 128:]
    hm = hm_ref[...]
    w_log = -jnp.exp(-_softplus(-(w0_ref[...] + _mm_x(jnp.tanh(wl), w2_ref[...], 2, 2))) - 0.5)
    a = _sigmoid(a0_ref[...] + _mm(al, a2_ref[...]))
    gate = _mm(_sigmoid(gl), g2_ref[...])
    kn = k * kk_ref[...]
    nrm = jnp.sqrt(_mm_x(kn * kn, hm, 2, 1) * float(HEAD_DIM))
    kn = kn / jnp.maximum(nrm, 1e-12)
    k = k * (1.0 + (a - 1.0) * ka_ref[...])
    r_s[...] = r
    k_s[...] = k
    v_s[...] = v
    kn_s[...] = kn
    kb_s[...] = kn * a
    wl_s[...] = w_log

    def chunk(c, carry):
        seqs = range(nb)
        sls = [pl.ds(pl.multiple_of(b * tt + c * RW_CHUNK, RW_CHUNK), RW_CHUNK) for b in seqs]
        ld = lambda s: [s[sl, :] for sl in sls]
        r_c, k_c, v_c, kn_c, kb_c, wl_c = (ld(s) for s in (r_s, k_s, v_s, kn_s, kb_s, wl_s))
        bd = bd_ref[...]
        bd16 = bd16_ref[...]
        lows = lows_ref[...]
        lowi = lowi_ref[...]
        st4 = lambda xs: [_stack4(x, bd16) for x in xs]
        gc = [_mm_x(tri_ref[...], w, 1, 3) for w in wl_c]
        g_end = [g[RW_CHUNK - 1:RW_CHUNK, :] for g in gc]
        inv = [jnp.exp(-g) for g in gc]
        lhs = [jnp.concatenate([-kn * jnp.exp(g - w), r * jnp.exp(g)], axis=0)
               for kn, r, g, w in zip(kn_c, r_c, gc, wl_c)]
        rhs = [jnp.concatenate([_stack4(kb * i, bd16), _stack4(k * i, bd16)], axis=0)
               for kb, k, i in zip(kb_c, k_c, inv)]
        gram = [_mm_nt(a, b) for a, b in zip(lhs, rhs)]
        state = [st_ref[b] for b in seqs]
        from_state = [_mm_nt(a, s) for a, s in zip(lhs, state)]
        a_ab = [g[:RW_CHUNK, :GW] * lows for g in gram]
        pw = [eye_ref[...] + a for a in a_ab]
        q = a_ab
        qs = st4(q)
        for _ in range(5):
            q = [_mm(a, b) for a, b in zip(q, qs)]
            qs = st4(q)
            pw = [p + _mm(p, b) for p, b in zip(pw, qs)]
        vst = st4(v_c)
        rhs_u = [f[:RW_CHUNK] + _mm(g[:RW_CHUNK, GW:] * lows, vs) for f, g, vs in zip(from_state, gram, vst)]
        u = [_mm(p, x) for p, x in zip(pw, st4(rhs_u))]
        ust = st4(u)
        for b in seqs:
            y_s[sls[b], :] = (from_state[b][RW_CHUNK:] + _mm(gram[b][RW_CHUNK:, :GW] * lowi, ust[b])
                              + _mm(gram[b][RW_CHUNK:, GW:] * lowi, vst[b]))
        for b in seqs:
            to_end = jnp.exp(g_end[b] - gc[b])
            upd = _mm_tn(jnp.concatenate([u[b], v_c[b]], axis=0),
                         jnp.concatenate([kb_c[b] * to_end, k_c[b] * to_end], axis=0))
            st_ref[b] = state[b] * jnp.exp(g_end[b]) + upd * bd
        return carry

    lax.fori_loop(0, tt // RW_CHUNK, chunk, 0)

    y = y_s[...]
    mean = _mm_x(y, hm, 2, 1)
    d = y - mean
    var = _mm_x(d * d, hm, 2, 1)
    yn = d * lax.rsqrt(var + RW_GN_EPS) * lnw_ref[...] + lnb_ref[...]
    bonus = _mm_x(r * k * rk_ref[...], hm, 2, 1) * float(HEAD_DIM) * v
    o_ref[...] = ((yn + bonus) * gate).reshape(nb, tt, GW)


def _rwkv(proj3, prm, consts, nb=4, tt=128):
    b, l, _ = proj3.shape
    vec = _const((1, GW))
    scr = pltpu.VMEM((nb * tt, GW), F32)
    return pl.pallas_call(
        _rwkv_body,
        out_shape=jax.ShapeDtypeStruct((b, l, GW), F32),
        grid=(b // nb, l // tt),
        in_specs=[pl.BlockSpec((nb, tt, 4 * GW), lambda i, j: (i, j, 0)),
                  _const((1, 4 * GW)), vec, _const((64, GW)), vec, _const((64, GW)),
                  _const((128, GW)), vec, vec, vec, vec, vec,
                  _const((GW, GW)), _const((GW, GW)), _const((GW, GW)), _const((RW_CHUNK, RW_CHUNK)),
                  _const((RW_CHUNK, GW)), _const((RW_CHUNK, GW)), _const((RW_CHUNK, GW))],
        out_specs=pl.BlockSpec((nb, tt, GW), lambda i, j: (i, j, 0)),
        scratch_shapes=[pltpu.VMEM((nb, SUBLANES, 4 * GW), F32), pltpu.VMEM((nb, GW, GW), F32)] + [scr] * 7,
        compiler_params=_params(("parallel", "arbitrary")),
        name="rwkv7",
    )(proj3, prm["mu"], prm["w0"], prm["w2"], prm["a0"], prm["a2"], prm["g2"], prm["k_k"],
      prm["k_a"], prm["r_k"], prm["lnx_w"], prm["lnx_b"],
      consts["bd"], consts["hones"], consts["hm"], consts["tri64"], consts["lowi"], consts["lows"], consts["eyew"])


def _s5_body(u_ref, bbd_ref, ar_ref, ai_ref, cbd_ref, d_ref, wglu_ref, bglu_ref, o_ref,
             ub_s, ut_s, xr_s, xi_s, ot_s, st_ref):
    nb, tt, w = u_ref.shape
    ncb = w // LANES
    assert nb == SUBLANES

    @pl.when(pl.program_id(0) == 0)
    def _():
        st_ref[...] = jnp.zeros_like(st_ref)

    for cb in range(ncb):
        ub_s[cb] = u_ref[:, :, cb * LANES:(cb + 1) * LANES].reshape(nb * tt, LANES)

    def regroup(t, carry):
        for cb in range(ncb):
            ut_s[cb, pl.ds(pl.multiple_of(t * nb, nb), nb), :] = ub_s[cb, pl.ds(t, nb, stride=tt), :]
        return carry

    lax.fori_loop(0, tt, regroup, 0, unroll=8)
    u = jnp.concatenate([ut_s[cb] for cb in range(ncb)], axis=-1)
    bu = _mm(u, bbd_ref[...])
    xr_s[...] = bu[:, :S5_STATE_W]
    xi_s[...] = bu[:, S5_STATE_W:]
    ar = jnp.broadcast_to(ar_ref[...], (nb, S5_STATE_W))
    ai = jnp.broadcast_to(ai_ref[...], (nb, S5_STATE_W))

    def step(t, carry):
        xr, xi = carry
        rows = pl.ds(pl.multiple_of(t * nb, nb), nb)
        nr = ar * xr - ai * xi + xr_s[rows, :]
        ni = ar * xi + ai * xr + xi_s[rows, :]
        xr_s[rows, :] = nr
        xi_s[rows, :] = ni
        return nr, ni

    xr, xi = lax.fori_loop(0, tt, step, (st_ref[0], st_ref[1]), unroll=4)
    st_ref[0] = xr
    st_ref[1] = xi
    cbd = cbd_ref[...]
    y = _mm(xr_s[...], cbd[:S5_STATE_W]) + _mm(xi_s[...], cbd[S5_STATE_W:]) + d_ref[...] * u
    y = 0.5 * y * (1.0 + jnp.tanh(math.sqrt(2.0 / math.pi) * (y + 0.044715 * (y * y * y))))
    z = _mm(y, wglu_ref[...]) + bglu_ref[...]
    out = y * _sigmoid(z)
    for cb in range(ncb):
        ot_s[cb] = out[:, cb * LANES:(cb + 1) * LANES]
    for b in range(nb):
        for cb in range(ncb):
            o_ref[b, :, cb * LANES:(cb + 1) * LANES] = ot_s[cb, pl.ds(b, tt, stride=nb), :]


def _s5(proj3, prm, tt=128):
    b, l, _ = proj3.shape
    slab = pltpu.VMEM((GW // LANES, b * tt, LANES), F32)
    wide = pltpu.VMEM((b * tt, S5_STATE_W), F32)
    return pl.pallas_call(
        _s5_body,
        out_shape=jax.ShapeDtypeStruct((b, l, GW), F32),
        grid=(l // tt,),
        in_specs=[pl.BlockSpec((b, tt, GW), lambda j: (0, j, 4)),
                  _const((GW, 2 * S5_STATE_W)), _const((1, S5_STATE_W)), _const((1, S5_STATE_W)),
                  _const((2 * S5_STATE_W, GW)), _const((1, GW)), _const((GW, GW)), _const((1, GW))],
        out_specs=pl.BlockSpec((b, tt, GW), lambda j: (0, j, 0)),
        scratch_shapes=[slab, slab, wide, wide, slab, pltpu.VMEM((2, b, S5_STATE_W), F32)],
        compiler_params=_params(("arbitrary",)),
        name="s5",
    )(proj3, prm["bbd"], prm["ar"], prm["ai"], prm["cbd"], prm["d"], prm["w_glu"], prm["b_glu"])


def _mamba_body(z_ref, xbc_ref, dt_ref, cw_ref, cb_ref, dtb_ref, aneg_ref, dexp_ref, nw_ref,
                tri_ref, exp_ref, gsel_ref, o_ref, carry_ref, st_ref):
    @pl.when(pl.program_id(1) == 0)
    def _():
        carry_ref[...] = jnp.zeros_like(carry_ref)
        st_ref[...] = jnp.zeros_like(st_ref)

    xbc = xbc_ref[0]
    tt, cw = xbc.shape
    rows = lax.broadcasted_iota(jnp.int32, xbc.shape, 0)
    tail = carry_ref[...]
    pad = jnp.zeros((tt - SUBLANES, cw), F32)
    conv = xbc * cw_ref[3:4, :]
    for s in (1, 2, 3):
        head = jnp.concatenate([pltpu.roll(tail, s, axis=0), pad], axis=0)
        shifted = jnp.where(rows < s, head, pltpu.roll(xbc, s, axis=0))
        conv = conv + shifted * cw_ref[3 - s:4 - s, :]
    carry_ref[...] = xbc[tt - SUBLANES:, :]
    xc = _silu(conv + cb_ref[...])
    xs = xc[:, :GW]
    bm = xc[:, GW:GW + 128]
    cm = xc[:, GW + 128:]
    expand = exp_ref[...]
    dt = _softplus(dt_ref[0] + dtb_ref[...])
    cs = _mm_x(tri_ref[...], dt * aneg_ref[...], 1, 3)
    cs_t = cs.T
    cs_end = cs[tt - 1:tt, :]
    xdt = xs * _mm_x(dt, expand, 3, 1)
    lane = lax.broadcasted_iota(jnp.int32, (tt, 128), 1)
    tril = lax.broadcasted_iota(jnp.int32, (tt, tt), 0) >= lax.broadcasted_iota(jnp.int32, (tt, tt), 1)
    lane_w = lax.broadcasted_iota(jnp.int32, (tt, GW), 1)
    y = jnp.zeros((tt, GW), F32)
    for g in range(2):
        cb = _mm_nt(jnp.where(lane // 64 == g, cm, 0.0), bm)
        for h in (2 * g, 2 * g + 1):
            decay = jnp.where(tril, jnp.exp(jnp.minimum(cs[:, h:h + 1] - cs_t[h:h + 1, :], 0.0)), 0.0)
            y = y + jnp.where(lane_w // HEAD_DIM == h, _mm(cb * decay, xdt), 0.0)
    state = st_ref[...]
    cs_all = _mm_x(jnp.concatenate([cs, cs_end - cs, jnp.broadcast_to(cs_end, (SUBLANES, 128))], axis=0),
                   expand, 3, 1)
    y = y + _mm(cm, state) * jnp.exp(cs_all[:tt])
    y = y + dexp_ref[...] * xs
    to_end = jnp.exp(cs_all[tt:2 * tt])
    st_ref[...] = state * jnp.exp(cs_all[2 * tt:2 * tt + 1]) + _mm_tn(bm, xdt * to_end) * gsel_ref[...]
    y = y * _silu(z_ref[0])
    nw = nw_ref[...]
    half = GW // 2
    o_ref[0] = jnp.concatenate([_rms(y[:, :half], nw[:, :half]), _rms(y[:, half:], nw[:, half:])], axis=-1)


def _mamba(proj3, prm, consts):
    b, l, _ = proj3.shape
    tt = M_CHUNK
    return pl.pallas_call(
        _mamba_body,
        out_shape=jax.ShapeDtypeStruct((b, l, GW), F32),
        grid=(b, l // tt),
        in_specs=[pl.BlockSpec((1, tt, GW), lambda i, j: (i, j, 5)),
                  pl.BlockSpec((1, tt, 2 * GW), lambda i, j: (i, j, 3)),
                  pl.BlockSpec((1, tt, 128), lambda i, j: (i, j, 24)),
                  _const((4, 2 * GW)), _const((1, 2 * GW)), _const((1, 128)), _const((1, 128)),
                  _const((1, GW)), _const((1, GW)),
                  _const((tt, tt)), _const((128, GW)), _const((128, GW))],
        out_specs=pl.BlockSpec((1, tt, GW), lambda i, j: (i, j, 0)),
        scratch_shapes=[pltpu.VMEM((SUBLANES, 2 * GW), F32), pltpu.VMEM((128, GW), F32)],
        compiler_params=_params(("parallel", "arbitrary")),
        name="mamba2",
    )(proj3, proj3, proj3, prm["conv_w"], prm["conv_b"], prm["dt_bias"], prm["a_neg"], prm["d_exp"],
      prm["norm_w"], consts["tri128"], consts["expand"], consts["gsel"])


HG_BLOCK = 16


def _hgrn_body(p_ref, lb_ref, nw_ref, bd_ref, hm_ref, hones_ref, tri_ref, o_ref,
               st_ref, q_s, k_s, v_s, lf_s, o_s):
    nb, tt, _ = p_ref.shape

    @pl.when(pl.program_id(1) == 0)
    def _():
        st_ref[...] = jnp.zeros_like(st_ref)

    p = p_ref[...].reshape(nb * tt, 4 * GW)
    lb = lb_ref[...]
    hf = p[:, GW:2 * GW]
    q_s[...] = _silu(p[:, :GW])
    k_s[...] = (1.0 - lb) * _sigmoid(-hf)
    v_s[...] = p[:, 2 * GW:3 * GW]
    lf_s[...] = jnp.log(jnp.maximum(lb + (1.0 - lb) * _sigmoid(hf), HG_F_FLOOR))

    def block(n, carry):
        seqs = range(nb)
        sls = [pl.ds(pl.multiple_of(b * tt + n * HG_BLOCK, HG_BLOCK), HG_BLOCK) for b in seqs]
        ld = lambda s: [s[sl, :] for sl in sls]
        q_b, k_b, v_b, lf_b = (ld(s) for s in (q_s, k_s, v_s, lf_s))
        g = [_mm_x(tri_ref[...], lf, 1, 3) for lf in lf_b]
        g_end = [x[HG_BLOCK - 1:HG_BLOCK, :] for x in g]
        state = [st_ref[b] for b in seqs]
        o = [_mm_nt(q * jnp.exp(x), s) for q, x, s in zip(q_b, g, state)]
        prods = [jnp.concatenate([q * jnp.exp(jnp.minimum(x - x[j:j + 1, :], 0.0)) * k[j:j + 1, :]
                                  for j in range(HG_BLOCK)], axis=0) for q, k, x in zip(q_b, k_b, g)]
        att = [_mm(p, hones_ref[...]) for p in prods]
        upd = [_mm_tn(v, k * jnp.exp(ge - x)) for v, k, ge, x in zip(v_b, k_b, g_end, g)]
        row = lax.broadcasted_iota(jnp.int32, (HG_BLOCK, GW), 0)
        for b in seqs:
            ob = o[b]
            for j in range(HG_BLOCK):
                ob = ob + jnp.where(row >= j, att[b][HG_BLOCK * j:HG_BLOCK * (j + 1), :], 0.0) * v_b[b][j:j + 1, :]
            o_s[sls[b], :] = ob
            st_ref[b] = state[b] * jnp.exp(g_end[b]) + upd[b] * bd_ref[...]
        return carry

    lax.fori_loop(0, tt // HG_BLOCK, block, 0)
    o = o_s[...]
    ms = _mm_x(o * o, hm_ref[...], 2, 1)
    o_ref[...] = (o * lax.rsqrt(ms + NORM_EPS) * nw_ref[...] * _silu(p[:, 3 * GW:])).reshape(nb, tt, GW)


def _hgrn(proj3, lb, nw, consts, nb=4, tt=128):
    b, l, _ = proj3.shape
    return pl.pallas_call(
        _hgrn_body,
        out_shape=jax.ShapeDtypeStruct((b, l, GW), F32),
        grid=(b // nb, l // tt),
        in_specs=[pl.BlockSpec((nb, tt, 4 * GW), lambda i, j: (i, j, 2)),
                  _const((1, GW)), _const((1, GW)), _const((GW, GW)), _const((GW, GW)),
                  _const((GW, GW)), _const((HG_BLOCK, HG_BLOCK))],
        out_specs=pl.BlockSpec((nb, tt, GW), lambda i, j: (i, j, 0)),
        scratch_shapes=[pltpu.VMEM((nb, GW, GW), F32)] + [pltpu.VMEM((nb * tt, GW), F32)] * 5,
        compiler_params=_params(("parallel", "arbitrary")),
        name="hgrn2",
    )(proj3, lb, nw, consts["bd"], consts["hm"], consts["hones"], consts["tri16"])


def _outproj_body(h_ref, y1_ref, y2_ref, y3_ref, y4_ref, wo_ref, ln2_ref, wr_ref, br_ref,
                  hn_ref, xn_ref, comb_ref):
    y = jnp.concatenate([y1_ref[...], y2_ref[...], y3_ref[...], y4_ref[...]], axis=-1)
    h = h_ref[...] + _mm(y, wo_ref[...])
    hn_ref[...] = h
    xn = _rms(h, ln2_ref[...])
    xn_ref[...] = xn.astype(BF16)
    xh, xl = _parts(xn, 2)
    wr = wr_ref[...]
    first = jnp.dot(xh, wr, preferred_element_type=F32)
    logits = (first[:, :ROUTE_LANES] + first[:, ROUTE_LANES:]
              + jnp.dot(xl, wr[:, :ROUTE_LANES], preferred_element_type=F32) + br_ref[...])
    lane = lax.broadcasted_iota(jnp.int32, logits.shape, 1)
    neg = -jnp.inf
    big = ROUTE_LANES
    glog = jnp.where(lane < N_EXPERT_GROUPS, logits, neg)
    gmax = jnp.max(glog, axis=-1, keepdims=True)
    g_w = 1.0 / jnp.sum(jnp.exp(glog - gmax), axis=-1, keepdims=True)
    g_idx = jnp.min(jnp.where(glog == gmax, lane, big), axis=-1, keepdims=True)
    lo = ROUTE_OFF + EXPERTS_PER_GROUP * g_idx
    elog = jnp.where((lane >= lo) & (lane < lo + EXPERTS_PER_GROUP), logits, neg)
    m1 = jnp.max(elog, axis=-1, keepdims=True)
    i1 = jnp.min(jnp.where(elog == m1, lane, big), axis=-1, keepdims=True)
    elog2 = jnp.where(lane == i1, neg, elog)
    m2 = jnp.max(elog2, axis=-1, keepdims=True)
    i2 = jnp.min(jnp.where(elog2 == m2, lane, big), axis=-1, keepdims=True)
    e2 = jnp.exp(m2 - m1)
    w1 = 1.0 / (1.0 + e2)
    w2 = e2 / (1.0 + e2)
    comb_ref[...] = g_w * (jnp.where(lane == i1, w1, 0.0) + jnp.where(lane == i2, w2, 0.0))


def _outproj(h, ys, wo, ln2, wr, br, tm=512):
    t, d = h.shape
    row = lambda w: pl.BlockSpec((tm, w), lambda i: (i, 0))
    return pl.pallas_call(
        _outproj_body,
        out_shape=(jax.ShapeDtypeStruct((t, d), F32), jax.ShapeDtypeStruct((t, d), BF16),
                   jax.ShapeDtypeStruct((t, ROUTE_LANES), F32)),
        grid=(t // tm,),
        in_specs=[row(d), row(GW), row(GW), row(GW), row(GW), _const((d, d)), _const((1, d)),
                  _const((d, 2 * ROUTE_LANES)), _const((1, ROUTE_LANES))],
        out_specs=(row(d), row(d), row(ROUTE_LANES)),
        compiler_params=_params(("parallel",)),
        name="outproj_router",
    )(h, *ys, wo, ln2, wr, br)


def _moe_body(x_ref, comb_ref, h_ref, wg_ref, wu_ref, wd_ref, o_ref, acc_ref):
    e = pl.program_id(1)

    @pl.when(e == 0)
    def _():
        acc_ref[...] = h_ref[...]

    x = x_ref[...]
    comb = comb_ref[...]
    lane = lax.broadcasted_iota(jnp.int32, comb.shape, 1)
    scale = jnp.sum(jnp.where(lane == e + ROUTE_OFF, comb, 0.0), axis=-1, keepdims=True)
    act = _silu(_mm(x, wg_ref[0])) * _mm(x, wu_ref[0]) * scale
    acc_ref[...] += _mm(act, wd_ref[0])

    @pl.when(e == N_EXPERTS - 1)
    def _():
        o_ref[...] = acc_ref[...]


def _moe(xn, comb, h, wg, wu, wd, tm=1024):
    t, d = h.shape
    de = wg.shape[-1]
    row = lambda w: pl.BlockSpec((tm, w), lambda i, e: (i, 0))
    return pl.pallas_call(
        _moe_body,
        out_shape=jax.ShapeDtypeStruct((t, d), F32),
        grid=(t // tm, N_EXPERTS),
        in_specs=[row(d), row(ROUTE_LANES), row(d),
                  pl.BlockSpec((1, d, de), lambda i, e: (e, 0, 0)),
                  pl.BlockSpec((1, d, de), lambda i, e: (e, 0, 0)),
                  pl.BlockSpec((1, de, d), lambda i, e: (e, 0, 0))],
        out_specs=row(d),
        scratch_shapes=[pltpu.VMEM((tm, d), F32)],
        compiler_params=_params(("parallel", "arbitrary")),
        name="moe",
    )(xn, comb, h, wg, wu, wd)


def _norm_body(h_ref, w_ref, o_ref):
    o_ref[...] = _rms(h_ref[...], w_ref[...])


def _final_norm(h, w, tm=1024):
    t, d = h.shape
    return pl.pallas_call(
        _norm_body,
        out_shape=jax.ShapeDtypeStruct((t, d), F32),
        grid=(t // tm,),
        in_specs=[pl.BlockSpec((tm, d), lambda i: (i, 0)), _const((1, d))],
        out_specs=pl.BlockSpec((tm, d), lambda i: (i, 0)),
        compiler_params=_params(("parallel",)),
        name="final_norm",
    )(h, w)


def _mask_consts():
    i256 = jnp.arange(GW)
    same_head = (i256[:, None] // HEAD_DIM) == (i256[None, :] // HEAD_DIM)
    t64 = jnp.arange(RW_CHUNK)
    s_w = i256 % RW_CHUNK
    h128 = jnp.arange(128)
    return {
        "bd": same_head.astype(F32),
        "hm": same_head.astype(F32) / HEAD_DIM,
        "hones": same_head.astype(BF16),
        "tri64": (t64[:, None] >= t64[None, :]).astype(F32),
        "lowi": (t64[:, None] >= s_w[None, :]).astype(F32),
        "lows": (t64[:, None] > s_w[None, :]).astype(F32),
        "eyew": (t64[:, None] == s_w[None, :]).astype(F32),
        "tri16": (jnp.arange(HG_BLOCK)[:, None] >= jnp.arange(HG_BLOCK)[None, :]).astype(F32),
        "tri128": (h128[:, None] >= h128[None, :]).astype(F32),
        "expand": (h128[:, None] == (i256[None, :] // HEAD_DIM)).astype(F32),
        "gsel": ((h128[:, None] // 64) == (i256[None, :] // 128)).astype(F32),
    }


def _s5_params(lam_re, lam_im, log_dt, b_re, b_im, c_re, c_im, d_skip, w_glu, b_glu):
    lr = jnp.minimum(lam_re, -1e-4)
    li = lam_im
    dt = jnp.exp(log_dt)[:, None]
    mag = jnp.exp(lr * dt)
    ar, ai = mag * jnp.cos(li * dt), mag * jnp.sin(li * dt)
    den = lr * lr + li * li
    nr = ar - 1.0
    er, ei = (nr * lr + ai * li) / den, (ai * lr - nr * li) / den
    bbr = er[..., None] * b_re - ei[..., None] * b_im
    bbi = er[..., None] * b_im + ei[..., None] * b_re
    eye = jnp.eye(lam_re.shape[0], dtype=F32)
    pack_b = lambda m: jnp.einsum("gph,gk->ghkp", m, eye).reshape(GW, S5_STATE_W)
    pack_c = lambda m: jnp.einsum("ghp,gk->gpkh", m, eye).reshape(S5_STATE_W, GW)
    return {
        "bbd": jnp.concatenate([pack_b(bbr), pack_b(bbi)], axis=1).astype(BF16),
        "cbd": jnp.concatenate([pack_c(c_re), -pack_c(c_im)], axis=0).astype(BF16),
        "ar": ar.reshape(1, S5_STATE_W), "ai": ai.reshape(1, S5_STATE_W),
        "d": d_skip.reshape(1, GW), "w_glu": w_glu.astype(BF16), "b_glu": b_glu.reshape(1, GW),
    }


def kernel(x, ln1_w, w_in, rw_mu, rw_w0, rw_w2, rw_a0, rw_a2, rw_g2, rw_k_k, rw_k_a, rw_r_k, rw_lnx_w, rw_lnx_b, s5_lam_re, s5_lam_im, s5_log_dt, s5_b_re, s5_b_im, s5_c_re, s5_c_im, s5_d, s5_w_glu, s5_b_glu, m_conv_w, m_conv_b, m_dt_bias, m_a_log, m_d, m_norm_w, hg_lb_logits, hg_norm_w, w_out, ln2_w, moe_w_rg, moe_b_rg, moe_w_re, moe_b_re, moe_w_gate, moe_w_up, moe_w_down, lnf_w):
    bsz, seq, d = x.shape
    depth = w_in.shape[0]
    consts = _mask_consts()
    lbs = jax.nn.softmax(hg_lb_logits.astype(F32), axis=0)
    lbs = jnp.cumsum(lbs, axis=0) - lbs[0:1]
    row = lambda v: v.reshape(1, -1).astype(F32)
    n_dt = N_HEADS
    h = x.reshape(bsz * seq, d)
    for l in range(depth):
        wa = w_in[l, :, :PROJ_MAIN].astype(BF16)
        wb = w_in[l, :, PROJ_MAIN + n_dt:].astype(BF16)
        wc = jnp.pad(w_in[l, :, PROJ_MAIN:PROJ_MAIN + n_dt].astype(BF16), ((0, 0), (0, LANES - n_dt)))
        proj = _inproj(h, row(ln1_w[l]), wa, wb, wc).reshape(bsz, seq, PROJ_PAD)
        rw = {"mu": row(rw_mu[l]), "w0": row(rw_w0[l]), "w2": rw_w2[l], "a0": row(rw_a0[l]),
              "a2": rw_a2[l], "g2": rw_g2[l], "k_k": row(rw_k_k[l]), "k_a": row(rw_k_a[l]),
              "r_k": row(rw_r_k[l]), "lnx_w": row(rw_lnx_w[l]), "lnx_b": row(rw_lnx_b[l])}
        y_rw = _rwkv(proj, rw, consts)
        y_s5 = _s5(proj, _s5_params(s5_lam_re[l], s5_lam_im[l], s5_log_dt[l], s5_b_re[l], s5_b_im[l],
                                    s5_c_re[l], s5_c_im[l], s5_d[l], s5_w_glu[l], s5_b_glu[l]))
        pad_h = lambda v: jnp.pad(v.astype(F32), (0, LANES - n_dt)).reshape(1, LANES)
        mp = {"conv_w": m_conv_w[l], "conv_b": row(m_conv_b[l]), "dt_bias": pad_h(m_dt_bias[l]),
              "a_neg": pad_h(-jnp.exp(m_a_log[l].astype(F32))),
              "d_exp": row(jnp.repeat(m_d[l], HEAD_DIM)), "norm_w": row(m_norm_w[l])}
        y_m = _mamba(proj, mp, consts)
        y_hg = _hgrn(proj, row(lbs[l]), row(hg_norm_w[l]), consts)
        ys = [y.reshape(bsz * seq, GW) for y in (y_rw, y_s5, y_m, y_hg)]
        n_route = N_EXPERT_GROUPS + N_EXPERTS
        wr = jnp.pad(jnp.concatenate([moe_w_rg[l], moe_w_re[l]], axis=1), ((0, 0), (0, ROUTE_LANES - n_route)))
        wr_hi = wr.astype(BF16)
        wr_lo = (wr - wr_hi.astype(F32)).astype(BF16)
        br = jnp.pad(jnp.concatenate([moe_b_rg[l], moe_b_re[l]]), (0, ROUTE_LANES - n_route)).reshape(1, -1)
        h, xn, comb = _outproj(h, ys, w_out[l].astype(BF16), row(ln2_w[l]),
                               jnp.concatenate([wr_hi, wr_lo], axis=1), br)
        de = moe_w_gate.shape[-1]
        h = _moe(xn, comb, h,
                 moe_w_gate[l].reshape(N_EXPERTS, d, de).astype(BF16),
                 moe_w_up[l].reshape(N_EXPERTS, d, de).astype(BF16),
                 moe_w_down[l].reshape(N_EXPERTS, de, d).astype(BF16))
    return _final_norm(h, row(lnf_w)).reshape(bsz, seq, d)
```

```python
import functools
import math

import jax
import jax.numpy as jnp
from jax import lax
from jax.experimental import pallas as pl
from jax.experimental.pallas import tpu as pltpu

F32 = jnp.float32
BF16 = jnp.bfloat16

NORM_EPS = 1e-6
GW = 256
HEAD_DIM = 64
N_HEADS = GW // HEAD_DIM
LANES = 128
SUBLANES = 8
RW_GN_EPS = 64e-5
HG_F_FLOOR = 1e-20
S5_STATE_W = 1024
M_CHUNK = 128
N_EXPERTS = 32
EXPERTS_PER_GROUP = 8
N_EXPERT_GROUPS = 4
ROUTE_LANES = LANES
ROUTE_OFF = N_EXPERT_GROUPS

PROJ_MAIN = 2048
PROJ_HG = 1024
PROJ_PAD = PROJ_MAIN + PROJ_HG + LANES
VMEM_LIMIT = 56 * 1024 * 1024


def _mm(a, b):
    return jnp.dot(a.astype(BF16), b.astype(BF16), preferred_element_type=F32)


def _mm_nt(a, b):
    return lax.dot_general(a.astype(BF16), b.astype(BF16), (((1,), (1,)), ((), ())),
                           preferred_element_type=F32)


def _mm_tn(a, b):
    return lax.dot_general(a.astype(BF16), b.astype(BF16), (((0,), (0,)), ((), ())),
                           preferred_element_type=F32)


def _parts(x, n):
    out, rem = [], x
    for i in range(n):
        p = rem.astype(BF16)
        out.append(p)
        if i + 1 < n:
            rem = rem - p.astype(F32)
    return out


def _mm_x(a, b, na, nb):
    pa, pb = _parts(a, na), _parts(b, nb)
    acc = None
    for i in range(na):
        for j in range(nb):
            if i + j < max(na, nb):
                t = jnp.dot(pa[i], pb[j], preferred_element_type=F32)
                acc = t if acc is None else acc + t
    return acc


def _sigmoid(x):
    return 1.0 / (1.0 + jnp.exp(-x))


def _silu(x):
    return x * _sigmoid(x)


def _softplus(x):
    return jnp.maximum(x, 0.0) + jnp.log1p(jnp.exp(-jnp.abs(x)))


def _rms(x, w):
    ms = jnp.mean(x * x, axis=-1, keepdims=True)
    return x * lax.rsqrt(ms + NORM_EPS) * w


def _stack4(x, bd16):
    xb = x.astype(BF16)
    return jnp.concatenate([xb, xb, xb, xb], axis=0) * bd16


def _const(shape):
    return pl.BlockSpec(shape, lambda *_: (0,) * len(shape))


def _params(sem):
    return pltpu.CompilerParams(dimension_semantics=sem, vmem_limit_bytes=VMEM_LIMIT)


def _inproj_body(*refs, with_delta):
    if with_delta:
        h_ref, dl_ref, lnw_ref, wa_ref, wb_ref, wc_ref, hn_ref, o_ref = refs
        h = h_ref[...] + dl_ref[...]
        hn_ref[...] = h
    else:
        h_ref, lnw_ref, wa_ref, wb_ref, wc_ref, o_ref = refs
        h = h_ref[...]
    xn = _rms(h, lnw_ref[...]).astype(BF16)
    o_ref[:, :PROJ_MAIN] = jnp.dot(xn, wa_ref[...], preferred_element_type=F32)
    o_ref[:, PROJ_MAIN:PROJ_MAIN + PROJ_HG] = jnp.dot(xn, wb_ref[...], preferred_element_type=F32)
    o_ref[:, PROJ_MAIN + PROJ_HG:] = jnp.dot(xn, wc_ref[...], preferred_element_type=F32)


def _inproj(h, delta, lnw, wa, wb, wc, tm=512):
    t, d = h.shape
    row = pl.BlockSpec((tm, d), lambda i: (i, 0))
    proj_spec = pl.BlockSpec((tm, PROJ_PAD), lambda i: (i, 0))
    proj_shape = jax.ShapeDtypeStruct((t, PROJ_PAD), F32)
    weights = [_const((1, d)), _const((d, PROJ_MAIN)), _const((d, PROJ_HG)), _const((d, LANES))]
    with_delta = delta is not None
    out = pl.pallas_call(
        functools.partial(_inproj_body, with_delta=with_delta),
        out_shape=(jax.ShapeDtypeStruct((t, d), F32), proj_shape) if with_delta else proj_shape,
        grid=(t // tm,),
        in_specs=([row, row] if with_delta else [row]) + weights,
        out_specs=(row, proj_spec) if with_delta else proj_spec,
        compiler_params=_params(("parallel",)),
        name="inproj",
    )(*((h, delta) if with_delta else (h,)), lnw, wa, wb, wc)
    return out if with_delta else (h, out)


RW_CHUNK = 64


def _rwkv_body(p_ref, mu_ref, w0_ref, w2_ref, a0_ref, a2_ref, g2_ref, kk_ref, ka_ref, rk_ref,
               lnw_ref, lnb_ref, bd_ref, bd16_ref, hm_ref, tri_ref, lowi_ref, lows_ref, eye_ref,
               o_ref, carry_ref, st_ref, r_s, k_s, v_s, kn_s, kb_s, wl_s, y_s):
    nb, tt, _ = p_ref.shape

    @pl.when(pl.program_id(1) == 0)
    def _():
        carry_ref[...] = jnp.zeros_like(carry_ref)
        st_ref[...] = jnp.zeros_like(st_ref)

    rows = lax.broadcasted_iota(jnp.int32, (tt, 4 * GW), 0)
    mixed = []
    for b in range(nb):
        p = p_ref[b]
        prev = jnp.where(rows == 0, carry_ref[b, 0:1, :], pltpu.roll(p, 1, axis=0))
        carry_ref[b, 0:1, :] = p[tt - 1:tt, :]
        mixed.append(p + (prev - p) * mu_ref[...])
    p = jnp.concatenate(mixed, axis=0)
    r = p[:, 0:GW]
    k = p[:, GW:2 * GW]
    v = p[:, 2 * GW:3 * GW]
    wl = p[:, 3 * GW:3 * GW + 64]
    al = p[:, 3 * GW + 64:3 * GW + 128]
    gl = p[:, 3 * GW + 128:]
    hm = hm_ref[...]
    w_log = -jnp.exp(-_softplus(-(w0_ref[...] + _mm_x(jnp.tanh(wl), w2_ref[...], 2, 2))) - 0.5)
    a = _sigmoid(a0_ref[...] + _mm(al, a2_ref[...]))
    gate = _mm(_sigmoid(gl), g2_ref[...])
    kn = k * kk_ref[...]
    nrm = jnp.sqrt(_mm_x(kn * kn, hm, 2, 1) * float(HEAD_DIM))
    kn = kn / jnp.maximum(nrm, 1e-12)
    k = k * (1.0 + (a - 1.0) * ka_ref[...])
    r_s[...] = r
    k_s[...] = k
    v_s[...] = v
    kn_s[...] = kn
    kb_s[...] = kn * a
    wl_s[...] = w_log

    def chunk(c, carry):
        seqs = range(nb)
        sls = [pl.ds(pl.multiple_of(b * tt + c * RW_CHUNK, RW_CHUNK), RW_CHUNK) for b in seqs]
        ld = lambda s: [s[sl, :] for sl in sls]
        r_c, k_c, v_c, kn_c, kb_c, wl_c = (ld(s) for s in (r_s, k_s, v_s, kn_s, kb_s, wl_s))
        bd = bd_ref[...]
        bd16 = bd16_ref[...]
        lows = lows_ref[...]
        lowi = lowi_ref[...]
        st4 = lambda xs: [_stack4(x, bd16) for x in xs]
        gc = [_mm_x(tri_ref[...], w, 1, 3) for w in wl_c]
        g_end = [g[RW_CHUNK - 1:RW_CHUNK, :] for g in gc]
        inv = [jnp.exp(-g) for g in gc]
        lhs = [jnp.concatenate([-kn * jnp.exp(g - w), r * jnp.exp(g)], axis=0)
               for kn, r, g, w in zip(kn_c, r_c, gc, wl_c)]
        rhs = [jnp.concatenate([_stack4(kb * i, bd16), _stack4(k * i, bd16)], axis=0)
               for kb, k, i in zip(kb_c, k_c, inv)]
        gram = [_mm_nt(a, b) for a, b in zip(lhs, rhs)]
        state = [st_ref[b] for b in seqs]
        from_state = [_mm_nt(a, s) for a, s in zip(lhs, state)]
        a_ab = [g[:RW_CHUNK, :GW] * lows for g in gram]
        pw = [eye_ref[...] + a for a in a_ab]
        q = a_ab
        qs = st4(q)
        for _ in range(5):
            q = [_mm(a, b) for a, b in zip(q, qs)]
            qs = st4(q)
            pw = [p + _mm(p, b) for p, b in zip(pw, qs)]
        vst = st4(v_c)
        rhs_u = [f[:RW_CHUNK] + _mm(g[:RW_CHUNK, GW:] * lows, vs) for f, g, vs in zip(from_state, gram, vst)]
        u = [_mm(p, x) for p, x in zip(pw, st4(rhs_u))]
        ust = st4(u)
        for b in seqs:
            y_s[sls[b], :] = (from_state[b][RW_CHUNK:] + _mm(gram[b][RW_CHUNK:, :GW] * lowi, ust[b])
                              + _mm(gram[b][RW_CHUNK:, GW:] * lowi, vst[b]))
        for b in seqs:
            to_end = jnp.exp(g_end[b] - gc[b])
            upd = _mm_tn(jnp.concatenate([u[b], v_c[b]], axis=0),
                         jnp.concatenate([kb_c[b] * to_end, k_c[b] * to_end], axis=0))
            st_ref[b] = state[b] * jnp.exp(g_end[b]) + upd * bd
        return carry

    lax.fori_loop(0, tt // RW_CHUNK, chunk, 0)

    y = y_s[...]
    mean = _mm_x(y, hm, 2, 1)
    d = y - mean
    var = _mm_x(d * d, hm, 2, 1)
    yn = d * lax.rsqrt(var + RW_GN_EPS) * lnw_ref[...] + lnb_ref[...]
    bonus = _mm_x(r * k * rk_ref[...], hm, 2, 1) * float(HEAD_DIM) * v
    o_ref[...] = ((yn + bonus) * gate).reshape(nb, tt, GW)


def _rwkv(proj3, prm, consts, nb=4, tt=128):
    b, l, _ = proj3.shape
    vec = _const((1, GW))
    scr = pltpu.VMEM((nb * tt, GW), F32)
    return pl.pallas_call(
        _rwkv_body,
        out_shape=jax.ShapeDtypeStruct((b, l, GW), F32),
        grid=(b // nb, l // tt),
        in_specs=[pl.BlockSpec((nb, tt, 4 * GW), lambda i, j: (i, j, 0)),
                  _const((1, 4 * GW)), vec, _const((64, GW)), vec, _const((64, GW)),
                  _const((128, GW)), vec, vec, vec, vec, vec,
                  _const((GW, GW)), _const((GW, GW)), _const((GW, GW)), _const((RW_CHUNK, RW_CHUNK)),
                  _const((RW_CHUNK, GW)), _const((RW_CHUNK, GW)), _const((RW_CHUNK, GW))],
        out_specs=pl.BlockSpec((nb, tt, GW), lambda i, j: (i, j, 0)),
        scratch_shapes=[pltpu.VMEM((nb, SUBLANES, 4 * GW), F32), pltpu.VMEM((nb, GW, GW), F32)] + [scr] * 7,
        compiler_params=_params(("parallel", "arbitrary")),
        name="rwkv7",
    )(proj3, prm["mu"], prm["w0"], prm["w2"], prm["a0"], prm["a2"], prm["g2"], prm["k_k"],
      prm["k_a"], prm["r_k"], prm["lnx_w"], prm["lnx_b"],
      consts["bd"], consts["hones"], consts["hm"], consts["tri64"], consts["lowi"], consts["lows"], consts["eyew"])


def _s5_body(u_ref, bbd_ref, ar_ref, ai_ref, cbd_ref, d_ref, wglu_ref, bglu_ref, o_ref,
             ub_s, ut_s, xr_s, xi_s, ot_s, st_ref):
    nb, tt, w = u_ref.shape
    ncb = w // LANES
    assert nb == SUBLANES

    @pl.when(pl.program_id(0) == 0)
    def _():
        st_ref[...] = jnp.zeros_like(st_ref)

    for cb in range(ncb):
        ub_s[cb] = u_ref[:, :, cb * LANES:(cb + 1) * LANES].reshape(nb * tt, LANES)

    def regroup(t, carry):
        for cb in range(ncb):
            ut_s[cb, pl.ds(pl.multiple_of(t * nb, nb), nb), :] = ub_s[cb, pl.ds(t, nb, stride=tt), :]
        return carry

    lax.fori_loop(0, tt, regroup, 0, unroll=8)
    u = jnp.concatenate([ut_s[cb] for cb in range(ncb)], axis=-1)
    bu = _mm(u, bbd_ref[...])
    xr_s[...] = bu[:, :S5_STATE_W]
    xi_s[...] = bu[:, S5_STATE_W:]
    ar = jnp.broadcast_to(ar_ref[...], (nb, S5_STATE_W))
    ai = jnp.broadcast_to(ai_ref[...], (nb, S5_STATE_W))

    def step(t, carry):
        xr, xi = carry
        rows = pl.ds(pl.multiple_of(t * nb, nb), nb)
        nr = ar * xr - ai * xi + xr_s[rows, :]
        ni = ar * xi + ai * xr + xi_s[rows, :]
        xr_s[rows, :] = nr
        xi_s[rows, :] = ni
        return nr, ni

    xr, xi = lax.fori_loop(0, tt, step, (st_ref[0], st_ref[1]), unroll=4)
    st_ref[0] = xr
    st_ref[1] = xi
    cbd = cbd_ref[...]
    y = _mm(xr_s[...], cbd[:S5_STATE_W]) + _mm(xi_s[...], cbd[S5_STATE_W:]) + d_ref[...] * u
    y = 0.5 * y * (1.0 + jnp.tanh(math.sqrt(2.0 / math.pi) * (y + 0.044715 * (y * y * y))))
    z = _mm(y, wglu_ref[...]) + bglu_ref[...]
    out = y * _sigmoid(z)
    for cb in range(ncb):
        ot_s[cb] = out[:, cb * LANES:(cb + 1) * LANES]
    for b in range(nb):
        for cb in range(ncb):
            o_ref[b, :, cb * LANES:(cb + 1) * LANES] = ot_s[cb, pl.ds(b, tt, stride=nb), :]


def _s5(proj3, prm, tt=128):
    b, l, _ = proj3.shape
    slab = pltpu.VMEM((GW // LANES, b * tt, LANES), F32)
    wide = pltpu.VMEM((b * tt, S5_STATE_W), F32)
    return pl.pallas_call(
        _s5_body,
        out_shape=jax.ShapeDtypeStruct((b, l, GW), F32),
        grid=(l // tt,),
        in_specs=[pl.BlockSpec((b, tt, GW), lambda j: (0, j, 4)),
                  _const((GW, 2 * S5_STATE_W)), _const((1, S5_STATE_W)), _const((1, S5_STATE_W)),
                  _const((2 * S5_STATE_W, GW)), _const((1, GW)), _const((GW, GW)), _const((1, GW))],
        out_specs=pl.BlockSpec((b, tt, GW), lambda j: (0, j, 0)),
        scratch_shapes=[slab, slab, wide, wide, slab, pltpu.VMEM((2, b, S5_STATE_W), F32)],
        compiler_params=_params(("arbitrary",)),
        name="s5",
    )(proj3, prm["bbd"], prm["ar"], prm["ai"], prm["cbd"], prm["d"], prm["w_glu"], prm["b_glu"])


def _mamba_body(z_ref, xbc_ref, dt_ref, cw_ref, cb_ref, dtb_ref, aneg_ref, dexp_ref, nw_ref,
                tri_ref, exp_ref, gsel_ref, o_ref, carry_ref, st_ref):
    @pl.when(pl.program_id(1) == 0)
    def _():
        carry_ref[...] = jnp.zeros_like(carry_ref)
        st_ref[...] = jnp.zeros_like(st_ref)

    xbc = xbc_ref[0]
    tt, cw = xbc.shape
    rows = lax.broadcasted_iota(jnp.int32, xbc.shape, 0)
    tail = carry_ref[...]
    pad = jnp.zeros((tt - SUBLANES, cw), F32)
    conv = xbc * cw_ref[3:4, :]
    for s in (1, 2, 3):
        head = jnp.concatenate([pltpu.roll(tail, s, axis=0), pad], axis=0)
        shifted = jnp.where(rows < s, head, pltpu.roll(xbc, s, axis=0))
        conv = conv + shifted * cw_ref[3 - s:4 - s, :]
    carry_ref[...] = xbc[tt - SUBLANES:, :]
    xc = _silu(conv + cb_ref[...])
    xs = xc[:, :GW]
    bm = xc[:, GW:GW + 128]
    cm = xc[:, GW + 128:]
    expand = exp_ref[...]
    dt = _softplus(dt_ref[0] + dtb_ref[...])
    cs = _mm_x(tri_ref[...], dt * aneg_ref[...], 1, 3)
    cs_t = cs.T
    cs_end = cs[tt - 1:tt, :]
    xdt = xs * _mm_x(dt, expand, 3, 1)
    lane = lax.broadcasted_iota(jnp.int32, (tt, 128), 1)
    tril = lax.broadcasted_iota(jnp.int32, (tt, tt), 0) >= lax.broadcasted_iota(jnp.int32, (tt, tt), 1)
    lane_w = lax.broadcasted_iota(jnp.int32, (tt, GW), 1)
    y = jnp.zeros((tt, GW), F32)
    for g in range(2):
        cb = _mm_nt(jnp.where(lane // 64 == g, cm, 0.0), bm)
        for h in (2 * g, 2 * g + 1):
            decay = jnp.where(tril, jnp.exp(jnp.minimum(cs[:, h:h + 1] - cs_t[h:h + 1, :], 0.0)), 0.0)
            y = y + jnp.where(lane_w // HEAD_DIM == h, _mm(cb * decay, xdt), 0.0)
    state = st_ref[...]
    cs_all = _mm_x(jnp.concatenate([cs, cs_end - cs, jnp.broadcast_to(cs_end, (SUBLANES, 128))], axis=0),
                   expand, 3, 1)
    y = y + _mm(cm, state) * jnp.exp(cs_all[:tt])
    y = y + dexp_ref[...] * xs
    to_end = jnp.exp(cs_all[tt:2 * tt])
    st_ref[...] = state * jnp.exp(cs_all[2 * tt:2 * tt + 1]) + _mm_tn(bm, xdt * to_end) * gsel_ref[...]
    y = y * _silu(z_ref[0])
    nw = nw_ref[...]
    half = GW // 2
    o_ref[0] = jnp.concatenate([_rms(y[:, :half], nw[:, :half]), _rms(y[:, half:], nw[:, half:])], axis=-1)


def _mamba(proj3, prm, consts):
    b, l, _ = proj3.shape
    tt = M_CHUNK
    return pl.pallas_call(
        _mamba_body,
        out_shape=jax.ShapeDtypeStruct((b, l, GW), F32),
        grid=(b, l // tt),
        in_specs=[pl.BlockSpec((1, tt, GW), lambda i, j: (i, j, 5)),
                  pl.BlockSpec((1, tt, 2 * GW), lambda i, j: (i, j, 3)),
                  pl.BlockSpec((1, tt, 128), lambda i, j: (i, j, 24)),
                  _const((4, 2 * GW)), _const((1, 2 * GW)), _const((1, 128)), _const((1, 128)),
                  _const((1, GW)), _const((1, GW)),
                  _const((tt, tt)), _const((128, GW)), _const((128, GW))],
        out_specs=pl.BlockSpec((1, tt, GW), lambda i, j: (i, j, 0)),
        scratch_shapes=[pltpu.VMEM((SUBLANES, 2 * GW), F32), pltpu.VMEM((128, GW), F32)],
        compiler_params=_params(("parallel", "arbitrary")),
        name="mamba2",
    )(proj3, proj3, proj3, prm["conv_w"], prm["conv_b"], prm["dt_bias"], prm["a_neg"], prm["d_exp"],
      prm["norm_w"], consts["tri128"], consts["expand"], consts["gsel"])


HG_BLOCK = 16


def _hgrn_body(p_ref, lb_ref, nw_ref, bd_ref, hm_ref, hones_ref, tri_ref, o_ref,
               st_ref, q_s, k_s, v_s, lf_s, o_s):
    nb, tt, _ = p_ref.shape

    @pl.when(pl.program_id(1) == 0)
    def _():
        st_ref[...] = jnp.zeros_like(st_ref)

    p = p_ref[...].reshape(nb * tt, 4 * GW)
    lb = lb_ref[...]
    hf = p[:, GW:2 * GW]
    q_s[...] = _silu(p[:, :GW])
    k_s[...] = (1.0 - lb) * _sigmoid(-hf)
    v_s[...] = p[:, 2 * GW:3 * GW]
    lf_s[...] = jnp.log(jnp.maximum(lb + (1.0 - lb) * _sigmoid(hf), HG_F_FLOOR))

    def block(n, carry):
        seqs = range(nb)
        sls = [pl.ds(pl.multiple_of(b * tt + n * HG_BLOCK, HG_BLOCK), HG_BLOCK) for b in seqs]
        ld = lambda s: [s[sl, :] for sl in sls]
        q_b, k_b, v_b, lf_b = (ld(s) for s in (q_s, k_s, v_s, lf_s))
        g = [_mm_x(tri_ref[...], lf, 1, 3) for lf in lf_b]
        g_end = [x[HG_BLOCK - 1:HG_BLOCK, :] for x in g]
        state = [st_ref[b] for b in seqs]
        o = [_mm_nt(q * jnp.exp(x), s) for q, x, s in zip(q_b, g, state)]
        prods = [jnp.concatenate([q * jnp.exp(jnp.minimum(x - x[j:j + 1, :], 0.0)) * k[j:j + 1, :]
                                  for j in range(HG_BLOCK)], axis=0) for q, k, x in zip(q_b, k_b, g)]
        att = [_mm(p, hones_ref[...]) for p in prods]
        upd = [_mm_tn(v, k * jnp.exp(ge - x)) for v, k, ge, x in zip(v_b, k_b, g_end, g)]
        row = lax.broadcasted_iota(jnp.int32, (HG_BLOCK, GW), 0)
        for b in seqs:
            ob = o[b]
            for j in range(HG_BLOCK):
                ob = ob + jnp.where(row >= j, att[b][HG_BLOCK * j:HG_BLOCK * (j + 1), :], 0.0) * v_b[b][j:j + 1, :]
            o_s[sls[b], :] = ob
            st_ref[b] = state[b] * jnp.exp(g_end[b]) + upd[b] * bd_ref[...]
        return carry

    lax.fori_loop(0, tt // HG_BLOCK, block, 0)
    o = o_s[...]
    ms = _mm_x(o * o, hm_ref[...], 2, 1)
    o_ref[...] = (o * lax.rsqrt(ms + NORM_EPS) * nw_ref[...] * _silu(p[:, 3 * GW:])).reshape(nb, tt, GW)


def _hgrn(proj3, lb, nw, consts, nb=4, tt=128):
    b, l, _ = proj3.shape
    return pl.pallas_call(
        _hgrn_body,
        out_shape=jax.ShapeDtypeStruct((b, l, GW), F32),
        grid=(b // nb, l // tt),
        in_specs=[pl.BlockSpec((nb, tt, 4 * GW), lambda i, j: (i, j, 2)),
                  _const((1, GW)), _const((1, GW)), _const((GW, GW)), _const((GW, GW)),
                  _const((GW, GW)), _const((HG_BLOCK, HG_BLOCK))],
        out_specs=pl.BlockSpec((nb, tt, GW), lambda i, j: (i, j, 0)),
        scratch_shapes=[pltpu.VMEM((nb, GW, GW), F32)] + [pltpu.VMEM((nb * tt, GW), F32)] * 5,
        compiler_params=_params(("parallel", "arbitrary")),
        name="hgrn2",
    )(proj3, lb, nw, consts["bd"], consts["hm"], consts["hones"], consts["tri16"])


def _outproj_body(h_ref, y1_ref, y2_ref, y3_ref, y4_ref, wo_ref, ln2_ref, wr_ref, br_ref,
                  hn_ref, xn_ref, comb_ref):
    y = jnp.concatenate([y1_ref[...], y2_ref[...], y3_ref[...], y4_ref[...]], axis=-1)
    h = h_ref[...] + _mm(y, wo_ref[...])
    hn_ref[...] = h
    xn = _rms(h, ln2_ref[...])
    xn_ref[...] = xn.astype(BF16)
    xh, xl = _parts(xn, 2)
    wr = wr_ref[...]
    first = jnp.dot(xh, wr, preferred_element_type=F32)
    logits = (first[:, :ROUTE_LANES] + first[:, ROUTE_LANES:]
              + jnp.dot(xl, wr[:, :ROUTE_LANES], preferred_element_type=F32) + br_ref[...])
    lane = lax.broadcasted_iota(jnp.int32, logits.shape, 1)
    neg = -jnp.inf
    big = ROUTE_LANES
    glog = jnp.where(lane < N_EXPERT_GROUPS, logits, neg)
    gmax = jnp.max(glog, axis=-1, keepdims=True)
    g_w = 1.0 / jnp.sum(jnp.exp(glog - gmax), axis=-1, keepdims=True)
    g_idx = jnp.min(jnp.where(glog == gmax, lane, big), axis=-1, keepdims=True)
    lo = ROUTE_OFF + EXPERTS_PER_GROUP * g_idx
    elog = jnp.where((lane >= lo) & (lane < lo + EXPERTS_PER_GROUP), logits, neg)
    m1 = jnp.max(elog, axis=-1, keepdims=True)
    i1 = jnp.min(jnp.where(elog == m1, lane, big), axis=-1, keepdims=True)
    elog2 = jnp.where(lane == i1, neg, elog)
    m2 = jnp.max(elog2, axis=-1, keepdims=True)
    i2 = jnp.min(jnp.where(elog2 == m2, lane, big), axis=-1, keepdims=True)
    e2 = jnp.exp(m2 - m1)
    w1 = 1.0 / (1.0 + e2)
    w2 = e2 / (1.0 + e2)
    comb_ref[...] = (g_w * (jnp.where(lane == i1 - lo, w1, 0.0) + jnp.where(lane == i2 - lo, w2, 0.0))
                     + jnp.where(lane == EXPERTS_PER_GROUP, g_idx.astype(F32), 0.0))


def _outproj(h, ys, wo, ln2, wr, br, tm=512):
    t, d = h.shape
    row = lambda w: pl.BlockSpec((tm, w), lambda i: (i, 0))
    return pl.pallas_call(
        _outproj_body,
        out_shape=(jax.ShapeDtypeStruct((t, d), F32), jax.ShapeDtypeStruct((t, d), BF16),
                   jax.ShapeDtypeStruct((t, ROUTE_LANES), F32)),
        grid=(t // tm,),
        in_specs=[row(d), row(GW), row(GW), row(GW), row(GW), _const((d, d)), _const((1, d)),
                  _const((d, 2 * ROUTE_LANES)), _const((1, ROUTE_LANES))],
        out_specs=(row(d), row(d), row(ROUTE_LANES)),
        compiler_params=_params(("parallel",)),
        name="outproj_router",
    )(h, *ys, wo, ln2, wr, br)


MOE_SUB = 256
MOE_ROWS = 80
COMB_GROUP_LANE = EXPERTS_PER_GROUP


def _moe_body(x_ref, comb_ref, wg_ref, wu_ref, wd_ref, o_ref, tri_s, key_s, cp_s):
    i = pl.program_id(0)
    g = pl.program_id(1)
    tm = x_ref.shape[0]
    nsub = tm // MOE_SUB
    subs = [slice(s * MOE_SUB, (s + 1) * MOE_SUB) for s in range(nsub)]

    @pl.when((i == 0) & (g == 0))
    def _():
        r = lax.broadcasted_iota(jnp.int32, (MOE_SUB, MOE_SUB), 0)
        c = lax.broadcasted_iota(jnp.int32, (MOE_SUB, MOE_SUB), 1)
        tri_s[...] = jnp.where(r > c, 1.0, 0.0).astype(BF16)

    @pl.when(g == 0)
    def _():
        comb = comb_ref[...]
        lane = lax.broadcasted_iota(jnp.int32, comb.shape, 1)
        gcol = comb[:, COMB_GROUP_LANE:COMB_GROUP_LANE + 1]
        onehot = jnp.where((lane < N_EXPERT_GROUPS) & (gcol == lane.astype(F32)), 1.0, 0.0)
        for rows in subs:
            before = jnp.dot(tri_s[...], onehot[rows].astype(BF16), preferred_element_type=F32)
            own = jnp.sum(before * onehot[rows], axis=-1, keepdims=True)
            key_s[rows, :] = jnp.broadcast_to(own, (MOE_SUB, ROUTE_LANES))
        for k, part in enumerate(_parts(comb, 3)):
            cp_s[k] = part
        o_ref[...] = jnp.zeros_like(o_ref)

    member = comb_ref[:, COMB_GROUP_LANE:COMB_GROUP_LANE + 1] == g.astype(F32)
    key = jnp.where(member, key_s[:, 0:1], -1.0)
    count = jnp.sum(jnp.where(member[subs[0]], 1.0, 0.0))
    for rows in subs[1:]:
        count = jnp.maximum(count, jnp.sum(jnp.where(member[rows], 1.0, 0.0)))
    count = count.astype(jnp.int32)
    slot = lax.broadcasted_iota(jnp.int32, (MOE_SUB, MOE_ROWS), 1).astype(F32)
    tn = lambda a, b: lax.dot_general(a, b, (((0,), (0,)), ((), ())), preferred_element_type=F32)

    def block(blk, carry):
        base = (blk * MOE_ROWS).astype(F32)
        pts = [jnp.where(key[rows] - base == slot, 1.0, 0.0).astype(BF16) for rows in subs]
        xg = jnp.concatenate([tn(pt, x_ref[rows, :]) for pt, rows in zip(pts, subs)], axis=0).astype(BF16)
        cg = jnp.concatenate([tn(pt, cp_s[0, rows, :]) + tn(pt, cp_s[1, rows, :]) + tn(pt, cp_s[2, rows, :])
                              for pt, rows in zip(pts, subs)], axis=0)
        y = jnp.zeros((nsub * MOE_ROWS, x_ref.shape[1]), F32)
        for e in range(EXPERTS_PER_GROUP):
            act = (_silu(jnp.dot(xg, wg_ref[0, e], preferred_element_type=F32))
                   * jnp.dot(xg, wu_ref[0, e], preferred_element_type=F32) * cg[:, e:e + 1])
            y = y + _mm(act, wd_ref[0, e])
        yh, yl = _parts(y, 2)
        for s, (pt, rows) in enumerate(zip(pts, subs)):
            part = slice(s * MOE_ROWS, (s + 1) * MOE_ROWS)
            o_ref[rows, :] += (jnp.dot(pt, yh[part], preferred_element_type=F32)
                               + jnp.dot(pt, yl[part], preferred_element_type=F32))
        return carry

    lax.fori_loop(0, (count + MOE_ROWS - 1) // MOE_ROWS, block, 0)


def _moe(xn, comb, wg, wu, wd, tm=1024):
    t, d = xn.shape
    ng, eg, _, de = wg.shape
    row = lambda w: pl.BlockSpec((tm, w), lambda i, g: (i, 0))
    return pl.pallas_call(
        _moe_body,
        out_shape=jax.ShapeDtypeStruct((t, d), F32),
        grid=(t // tm, ng),
        in_specs=[row(d), row(ROUTE_LANES),
                  pl.BlockSpec((1, eg, d, de), lambda i, g: (g, 0, 0, 0)),
                  pl.BlockSpec((1, eg, d, de), lambda i, g: (g, 0, 0, 0)),
                  pl.BlockSpec((1, eg, de, d), lambda i, g: (g, 0, 0, 0))],
        out_specs=row(d),
        scratch_shapes=[pltpu.VMEM((MOE_SUB, MOE_SUB), BF16), pltpu.VMEM((tm, ROUTE_LANES), F32),
                        pltpu.VMEM((3, tm, ROUTE_LANES), BF16)],
        compiler_params=_params(("arbitrary", "arbitrary")),
        name="moe",
    )(xn, comb, wg, wu, wd)


def _norm_body(h_ref, dl_ref, w_ref, o_ref):
    o_ref[...] = _rms(h_ref[...] + dl_ref[...], w_ref[...])


def _final_norm(h, delta, w, tm=1024):
    t, d = h.shape
    row = pl.BlockSpec((tm, d), lambda i: (i, 0))
    return pl.pallas_call(
        _norm_body,
        out_shape=jax.ShapeDtypeStruct((t, d), F32),
        grid=(t // tm,),
        in_specs=[row, row, _const((1, d))],
        out_specs=row,
        compiler_params=_params(("parallel",)),
        name="final_norm",
    )(h, delta, w)


def _mask_consts():
    i256 = jnp.arange(GW)
    same_head = (i256[:, None] // HEAD_DIM) == (i256[None, :] // HEAD_DIM)
    t64 = jnp.arange(RW_CHUNK)
    s_w = i256 % RW_CHUNK
    h128 = jnp.arange(128)
    return {
        "bd": same_head.astype(F32),
        "hm": same_head.astype(F32) / HEAD_DIM,
        "hones": same_head.astype(BF16),
        "tri64": (t64[:, None] >= t64[None, :]).astype(F32),
        "lowi": (t64[:, None] >= s_w[None, :]).astype(F32),
        "lows": (t64[:, None] > s_w[None, :]).astype(F32),
        "eyew": (t64[:, None] == s_w[None, :]).astype(F32),
        "tri16": (jnp.arange(HG_BLOCK)[:, None] >= jnp.arange(HG_BLOCK)[None, :]).astype(F32),
        "tri128": (h128[:, None] >= h128[None, :]).astype(F32),
        "expand": (h128[:, None] == (i256[None, :] // HEAD_DIM)).astype(F32),
        "gsel": ((h128[:, None] // 64) == (i256[None, :] // 128)).astype(F32),
    }


def _s5_params(lam_re, lam_im, log_dt, b_re, b_im, c_re, c_im, d_skip, w_glu, b_glu):
    lr = jnp.minimum(lam_re, -1e-4)
    li = lam_im
    dt = jnp.exp(log_dt)[:, None]
    mag = jnp.exp(lr * dt)
    ar, ai = mag * jnp.cos(li * dt), mag * jnp.sin(li * dt)
    den = lr * lr + li * li
    nr = ar - 1.0
    er, ei = (nr * lr + ai * li) / den, (ai * lr - nr * li) / den
    bbr = er[..., None] * b_re - ei[..., None] * b_im
    bbi = er[..., None] * b_im + ei[..., None] * b_re
    eye = jnp.eye(lam_re.shape[0], dtype=F32)
    pack_b = lambda m: jnp.einsum("gph,gk->ghkp", m, eye).reshape(GW, S5_STATE_W)
    pack_c = lambda m: jnp.einsum("ghp,gk->gpkh", m, eye).reshape(S5_STATE_W, GW)
    return {
        "bbd": jnp.concatenate([pack_b(bbr), pack_b(bbi)], axis=1).astype(BF16),
        "cbd": jnp.concatenate([pack_c(c_re), -pack_c(c_im)], axis=0).astype(BF16),
        "ar": ar.reshape(1, S5_STATE_W), "ai": ai.reshape(1, S5_STATE_W),
        "d": d_skip.reshape(1, GW), "w_glu": w_glu.astype(BF16), "b_glu": b_glu.reshape(1, GW),
    }


def kernel(x, ln1_w, w_in, rw_mu, rw_w0, rw_w2, rw_a0, rw_a2, rw_g2, rw_k_k, rw_k_a, rw_r_k, rw_lnx_w, rw_lnx_b, s5_lam_re, s5_lam_im, s5_log_dt, s5_b_re, s5_b_im, s5_c_re, s5_c_im, s5_d, s5_w_glu, s5_b_glu, m_conv_w, m_conv_b, m_dt_bias, m_a_log, m_d, m_norm_w, hg_lb_logits, hg_norm_w, w_out, ln2_w, moe_w_rg, moe_b_rg, moe_w_re, moe_b_re, moe_w_gate, moe_w_up, moe_w_down, lnf_w):
    bsz, seq, d = x.shape
    depth = w_in.shape[0]
    consts = _mask_consts()
    lbs = jax.nn.softmax(hg_lb_logits.astype(F32), axis=0)
    lbs = jnp.cumsum(lbs, axis=0) - lbs[0:1]
    row = lambda v: v.reshape(1, -1).astype(F32)
    n_dt = N_HEADS
    h = x.reshape(bsz * seq, d)
    delta = None
    for l in range(depth):
        wa = w_in[l, :, :PROJ_MAIN].astype(BF16)
        wb = w_in[l, :, PROJ_MAIN + n_dt:].astype(BF16)
        wc = jnp.pad(w_in[l, :, PROJ_MAIN:PROJ_MAIN + n_dt].astype(BF16), ((0, 0), (0, LANES - n_dt)))
        h, proj = _inproj(h, delta, row(ln1_w[l]), wa, wb, wc)
        proj = proj.reshape(bsz, seq, PROJ_PAD)
        rw = {"mu": row(rw_mu[l]), "w0": row(rw_w0[l]), "w2": rw_w2[l], "a0": row(rw_a0[l]),
              "a2": rw_a2[l], "g2": rw_g2[l], "k_k": row(rw_k_k[l]), "k_a": row(rw_k_a[l]),
              "r_k": row(rw_r_k[l]), "lnx_w": row(rw_lnx_w[l]), "lnx_b": row(rw_lnx_b[l])}
        y_rw = _rwkv(proj, rw, consts)
        y_s5 = _s5(proj, _s5_params(s5_lam_re[l], s5_lam_im[l], s5_log_dt[l], s5_b_re[l], s5_b_im[l],
                                    s5_c_re[l], s5_c_im[l], s5_d[l], s5_w_glu[l], s5_b_glu[l]))
        pad_h = lambda v: jnp.pad(v.astype(F32), (0, LANES - n_dt)).reshape(1, LANES)
        mp = {"conv_w": m_conv_w[l], "conv_b": row(m_conv_b[l]), "dt_bias": pad_h(m_dt_bias[l]),
              "a_neg": pad_h(-jnp.exp(m_a_log[l].astype(F32))),
              "d_exp": row(jnp.repeat(m_d[l], HEAD_DIM)), "norm_w": row(m_norm_w[l])}
        y_m = _mamba(proj, mp, consts)
        y_hg = _hgrn(proj, row(lbs[l]), row(hg_norm_w[l]), consts)
        ys = [y.reshape(bsz * seq, GW) for y in (y_rw, y_s5, y_m, y_hg)]
        n_route = N_EXPERT_GROUPS + N_EXPERTS
        wr = jnp.pad(jnp.concatenate([moe_w_rg[l], moe_w_re[l]], axis=1), ((0, 0), (0, ROUTE_LANES - n_route)))
        wr_hi = wr.astype(BF16)
        wr_lo = (wr - wr_hi.astype(F32)).astype(BF16)
        br = jnp.pad(jnp.concatenate([moe_b_rg[l], moe_b_re[l]]), (0, ROUTE_LANES - n_route)).reshape(1, -1)
        h, xn, comb = _outproj(h, ys, w_out[l].astype(BF16), row(ln2_w[l]),
                               jnp.concatenate([wr_hi, wr_lo], axis=1), br)
        delta = _moe(xn, comb, moe_w_gate[l].astype(BF16), moe_w_up[l].astype(BF16), moe_w_down[l].astype(BF16))
    return _final_norm(h, delta, row(lnf_w)).reshape(bsz, seq, d)
```

```python
import functools
import math

import jax
import jax.numpy as jnp
from jax import lax
from jax.experimental import pallas as pl
from jax.experimental.pallas import tpu as pltpu

F32 = jnp.float32
BF16 = jnp.bfloat16

NORM_EPS = 1e-6
GW = 256
HEAD_DIM = 64
N_HEADS = GW // HEAD_DIM
LANES = 128
SUBLANES = 8
RW_GN_EPS = 64e-5
HG_F_FLOOR = 1e-20
S5_STATE_W = 1024
M_CHUNK = 128
N_EXPERTS = 32
EXPERTS_PER_GROUP = 8
N_EXPERT_GROUPS = 4
ROUTE_LANES = LANES
ROUTE_OFF = N_EXPERT_GROUPS

VMEM_LIMIT = 56 * 1024 * 1024


def _mm(a, b):
    return jnp.dot(a.astype(BF16), b.astype(BF16), preferred_element_type=F32)


def _mm_nt(a, b):
    return lax.dot_general(a.astype(BF16), b.astype(BF16), (((1,), (1,)), ((), ())),
                           preferred_element_type=F32)


def _mm_tn(a, b):
    return lax.dot_general(a.astype(BF16), b.astype(BF16), (((0,), (0,)), ((), ())),
                           preferred_element_type=F32)


def _parts(x, n):
    out, rem = [], x
    for i in range(n):
        p = rem.astype(BF16)
        out.append(p)
        if i + 1 < n:
            rem = rem - p.astype(F32)
    return out


def _mm_x(a, b, na, nb):
    pa, pb = _parts(a, na), _parts(b, nb)
    acc = None
    for i in range(na):
        for j in range(nb):
            if i + j < max(na, nb):
                t = jnp.dot(pa[i], pb[j], preferred_element_type=F32)
                acc = t if acc is None else acc + t
    return acc


def _sigmoid(x):
    return 1.0 / (1.0 + jnp.exp(-x))


def _silu(x):
    return x * _sigmoid(x)


def _softplus(x):
    return jnp.maximum(x, 0.0) + jnp.log1p(jnp.exp(-jnp.abs(x)))


def _rms(x, w):
    ms = jnp.mean(x * x, axis=-1, keepdims=True)
    return x * lax.rsqrt(ms + NORM_EPS) * w


def _stack4(x, bd16):
    xb = x.astype(BF16)
    return jnp.concatenate([xb, xb, xb, xb], axis=0) * bd16


def _const(shape):
    return pl.BlockSpec(shape, lambda *_: (0,) * len(shape))


def _params(sem):
    return pltpu.CompilerParams(dimension_semantics=sem, vmem_limit_bytes=VMEM_LIMIT)


def _prenorm_body(*refs, with_delta):
    if with_delta:
        h_ref, dl_ref, lnw_ref, hn_ref, xn_ref = refs
        h = h_ref[...] + dl_ref[...]
        hn_ref[...] = h
    else:
        h_ref, lnw_ref, xn_ref = refs
        h = h_ref[...]
    xn_ref[...] = _rms(h, lnw_ref[...]).astype(BF16)


def _prenorm(h, delta, lnw, tm=1024):
    t, d = h.shape
    row = pl.BlockSpec((tm, d), lambda i: (i, 0))
    xn_shape = jax.ShapeDtypeStruct((t, d), BF16)
    with_delta = delta is not None
    out = pl.pallas_call(
        functools.partial(_prenorm_body, with_delta=with_delta),
        out_shape=(jax.ShapeDtypeStruct((t, d), F32), xn_shape) if with_delta else xn_shape,
        grid=(t // tm,),
        in_specs=([row, row] if with_delta else [row]) + [_const((1, d))],
        out_specs=(row, row) if with_delta else row,
        compiler_params=_params(("parallel",)),
        name="prenorm",
    )(*((h, delta) if with_delta else (h,)), lnw)
    return out if with_delta else (h, out)


RW_CHUNK = 64


def _rwkv_body(x_ref, win_ref, mu_ref, w0_ref, w2_ref, a0_ref, a2_ref, g2_ref, kk_ref, ka_ref, rk_ref,
               lnw_ref, lnb_ref, bd_ref, bd16_ref, hm_ref, tri_ref, lowi_ref, lows_ref, eye_ref,
               o_ref, carry_ref, st_ref, r_s, k_s, v_s, kn_s, kb_s, wl_s, y_s):
    nb, tt, d = x_ref.shape

    @pl.when(pl.program_id(1) == 0)
    def _():
        carry_ref[...] = jnp.zeros_like(carry_ref)
        st_ref[...] = jnp.zeros_like(st_ref)

    proj = jnp.dot(x_ref[...].reshape(nb * tt, d), win_ref[...], preferred_element_type=F32)
    rows = lax.broadcasted_iota(jnp.int32, (tt, 4 * GW), 0)
    mixed = []
    for b in range(nb):
        p = proj[b * tt:(b + 1) * tt]
        prev = jnp.where(rows == 0, carry_ref[b, 0:1, :], pltpu.roll(p, 1, axis=0))
        carry_ref[b, 0:1, :] = p[tt - 1:tt, :]
        mixed.append(p + (prev - p) * mu_ref[...])
    p = jnp.concatenate(mixed, axis=0)
    r = p[:, 0:GW]
    k = p[:, GW:2 * GW]
    v = p[:, 2 * GW:3 * GW]
    wl = p[:, 3 * GW:3 * GW + 64]
    al = p[:, 3 * GW + 64:3 * GW + 128]
    gl = p[:, 3 * GW + 128:]
    hm = hm_ref[...]
    w_log = -jnp.exp(-_softplus(-(w0_ref[...] + _mm_x(jnp.tanh(wl), w2_ref[...], 2, 2))) - 0.5)
    a = _sigmoid(a0_ref[...] + _mm(al, a2_ref[...]))
    gate = _mm(_sigmoid(gl), g2_ref[...])
    kn = k * kk_ref[...]
    nrm = jnp.sqrt(_mm_x(kn * kn, hm, 2, 1) * float(HEAD_DIM))
    kn = kn / jnp.maximum(nrm, 1e-12)
    k = k * (1.0 + (a - 1.0) * ka_ref[...])
    r_s[...] = r
    k_s[...] = k
    v_s[...] = v
    kn_s[...] = kn
    kb_s[...] = kn * a
    wl_s[...] = w_log

    def chunk(c, carry):
        seqs = range(nb)
        sls = [pl.ds(pl.multiple_of(b * tt + c * RW_CHUNK, RW_CHUNK), RW_CHUNK) for b in seqs]
        ld = lambda s: [s[sl, :] for sl in sls]
        r_c, k_c, v_c, kn_c, kb_c, wl_c = (ld(s) for s in (r_s, k_s, v_s, kn_s, kb_s, wl_s))
        bd = bd_ref[...]
        bd16 = bd16_ref[...]
        lows = lows_ref[...]
        lowi = lowi_ref[...]
        st4 = lambda xs: [_stack4(x, bd16) for x in xs]
        gc = [_mm_x(tri_ref[...], w, 1, 3) for w in wl_c]
        g_end = [g[RW_CHUNK - 1:RW_CHUNK, :] for g in gc]
        inv = [jnp.exp(-g) for g in gc]
        lhs = [jnp.concatenate([-kn * jnp.exp(g - w), r * jnp.exp(g)], axis=0)
               for kn, r, g, w in zip(kn_c, r_c, gc, wl_c)]
        rhs = [jnp.concatenate([_stack4(kb * i, bd16), _stack4(k * i, bd16)], axis=0)
               for kb, k, i in zip(kb_c, k_c, inv)]
        gram = [_mm_nt(a, b) for a, b in zip(lhs, rhs)]
        state = [st_ref[b] for b in seqs]
        from_state = [_mm_nt(a, s) for a, s in zip(lhs, state)]
        a_ab = [g[:RW_CHUNK, :GW] * lows for g in gram]
        pw = [eye_ref[...] + a for a in a_ab]
        q = a_ab
        qs = st4(q)
        for _ in range(5):
            q = [_mm(a, b) for a, b in zip(q, qs)]
            qs = st4(q)
            pw = [p + _mm(p, b) for p, b in zip(pw, qs)]
        vst = st4(v_c)
        rhs_u = [f[:RW_CHUNK] + _mm(g[:RW_CHUNK, GW:] * lows, vs) for f, g, vs in zip(from_state, gram, vst)]
        u = [_mm(p, x) for p, x in zip(pw, st4(rhs_u))]
        ust = st4(u)
        for b in seqs:
            y_s[sls[b], :] = (from_state[b][RW_CHUNK:] + _mm(gram[b][RW_CHUNK:, :GW] * lowi, ust[b])
                              + _mm(gram[b][RW_CHUNK:, GW:] * lowi, vst[b]))
        for b in seqs:
            to_end = jnp.exp(g_end[b] - gc[b])
            upd = _mm_tn(jnp.concatenate([u[b], v_c[b]], axis=0),
                         jnp.concatenate([kb_c[b] * to_end, k_c[b] * to_end], axis=0))
            st_ref[b] = state[b] * jnp.exp(g_end[b]) + upd * bd
        return carry

    lax.fori_loop(0, tt // RW_CHUNK, chunk, 0)

    y = y_s[...]
    mean = _mm_x(y, hm, 2, 1)
    d = y - mean
    var = _mm_x(d * d, hm, 2, 1)
    yn = d * lax.rsqrt(var + RW_GN_EPS) * lnw_ref[...] + lnb_ref[...]
    bonus = _mm_x(r * k * rk_ref[...], hm, 2, 1) * float(HEAD_DIM) * v
    o_ref[...] = ((yn + bonus) * gate).reshape(nb, tt, GW)


def _rwkv(xn3, w_in, prm, consts, nb=4, tt=128):
    b, l, d = xn3.shape
    vec = _const((1, GW))
    scr = pltpu.VMEM((nb * tt, GW), F32)
    return pl.pallas_call(
        _rwkv_body,
        out_shape=jax.ShapeDtypeStruct((b, l, GW), F32),
        grid=(b // nb, l // tt),
        in_specs=[pl.BlockSpec((nb, tt, d), lambda i, j: (i, j, 0)), _const((d, 4 * GW)),
                  _const((1, 4 * GW)), vec, _const((64, GW)), vec, _const((64, GW)),
                  _const((128, GW)), vec, vec, vec, vec, vec,
                  _const((GW, GW)), _const((GW, GW)), _const((GW, GW)), _const((RW_CHUNK, RW_CHUNK)),
                  _const((RW_CHUNK, GW)), _const((RW_CHUNK, GW)), _const((RW_CHUNK, GW))],
        out_specs=pl.BlockSpec((nb, tt, GW), lambda i, j: (i, j, 0)),
        scratch_shapes=[pltpu.VMEM((nb, SUBLANES, 4 * GW), F32), pltpu.VMEM((nb, GW, GW), F32)] + [scr] * 7,
        compiler_params=_params(("parallel", "arbitrary")),
        name="rwkv7",
    )(xn3, w_in, prm["mu"], prm["w0"], prm["w2"], prm["a0"], prm["a2"], prm["g2"], prm["k_k"],
      prm["k_a"], prm["r_k"], prm["lnx_w"], prm["lnx_b"],
      consts["bd"], consts["hones"], consts["hm"], consts["tri64"], consts["lowi"], consts["lows"], consts["eyew"])


def _s5_body(x_ref, win_ref, bbd_ref, ar_ref, ai_ref, cbd_ref, d_ref, wglu_ref, bglu_ref, o_ref,
             ub_s, ut_s, xr_s, xi_s, ot_s, st_ref):
    nb, tt, d = x_ref.shape
    w = win_ref.shape[1]
    ncb = w // LANES
    assert nb == SUBLANES

    @pl.when(pl.program_id(0) == 0)
    def _():
        st_ref[...] = jnp.zeros_like(st_ref)

    u_bt = jnp.dot(x_ref[...].reshape(nb * tt, d), win_ref[...], preferred_element_type=F32)
    for cb in range(ncb):
        ub_s[cb] = u_bt[:, cb * LANES:(cb + 1) * LANES]

    def regroup(t, carry):
        for cb in range(ncb):
            ut_s[cb, pl.ds(pl.multiple_of(t * nb, nb), nb), :] = ub_s[cb, pl.ds(t, nb, stride=tt), :]
        return carry

    lax.fori_loop(0, tt, regroup, 0, unroll=8)
    u = jnp.concatenate([ut_s[cb] for cb in range(ncb)], axis=-1)
    bu = _mm(u, bbd_ref[...])
    xr_s[...] = bu[:, :S5_STATE_W]
    xi_s[...] = bu[:, S5_STATE_W:]
    ar = jnp.broadcast_to(ar_ref[...], (nb, S5_STATE_W))
    ai = jnp.broadcast_to(ai_ref[...], (nb, S5_STATE_W))

    def step(t, carry):
        xr, xi = carry
        rows = pl.ds(pl.multiple_of(t * nb, nb), nb)
        nr = ar * xr - ai * xi + xr_s[rows, :]
        ni = ar * xi + ai * xr + xi_s[rows, :]
        xr_s[rows, :] = nr
        xi_s[rows, :] = ni
        return nr, ni

    xr, xi = lax.fori_loop(0, tt, step, (st_ref[0], st_ref[1]), unroll=4)
    st_ref[0] = xr
    st_ref[1] = xi
    cbd = cbd_ref[...]
    y = _mm(xr_s[...], cbd[:S5_STATE_W]) + _mm(xi_s[...], cbd[S5_STATE_W:]) + d_ref[...] * u
    y = 0.5 * y * (1.0 + jnp.tanh(math.sqrt(2.0 / math.pi) * (y + 0.044715 * (y * y * y))))
    z = _mm(y, wglu_ref[...]) + bglu_ref[...]
    out = y * _sigmoid(z)
    for cb in range(ncb):
        ot_s[cb] = out[:, cb * LANES:(cb + 1) * LANES]
    for b in range(nb):
        for cb in range(ncb):
            o_ref[b, :, cb * LANES:(cb + 1) * LANES] = ot_s[cb, pl.ds(b, tt, stride=nb), :]


def _s5(xn3, w_in, prm, tt=128):
    b, l, d = xn3.shape
    slab = pltpu.VMEM((GW // LANES, b * tt, LANES), F32)
    wide = pltpu.VMEM((b * tt, S5_STATE_W), F32)
    return pl.pallas_call(
        _s5_body,
        out_shape=jax.ShapeDtypeStruct((b, l, GW), F32),
        grid=(l // tt,),
        in_specs=[pl.BlockSpec((b, tt, d), lambda j: (0, j, 0)), _const((d, GW)),
                  _const((GW, 2 * S5_STATE_W)), _const((1, S5_STATE_W)), _const((1, S5_STATE_W)),
                  _const((2 * S5_STATE_W, GW)), _const((1, GW)), _const((GW, GW)), _const((1, GW))],
        out_specs=pl.BlockSpec((b, tt, GW), lambda j: (0, j, 0)),
        scratch_shapes=[slab, slab, wide, wide, slab, pltpu.VMEM((2, b, S5_STATE_W), F32)],
        compiler_params=_params(("arbitrary",)),
        name="s5",
    )(xn3, w_in, prm["bbd"], prm["ar"], prm["ai"], prm["cbd"], prm["d"], prm["w_glu"], prm["b_glu"])


def _mamba_body(x_ref, win_ref, cw_ref, cb_ref, dtb_ref, aneg_ref, dexp_ref, nw_ref,
                tri_ref, exp_ref, gsel_ref, o_ref, carry_ref, st_ref):
    nb, tt, d = x_ref.shape
    seqs = range(nb)

    @pl.when(pl.program_id(1) == 0)
    def _():
        carry_ref[...] = jnp.zeros_like(carry_ref)
        st_ref[...] = jnp.zeros_like(st_ref)

    proj = jnp.dot(x_ref[...].reshape(nb * tt, d), win_ref[...], preferred_element_type=F32)
    cw = 2 * GW
    rows = lax.broadcasted_iota(jnp.int32, (tt, cw), 0)
    pad = jnp.zeros((tt - SUBLANES, cw), F32)
    xcs = []
    for b in seqs:
        xbc = proj[b * tt:(b + 1) * tt, GW:GW + cw]
        tail = carry_ref[b]
        conv = xbc * cw_ref[3:4, :]
        for s in (1, 2, 3):
            head = jnp.concatenate([pltpu.roll(tail, s, axis=0), pad], axis=0)
            shifted = jnp.where(rows < s, head, pltpu.roll(xbc, s, axis=0))
            conv = conv + shifted * cw_ref[3 - s:4 - s, :]
        carry_ref[b] = xbc[tt - SUBLANES:, :]
        xcs.append(_silu(conv + cb_ref[...]))
    xs = [xc[:, :GW] for xc in xcs]
    bm = [xc[:, GW:GW + 128] for xc in xcs]
    cm = [xc[:, GW + 128:] for xc in xcs]
    expand = exp_ref[...]
    dt = [_softplus(proj[b * tt:(b + 1) * tt, GW + cw:] + dtb_ref[...]) for b in seqs]
    cs = [_mm_x(tri_ref[...], x * aneg_ref[...], 1, 3) for x in dt]
    cs_t = [c.T for c in cs]
    cs_end = [c[tt - 1:tt, :] for c in cs]
    wide = [_mm_x(jnp.concatenate([x, c, ce - c, jnp.broadcast_to(ce, (SUBLANES, 128))], axis=0), expand, 3, 1)
            for x, c, ce in zip(dt, cs, cs_end)]
    xdt = [x * w[:tt] for x, w in zip(xs, wide)]
    lane = lax.broadcasted_iota(jnp.int32, (tt, 128), 1)
    tril = lax.broadcasted_iota(jnp.int32, (tt, tt), 0) >= lax.broadcasted_iota(jnp.int32, (tt, tt), 1)
    lane_w = lax.broadcasted_iota(jnp.int32, (tt, GW), 1)
    state = [st_ref[b] for b in seqs]
    y = [_mm(c, s) * jnp.exp(w[tt:2 * tt]) + dexp_ref[...] * x for c, s, w, x in zip(cm, state, wide, xs)]
    for g in range(2):
        cbm = [_mm_nt(jnp.where(lane // 64 == g, c, 0.0), b_) for c, b_ in zip(cm, bm)]
        for h in (2 * g, 2 * g + 1):
            decay = [jnp.where(tril, jnp.exp(jnp.minimum(c[:, h:h + 1] - ct[h:h + 1, :], 0.0)), 0.0)
                     for c, ct in zip(cs, cs_t)]
            y = [yy + jnp.where(lane_w // HEAD_DIM == h, _mm(m * dc, xd), 0.0)
                 for yy, m, dc, xd in zip(y, cbm, decay, xdt)]
    upd = [_mm_tn(b_, xd * jnp.exp(w[2 * tt:3 * tt])) for b_, xd, w in zip(bm, xdt, wide)]
    nw = nw_ref[...]
    half = GW // 2
    for b in seqs:
        st_ref[b] = state[b] * jnp.exp(wide[b][3 * tt:3 * tt + 1]) + upd[b] * gsel_ref[...]
        yb = y[b] * _silu(proj[b * tt:(b + 1) * tt, :GW])
        o_ref[b] = jnp.concatenate([_rms(yb[:, :half], nw[:, :half]), _rms(yb[:, half:], nw[:, half:])], axis=-1)


def _mamba(xn3, w_in, prm, consts, nb=4):
    b, l, d = xn3.shape
    tt = M_CHUNK
    wcols = w_in.shape[1]
    return pl.pallas_call(
        _mamba_body,
        out_shape=jax.ShapeDtypeStruct((b, l, GW), F32),
        grid=(b // nb, l // tt),
        in_specs=[pl.BlockSpec((nb, tt, d), lambda i, j: (i, j, 0)), _const((d, wcols)),
                  _const((4, 2 * GW)), _const((1, 2 * GW)), _const((1, 128)), _const((1, 128)),
                  _const((1, GW)), _const((1, GW)),
                  _const((tt, tt)), _const((128, GW)), _const((128, GW))],
        out_specs=pl.BlockSpec((nb, tt, GW), lambda i, j: (i, j, 0)),
        scratch_shapes=[pltpu.VMEM((nb, SUBLANES, 2 * GW), F32), pltpu.VMEM((nb, 128, GW), F32)],
        compiler_params=_params(("parallel", "arbitrary")),
        name="mamba2",
    )(xn3, w_in, prm["conv_w"], prm["conv_b"], prm["dt_bias"], prm["a_neg"], prm["d_exp"],
      prm["norm_w"], consts["tri128"], consts["expand"], consts["gsel"])


HG_BLOCK = 16


def _hgrn_body(x_ref, win_ref, lb_ref, nw_ref, bd_ref, hm_ref, hones_ref, tri_ref, o_ref,
               st_ref, q_s, k_s, v_s, lf_s, o_s):
    nb, tt, d = x_ref.shape

    @pl.when(pl.program_id(1) == 0)
    def _():
        st_ref[...] = jnp.zeros_like(st_ref)

    p = jnp.dot(x_ref[...].reshape(nb * tt, d), win_ref[...], preferred_element_type=F32)
    lb = lb_ref[...]
    hf = p[:, GW:2 * GW]
    q_s[...] = _silu(p[:, :GW])
    k_s[...] = (1.0 - lb) * _sigmoid(-hf)
    v_s[...] = p[:, 2 * GW:3 * GW]
    lf_s[...] = jnp.log(jnp.maximum(lb + (1.0 - lb) * _sigmoid(hf), HG_F_FLOOR))

    def block(n, carry):
        seqs = range(nb)
        sls = [pl.ds(pl.multiple_of(b * tt + n * HG_BLOCK, HG_BLOCK), HG_BLOCK) for b in seqs]
        ld = lambda s: [s[sl, :] for sl in sls]
        q_b, k_b, v_b, lf_b = (ld(s) for s in (q_s, k_s, v_s, lf_s))
        g = [_mm_x(tri_ref[...], lf, 1, 3) for lf in lf_b]
        g_end = [x[HG_BLOCK - 1:HG_BLOCK, :] for x in g]
        state = [st_ref[b] for b in seqs]
        o = [_mm_nt(q * jnp.exp(x), s) for q, x, s in zip(q_b, g, state)]
        prods = [jnp.concatenate([q * jnp.exp(jnp.minimum(x - x[j:j + 1, :], 0.0)) * k[j:j + 1, :]
                                  for j in range(HG_BLOCK)], axis=0) for q, k, x in zip(q_b, k_b, g)]
        att = [_mm(p, hones_ref[...]) for p in prods]
        upd = [_mm_tn(v, k * jnp.exp(ge - x)) for v, k, ge, x in zip(v_b, k_b, g_end, g)]
        row = lax.broadcasted_iota(jnp.int32, (HG_BLOCK, GW), 0)
        for b in seqs:
            ob = o[b]
            for j in range(HG_BLOCK):
                ob = ob + jnp.where(row >= j, att[b][HG_BLOCK * j:HG_BLOCK * (j + 1), :], 0.0) * v_b[b][j:j + 1, :]
            o_s[sls[b], :] = ob
            st_ref[b] = state[b] * jnp.exp(g_end[b]) + upd[b] * bd_ref[...]
        return carry

    lax.fori_loop(0, tt // HG_BLOCK, block, 0)
    o = o_s[...]
    ms = _mm_x(o * o, hm_ref[...], 2, 1)
    o_ref[...] = (o * lax.rsqrt(ms + NORM_EPS) * nw_ref[...] * _silu(p[:, 3 * GW:])).reshape(nb, tt, GW)


def _hgrn(xn3, w_in, lb, nw, consts, nb=4, tt=128):
    b, l, d = xn3.shape
    return pl.pallas_call(
        _hgrn_body,
        out_shape=jax.ShapeDtypeStruct((b, l, GW), F32),
        grid=(b // nb, l // tt),
        in_specs=[pl.BlockSpec((nb, tt, d), lambda i, j: (i, j, 0)), _const((d, 4 * GW)),
                  _const((1, GW)), _const((1, GW)), _const((GW, GW)), _const((GW, GW)),
                  _const((GW, GW)), _const((HG_BLOCK, HG_BLOCK))],
        out_specs=pl.BlockSpec((nb, tt, GW), lambda i, j: (i, j, 0)),
        scratch_shapes=[pltpu.VMEM((nb, GW, GW), F32)] + [pltpu.VMEM((nb * tt, GW), F32)] * 5,
        compiler_params=_params(("parallel", "arbitrary")),
        name="hgrn2",
    )(xn3, w_in, lb, nw, consts["bd"], consts["hm"], consts["hones"], consts["tri16"])


def _outproj_body(h_ref, y1_ref, y2_ref, y3_ref, y4_ref, wo_ref, ln2_ref, wr_ref, br_ref,
                  hn_ref, xn_ref, comb_ref):
    y = jnp.concatenate([y1_ref[...], y2_ref[...], y3_ref[...], y4_ref[...]], axis=-1)
    h = h_ref[...] + _mm(y, wo_ref[...])
    hn_ref[...] = h
    xn = _rms(h, ln2_ref[...])
    xn_ref[...] = xn.astype(BF16)
    xh, xl = _parts(xn, 2)
    wr = wr_ref[...]
    first = jnp.dot(xh, wr, preferred_element_type=F32)
    logits = (first[:, :ROUTE_LANES] + first[:, ROUTE_LANES:]
              + jnp.dot(xl, wr[:, :ROUTE_LANES], preferred_element_type=F32) + br_ref[...])
    lane = lax.broadcasted_iota(jnp.int32, logits.shape, 1)
    neg = -jnp.inf
    big = ROUTE_LANES
    glog = jnp.where(lane < N_EXPERT_GROUPS, logits, neg)
    gmax = jnp.max(glog, axis=-1, keepdims=True)
    g_w = 1.0 / jnp.sum(jnp.exp(glog - gmax), axis=-1, keepdims=True)
    g_idx = jnp.min(jnp.where(glog == gmax, lane, big), axis=-1, keepdims=True)
    lo = ROUTE_OFF + EXPERTS_PER_GROUP * g_idx
    elog = jnp.where((lane >= lo) & (lane < lo + EXPERTS_PER_GROUP), logits, neg)
    m1 = jnp.max(elog, axis=-1, keepdims=True)
    i1 = jnp.min(jnp.where(elog == m1, lane, big), axis=-1, keepdims=True)
    elog2 = jnp.where(lane == i1, neg, elog)
    m2 = jnp.max(elog2, axis=-1, keepdims=True)
    i2 = jnp.min(jnp.where(elog2 == m2, lane, big), axis=-1, keepdims=True)
    e2 = jnp.exp(m2 - m1)
    w1 = 1.0 / (1.0 + e2)
    w2 = e2 / (1.0 + e2)
    comb_ref[...] = (g_w * (jnp.where(lane == i1 - lo, w1, 0.0) + jnp.where(lane == i2 - lo, w2, 0.0))
                     + jnp.where(lane == EXPERTS_PER_GROUP, g_idx.astype(F32), 0.0))


def _outproj(h, ys, wo, ln2, wr, br, tm=512):
    t, d = h.shape
    row = lambda w: pl.BlockSpec((tm, w), lambda i: (i, 0))
    return pl.pallas_call(
        _outproj_body,
        out_shape=(jax.ShapeDtypeStruct((t, d), F32), jax.ShapeDtypeStruct((t, d), BF16),
                   jax.ShapeDtypeStruct((t, ROUTE_LANES), F32)),
        grid=(t // tm,),
        in_specs=[row(d), row(GW), row(GW), row(GW), row(GW), _const((d, d)), _const((1, d)),
                  _const((d, 2 * ROUTE_LANES)), _const((1, ROUTE_LANES))],
        out_specs=(row(d), row(d), row(ROUTE_LANES)),
        compiler_params=_params(("parallel",)),
        name="outproj_router",
    )(h, *ys, wo, ln2, wr, br)


MOE_SUB = 256
MOE_ROWS = 80
MOE_HALF = EXPERTS_PER_GROUP // 2
COMB_GROUP_LANE = EXPERTS_PER_GROUP


def _moe_body(x_ref, comb_ref, wg_ref, wu_ref, wd_ref, o_ref, tri_s, key_s, cp_s, y_s, cnt_s):
    i = pl.program_id(0)
    g = pl.program_id(1)
    hf = pl.program_id(2)
    tm = x_ref.shape[0]
    nsub = tm // MOE_SUB
    subs = [slice(s * MOE_SUB, (s + 1) * MOE_SUB) for s in range(nsub)]

    @pl.when((i == 0) & (g == 0) & (hf == 0))
    def _():
        r = lax.broadcasted_iota(jnp.int32, (MOE_SUB, MOE_SUB), 0)
        c = lax.broadcasted_iota(jnp.int32, (MOE_SUB, MOE_SUB), 1)
        tri_s[...] = jnp.where(r > c, 1.0, 0.0).astype(BF16)

    @pl.when((g == 0) & (hf == 0))
    def _():
        comb = comb_ref[...]
        lane = lax.broadcasted_iota(jnp.int32, comb.shape, 1)
        gcol = comb[:, COMB_GROUP_LANE:COMB_GROUP_LANE + 1]
        onehot = jnp.where((lane < N_EXPERT_GROUPS) & (gcol == lane.astype(F32)), 1.0, 0.0)
        own = []
        for rows in subs:
            before = jnp.dot(tri_s[...], onehot[rows].astype(BF16), preferred_element_type=F32)
            own.append(jnp.sum(before * onehot[rows], axis=-1, keepdims=True))
        own = jnp.concatenate(own, axis=0)
        for grp in range(N_EXPERT_GROUPS):
            member = gcol == float(grp)
            key_s[grp] = jnp.broadcast_to(jnp.where(member, own, -1.0), (tm, ROUTE_LANES)).astype(BF16)
            count = jnp.sum(jnp.where(member[subs[0]], 1.0, 0.0))
            for rows in subs[1:]:
                count = jnp.maximum(count, jnp.sum(jnp.where(member[rows], 1.0, 0.0)))
            cnt_s[grp] = (count.astype(jnp.int32) + MOE_ROWS - 1) // MOE_ROWS
        cp_s[...] = jnp.concatenate(_parts(comb, 3), axis=-1)
        o_ref[...] = jnp.zeros_like(o_ref)

    slot = lax.broadcasted_iota(jnp.int32, (MOE_SUB, MOE_ROWS), 1).astype(F32)
    tn = lambda a, b: lax.dot_general(a, b, (((0,), (0,)), ((), ())), preferred_element_type=F32)
    second = hf == 1
    nl = ROUTE_LANES

    def block(blk, carry):
        base = (blk * MOE_ROWS).astype(F32)
        pts = [jnp.where(key_s[g, rows, :MOE_ROWS].astype(F32) - base == slot, 1.0, 0.0).astype(BF16)
               for rows in subs]
        xg = jnp.concatenate([tn(pt, x_ref[rows, :]) for pt, rows in zip(pts, subs)], axis=0).astype(BF16)
        cg3 = jnp.concatenate([tn(pt, cp_s[rows, :]) for pt, rows in zip(pts, subs)], axis=0)
        cg = cg3[:, :nl] + cg3[:, nl:2 * nl] + cg3[:, 2 * nl:]
        y = jnp.zeros((nsub * MOE_ROWS, x_ref.shape[1]), F32)
        for e in range(MOE_HALF):
            scale = jnp.where(second, cg[:, MOE_HALF + e:MOE_HALF + e + 1], cg[:, e:e + 1])
            act = (_silu(jnp.dot(xg, wg_ref[0, e], preferred_element_type=F32))
                   * jnp.dot(xg, wu_ref[0, e], preferred_element_type=F32) * scale)
            y = y + _mm(act, wd_ref[0, e])
        first_pass = blk == 0

        @pl.when(first_pass & jnp.logical_not(second))
        def _():
            y_s[...] = y

        @pl.when(second | jnp.logical_not(first_pass))
        def _():
            total = y + jnp.where(first_pass, y_s[...], 0.0)
            parts = _parts(total, 3)
            for s, (pt, rows) in enumerate(zip(pts, subs)):
                part = slice(s * MOE_ROWS, (s + 1) * MOE_ROWS)
                o_ref[rows, :] += jnp.dot(jnp.concatenate([pt, pt, pt], axis=1),
                                          jnp.concatenate([p[part] for p in parts], axis=0),
                                          preferred_element_type=F32)

        return carry

    lax.fori_loop(0, cnt_s[g], block, 0)


def _moe(xn, comb, wg, wu, wd, tm=2048):
    t, d = xn.shape
    ng, eg, _, de = wg.shape
    row = lambda w: pl.BlockSpec((tm, w), lambda i, g, hf: (i, 0))
    wspec = lambda a, b: pl.BlockSpec((1, MOE_HALF, a, b), lambda i, g, hf: (g, hf, 0, 0))
    return pl.pallas_call(
        _moe_body,
        out_shape=jax.ShapeDtypeStruct((t, d), F32),
        grid=(t // tm, ng, eg // MOE_HALF),
        in_specs=[row(d), row(ROUTE_LANES), wspec(d, de), wspec(d, de), wspec(de, d)],
        out_specs=row(d),
        scratch_shapes=[pltpu.VMEM((MOE_SUB, MOE_SUB), BF16), pltpu.VMEM((ng, tm, ROUTE_LANES), BF16),
                        pltpu.VMEM((tm, 3 * ROUTE_LANES), BF16),
                        pltpu.VMEM((tm // MOE_SUB * MOE_ROWS, d), F32),
                        pltpu.SMEM((ng,), jnp.int32)],
        compiler_params=_params(("arbitrary", "arbitrary", "arbitrary")),
        name="moe",
    )(xn, comb, wg, wu, wd)


def _norm_body(h_ref, dl_ref, w_ref, o_ref):
    o_ref[...] = _rms(h_ref[...] + dl_ref[...], w_ref[...])


def _final_norm(h, delta, w, tm=1024):
    t, d = h.shape
    row = pl.BlockSpec((tm, d), lambda i: (i, 0))
    return pl.pallas_call(
        _norm_body,
        out_shape=jax.ShapeDtypeStruct((t, d), F32),
        grid=(t // tm,),
        in_specs=[row, row, _const((1, d))],
        out_specs=row,
        compiler_params=_params(("parallel",)),
        name="final_norm",
    )(h, delta, w)


def _mask_consts():
    i256 = jnp.arange(GW)
    same_head = (i256[:, None] // HEAD_DIM) == (i256[None, :] // HEAD_DIM)
    t64 = jnp.arange(RW_CHUNK)
    s_w = i256 % RW_CHUNK
    h128 = jnp.arange(128)
    return {
        "bd": same_head.astype(F32),
        "hm": same_head.astype(F32) / HEAD_DIM,
        "hones": same_head.astype(BF16),
        "tri64": (t64[:, None] >= t64[None, :]).astype(F32),
        "lowi": (t64[:, None] >= s_w[None, :]).astype(F32),
        "lows": (t64[:, None] > s_w[None, :]).astype(F32),
        "eyew": (t64[:, None] == s_w[None, :]).astype(F32),
        "tri16": (jnp.arange(HG_BLOCK)[:, None] >= jnp.arange(HG_BLOCK)[None, :]).astype(F32),
        "tri128": (h128[:, None] >= h128[None, :]).astype(F32),
        "expand": (h128[:, None] == (i256[None, :] // HEAD_DIM)).astype(F32),
        "gsel": ((h128[:, None] // 64) == (i256[None, :] // 128)).astype(F32),
    }


def _s5_params(lam_re, lam_im, log_dt, b_re, b_im, c_re, c_im, d_skip, w_glu, b_glu):
    lr = jnp.minimum(lam_re, -1e-4)
    li = lam_im
    dt = jnp.exp(log_dt)[:, None]
    mag = jnp.exp(lr * dt)
    ar, ai = mag * jnp.cos(li * dt), mag * jnp.sin(li * dt)
    den = lr * lr + li * li
    nr = ar - 1.0
    er, ei = (nr * lr + ai * li) / den, (ai * lr - nr * li) / den
    bbr = er[..., None] * b_re - ei[..., None] * b_im
    bbi = er[..., None] * b_im + ei[..., None] * b_re
    eye = jnp.eye(lam_re.shape[0], dtype=F32)
    pack_b = lambda m: jnp.einsum("gph,gk->ghkp", m, eye).reshape(GW, S5_STATE_W)
    pack_c = lambda m: jnp.einsum("ghp,gk->gpkh", m, eye).reshape(S5_STATE_W, GW)
    return {
        "bbd": jnp.concatenate([pack_b(bbr), pack_b(bbi)], axis=1).astype(BF16),
        "cbd": jnp.concatenate([pack_c(c_re), -pack_c(c_im)], axis=0).astype(BF16),
        "ar": ar.reshape(1, S5_STATE_W), "ai": ai.reshape(1, S5_STATE_W),
        "d": d_skip.reshape(1, GW), "w_glu": w_glu.astype(BF16), "b_glu": b_glu.reshape(1, GW),
    }


def kernel(x, ln1_w, w_in, rw_mu, rw_w0, rw_w2, rw_a0, rw_a2, rw_g2, rw_k_k, rw_k_a, rw_r_k, rw_lnx_w, rw_lnx_b, s5_lam_re, s5_lam_im, s5_log_dt, s5_b_re, s5_b_im, s5_c_re, s5_c_im, s5_d, s5_w_glu, s5_b_glu, m_conv_w, m_conv_b, m_dt_bias, m_a_log, m_d, m_norm_w, hg_lb_logits, hg_norm_w, w_out, ln2_w, moe_w_rg, moe_b_rg, moe_w_re, moe_b_re, moe_w_gate, moe_w_up, moe_w_down, lnf_w):
    bsz, seq, d = x.shape
    depth = w_in.shape[0]
    consts = _mask_consts()
    lbs = jax.nn.softmax(hg_lb_logits.astype(F32), axis=0)
    lbs = jnp.cumsum(lbs, axis=0) - lbs[0:1]
    row = lambda v: v.reshape(1, -1).astype(F32)
    n_dt = N_HEADS
    h = x.reshape(bsz * seq, d)
    delta = None
    for l in range(depth):
        c_s5, c_m, c_hg = 4 * GW, 5 * GW, 8 * GW + n_dt
        w_rw = w_in[l, :, :c_s5].astype(BF16)
        w_s5 = w_in[l, :, c_s5:c_m].astype(BF16)
        w_m = jnp.pad(w_in[l, :, c_m:c_hg].astype(BF16), ((0, 0), (0, LANES - n_dt)))
        w_hg = w_in[l, :, c_hg:].astype(BF16)
        h, xn1 = _prenorm(h, delta, row(ln1_w[l]))
        xn1 = xn1.reshape(bsz, seq, d)
        rw = {"mu": row(rw_mu[l]), "w0": row(rw_w0[l]), "w2": rw_w2[l], "a0": row(rw_a0[l]),
              "a2": rw_a2[l], "g2": rw_g2[l], "k_k": row(rw_k_k[l]), "k_a": row(rw_k_a[l]),
              "r_k": row(rw_r_k[l]), "lnx_w": row(rw_lnx_w[l]), "lnx_b": row(rw_lnx_b[l])}
        y_rw = _rwkv(xn1, w_rw, rw, consts)
        y_s5 = _s5(xn1, w_s5, _s5_params(s5_lam_re[l], s5_lam_im[l], s5_log_dt[l], s5_b_re[l], s5_b_im[l],
                                    s5_c_re[l], s5_c_im[l], s5_d[l], s5_w_glu[l], s5_b_glu[l]))
        pad_h = lambda v: jnp.pad(v.astype(F32), (0, LANES - n_dt)).reshape(1, LANES)
        mp = {"conv_w": m_conv_w[l], "conv_b": row(m_conv_b[l]), "dt_bias": pad_h(m_dt_bias[l]),
              "a_neg": pad_h(-jnp.exp(m_a_log[l].astype(F32))),
              "d_exp": row(jnp.repeat(m_d[l], HEAD_DIM)), "norm_w": row(m_norm_w[l])}
        y_m = _mamba(xn1, w_m, mp, consts)
        y_hg = _hgrn(xn1, w_hg, row(lbs[l]), row(hg_norm_w[l]), consts)
        ys = [y.reshape(bsz * seq, GW) for y in (y_rw, y_s5, y_m, y_hg)]
        n_route = N_EXPERT_GROUPS + N_EXPERTS
        wr = jnp.pad(jnp.concatenate([moe_w_rg[l], moe_w_re[l]], axis=1), ((0, 0), (0, ROUTE_LANES - n_route)))
        wr_hi = wr.astype(BF16)
        wr_lo = (wr - wr_hi.astype(F32)).astype(BF16)
        br = jnp.pad(jnp.concatenate([moe_b_rg[l], moe_b_re[l]]), (0, ROUTE_LANES - n_route)).reshape(1, -1)
        h, xn, comb = _outproj(h, ys, w_out[l].astype(BF16), row(ln2_w[l]),
                               jnp.concatenate([wr_hi, wr_lo], axis=1), br)
        delta = _moe(xn, comb, moe_w_gate[l].astype(BF16), moe_w_up[l].astype(BF16), moe_w_down[l].astype(BF16))
    return _final_norm(h, delta, row(lnf_w)).reshape(bsz, seq, d)
```

```python
import functools
import math

import jax
import jax.numpy as jnp
from jax import lax
from jax.experimental import pallas as pl
from jax.experimental.pallas import tpu as pltpu

F32 = jnp.float32
BF16 = jnp.bfloat16

NORM_EPS = 1e-6
GW = 256
HEAD_DIM = 64
N_HEADS = GW // HEAD_DIM
LANES = 128
SUBLANES = 8
RW_GN_EPS = 64e-5
HG_F_FLOOR = 1e-20
S5_STATE_W = 1024
M_CHUNK = 128
N_EXPERTS = 32
EXPERTS_PER_GROUP = 8
N_EXPERT_GROUPS = 4
ROUTE_LANES = LANES
ROUTE_OFF = N_EXPERT_GROUPS

VMEM_LIMIT = 56 * 1024 * 1024


def _mm(a, b):
    return jnp.dot(a.astype(BF16), b.astype(BF16), preferred_element_type=F32)


def _mm_nt(a, b):
    return lax.dot_general(a.astype(BF16), b.astype(BF16), (((1,), (1,)), ((), ())),
                           preferred_element_type=F32)


def _mm_tn(a, b):
    return lax.dot_general(a.astype(BF16), b.astype(BF16), (((0,), (0,)), ((), ())),
                           preferred_element_type=F32)


def _parts(x, n):
    out, rem = [], x
    for i in range(n):
        p = rem.astype(BF16)
        out.append(p)
        if i + 1 < n:
            rem = rem - p.astype(F32)
    return out


def _mm_x(a, b, na, nb):
    pa, pb = _parts(a, na), _parts(b, nb)
    acc = None
    for i in range(na):
        for j in range(nb):
            if i + j < max(na, nb):
                t = jnp.dot(pa[i], pb[j], preferred_element_type=F32)
                acc = t if acc is None else acc + t
    return acc


def _sigmoid(x):
    return 1.0 / (1.0 + jnp.exp(-x))


def _silu(x):
    return x * _sigmoid(x)


def _softplus(x):
    return jnp.maximum(x, 0.0) + jnp.log1p(jnp.exp(-jnp.abs(x)))


def _rms(x, w):
    ms = jnp.mean(x * x, axis=-1, keepdims=True)
    return x * lax.rsqrt(ms + NORM_EPS) * w


def _stack4(x, bd16):
    xb = x.astype(BF16)
    return jnp.concatenate([xb, xb, xb, xb], axis=0) * bd16


def _const(shape):
    return pl.BlockSpec(shape, lambda *_: (0,) * len(shape))


def _params(sem):
    return pltpu.CompilerParams(dimension_semantics=sem, vmem_limit_bytes=VMEM_LIMIT)


def _prenorm_body(*refs, with_delta):
    if with_delta:
        h_ref, dl_ref, lnw_ref, hn_ref, xn_ref = refs
        h = h_ref[...] + dl_ref[...]
        hn_ref[...] = h
    else:
        h_ref, lnw_ref, xn_ref = refs
        h = h_ref[...]
    xn_ref[...] = _rms(h, lnw_ref[...]).astype(BF16)


def _prenorm(h, delta, lnw, tm=1024):
    t, d = h.shape
    row = pl.BlockSpec((tm, d), lambda i: (i, 0))
    xn_shape = jax.ShapeDtypeStruct((t, d), BF16)
    with_delta = delta is not None
    out = pl.pallas_call(
        functools.partial(_prenorm_body, with_delta=with_delta),
        out_shape=(jax.ShapeDtypeStruct((t, d), F32), xn_shape) if with_delta else xn_shape,
        grid=(t // tm,),
        in_specs=([row, row] if with_delta else [row]) + [_const((1, d))],
        out_specs=(row, row) if with_delta else row,
        compiler_params=_params(("parallel",)),
        name="prenorm",
    )(*((h, delta) if with_delta else (h,)), lnw)
    return out if with_delta else (h, out)


RW_CHUNK = 64


def _rwkv_body(x_ref, win_ref, mu_ref, w0_ref, w2_ref, a0_ref, a2_ref, g2_ref, kk_ref, ka_ref, rk_ref,
               lnw_ref, lnb_ref, bd_ref, bd16_ref, hm_ref, tri_ref, lowi_ref, lows_ref, eye_ref,
               o_ref, carry_ref, st_ref, r_s, k_s, v_s, kn_s, kb_s, wl_s, y_s):
    nb, tt, d = x_ref.shape

    @pl.when(pl.program_id(1) == 0)
    def _():
        carry_ref[...] = jnp.zeros_like(carry_ref)
        st_ref[...] = jnp.zeros_like(st_ref)

    proj = jnp.dot(x_ref[...].reshape(nb * tt, d), win_ref[...], preferred_element_type=F32)
    rows = lax.broadcasted_iota(jnp.int32, (tt, 4 * GW), 0)
    mixed = []
    for b in range(nb):
        p = proj[b * tt:(b + 1) * tt]
        prev = jnp.where(rows == 0, carry_ref[b, 0:1, :], pltpu.roll(p, 1, axis=0))
        carry_ref[b, 0:1, :] = p[tt - 1:tt, :]
        mixed.append(p + (prev - p) * mu_ref[...])
    p = jnp.concatenate(mixed, axis=0)
    r = p[:, 0:GW]
    k = p[:, GW:2 * GW]
    v = p[:, 2 * GW:3 * GW]
    wl = p[:, 3 * GW:3 * GW + 64]
    al = p[:, 3 * GW + 64:3 * GW + 128]
    gl = p[:, 3 * GW + 128:]
    hm = hm_ref[...]
    w_log = -jnp.exp(-_softplus(-(w0_ref[...] + _mm_x(jnp.tanh(wl), w2_ref[...], 2, 2))) - 0.5)
    a = _sigmoid(a0_ref[...] + _mm(al, a2_ref[...]))
    gate = _mm(_sigmoid(gl), g2_ref[...])
    kn = k * kk_ref[...]
    nrm = jnp.sqrt(_mm_x(kn * kn, hm, 2, 1) * float(HEAD_DIM))
    kn = kn / jnp.maximum(nrm, 1e-12)
    k = k * (1.0 + (a - 1.0) * ka_ref[...])
    r_s[...] = r
    k_s[...] = k
    v_s[...] = v
    kn_s[...] = kn
    kb_s[...] = kn * a
    wl_s[...] = w_log

    def chunk(c, carry):
        seqs = range(nb)
        sls = [pl.ds(pl.multiple_of(b * tt + c * RW_CHUNK, RW_CHUNK), RW_CHUNK) for b in seqs]
        ld = lambda s: [s[sl, :] for sl in sls]
        r_c, k_c, v_c, kn_c, kb_c, wl_c = (ld(s) for s in (r_s, k_s, v_s, kn_s, kb_s, wl_s))
        bd = bd_ref[...]
        bd16 = bd16_ref[...]
        lows = lows_ref[...]
        lowi = lowi_ref[...]
        st4 = lambda xs: [_stack4(x, bd16) for x in xs]
        gc = [_mm_x(tri_ref[...], w, 1, 3) for w in wl_c]
        g_end = [g[RW_CHUNK - 1:RW_CHUNK, :] for g in gc]
        inv = [jnp.exp(-g) for g in gc]
        lhs = [jnp.concatenate([-kn * jnp.exp(g - w), r * jnp.exp(g)], axis=0)
               for kn, r, g, w in zip(kn_c, r_c, gc, wl_c)]
        rhs = [jnp.concatenate([_stack4(kb * i, bd16), _stack4(k * i, bd16)], axis=0)
               for kb, k, i in zip(kb_c, k_c, inv)]
        gram = [_mm_nt(a, b) for a, b in zip(lhs, rhs)]
        state = [st_ref[b] for b in seqs]
        from_state = [_mm_nt(a, s) for a, s in zip(lhs, state)]
        a_ab = [g[:RW_CHUNK, :GW] * lows for g in gram]
        pw = [eye_ref[...] + a for a in a_ab]
        q = a_ab
        qs = st4(q)
        for _ in range(5):
            q = [_mm(a, b) for a, b in zip(q, qs)]
            qs = st4(q)
            pw = [p + _mm(p, b) for p, b in zip(pw, qs)]
        vst = st4(v_c)
        rhs_u = [f[:RW_CHUNK] + _mm(g[:RW_CHUNK, GW:] * lows, vs) for f, g, vs in zip(from_state, gram, vst)]
        u = [_mm(p, x) for p, x in zip(pw, st4(rhs_u))]
        ust = st4(u)
        for b in seqs:
            y_s[sls[b], :] = (from_state[b][RW_CHUNK:] + _mm(gram[b][RW_CHUNK:, :GW] * lowi, ust[b])
                              + _mm(gram[b][RW_CHUNK:, GW:] * lowi, vst[b]))
        for b in seqs:
            to_end = jnp.exp(g_end[b] - gc[b])
            upd = _mm_tn(jnp.concatenate([u[b], v_c[b]], axis=0),
                         jnp.concatenate([kb_c[b] * to_end, k_c[b] * to_end], axis=0))
            st_ref[b] = state[b] * jnp.exp(g_end[b]) + upd * bd
        return carry

    lax.fori_loop(0, tt // RW_CHUNK, chunk, 0)

    y = y_s[...]
    mean = _mm_x(y, hm, 2, 1)
    d = y - mean
    var = _mm_x(d * d, hm, 2, 1)
    yn = d * lax.rsqrt(var + RW_GN_EPS) * lnw_ref[...] + lnb_ref[...]
    bonus = _mm_x(r * k * rk_ref[...], hm, 2, 1) * float(HEAD_DIM) * v
    o_ref[...] = ((yn + bonus) * gate).astype(BF16).reshape(nb, tt, GW)


def _rwkv(xn3, w_in, prm, consts, nb=8, tt=128):
    b, l, d = xn3.shape
    vec = _const((1, GW))
    scr = pltpu.VMEM((nb * tt, GW), F32)
    return pl.pallas_call(
        _rwkv_body,
        out_shape=jax.ShapeDtypeStruct((b, l, GW), BF16),
        grid=(b // nb, l // tt),
        in_specs=[pl.BlockSpec((nb, tt, d), lambda i, j: (i, j, 0)), _const((d, 4 * GW)),
                  _const((1, 4 * GW)), vec, _const((64, GW)), vec, _const((64, GW)),
                  _const((128, GW)), vec, vec, vec, vec, vec,
                  _const((GW, GW)), _const((GW, GW)), _const((GW, GW)), _const((RW_CHUNK, RW_CHUNK)),
                  _const((RW_CHUNK, GW)), _const((RW_CHUNK, GW)), _const((RW_CHUNK, GW))],
        out_specs=pl.BlockSpec((nb, tt, GW), lambda i, j: (i, j, 0)),
        scratch_shapes=[pltpu.VMEM((nb, SUBLANES, 4 * GW), F32), pltpu.VMEM((nb, GW, GW), F32)] + [scr] * 7,
        compiler_params=_params(("parallel", "arbitrary")),
        name="rwkv7",
    )(xn3, w_in, prm["mu"], prm["w0"], prm["w2"], prm["a0"], prm["a2"], prm["g2"], prm["k_k"],
      prm["k_a"], prm["r_k"], prm["lnx_w"], prm["lnx_b"],
      consts["bd"], consts["hones"], consts["hm"], consts["tri64"], consts["lowi"], consts["lows"], consts["eyew"])


def _s5_body(x_ref, win_ref, bbd_ref, ar_ref, ai_ref, cbd_ref, d_ref, wglu_ref, bglu_ref, o_ref,
             ub_s, ut_s, xr_s, xi_s, ot_s, st_ref):
    nb, tt, d = x_ref.shape
    w = win_ref.shape[1]
    ncb = w // LANES
    assert nb == SUBLANES

    @pl.when(pl.program_id(0) == 0)
    def _():
        st_ref[...] = jnp.zeros_like(st_ref)

    u_bt = jnp.dot(x_ref[...].reshape(nb * tt, d), win_ref[...], preferred_element_type=F32)
    for cb in range(ncb):
        ub_s[cb] = u_bt[:, cb * LANES:(cb + 1) * LANES]

    def regroup(t, carry):
        for cb in range(ncb):
            ut_s[cb, pl.ds(pl.multiple_of(t * nb, nb), nb), :] = ub_s[cb, pl.ds(t, nb, stride=tt), :]
        return carry

    lax.fori_loop(0, tt, regroup, 0, unroll=8)
    u = jnp.concatenate([ut_s[cb] for cb in range(ncb)], axis=-1)
    bu = _mm(u, bbd_ref[...])
    xr_s[...] = bu[:, :S5_STATE_W]
    xi_s[...] = bu[:, S5_STATE_W:]
    ar = jnp.broadcast_to(ar_ref[...], (nb, S5_STATE_W))
    ai = jnp.broadcast_to(ai_ref[...], (nb, S5_STATE_W))

    def step(t, carry):
        xr, xi = carry
        rows = pl.ds(pl.multiple_of(t * nb, nb), nb)
        nr = ar * xr - ai * xi + xr_s[rows, :]
        ni = ar * xi + ai * xr + xi_s[rows, :]
        xr_s[rows, :] = nr
        xi_s[rows, :] = ni
        return nr, ni

    xr, xi = lax.fori_loop(0, tt, step, (st_ref[0], st_ref[1]), unroll=4)
    st_ref[0] = xr
    st_ref[1] = xi
    cbd = cbd_ref[...]
    y = _mm(xr_s[...], cbd[:S5_STATE_W]) + _mm(xi_s[...], cbd[S5_STATE_W:]) + d_ref[...] * u
    y = 0.5 * y * (1.0 + jnp.tanh(math.sqrt(2.0 / math.pi) * (y + 0.044715 * (y * y * y))))
    z = _mm(y, wglu_ref[...]) + bglu_ref[...]
    out = y * _sigmoid(z)
    for cb in range(ncb):
        ot_s[cb] = out[:, cb * LANES:(cb + 1) * LANES]
    for b in range(nb):
        for cb in range(ncb):
            o_ref[b, :, cb * LANES:(cb + 1) * LANES] = ot_s[cb, pl.ds(b, tt, stride=nb), :].astype(BF16)


def _s5(xn3, w_in, prm, tt=128):
    b, l, d = xn3.shape
    slab = pltpu.VMEM((GW // LANES, b * tt, LANES), F32)
    wide = pltpu.VMEM((b * tt, S5_STATE_W), F32)
    return pl.pallas_call(
        _s5_body,
        out_shape=jax.ShapeDtypeStruct((b, l, GW), BF16),
        grid=(l // tt,),
        in_specs=[pl.BlockSpec((b, tt, d), lambda j: (0, j, 0)), _const((d, GW)),
                  _const((GW, 2 * S5_STATE_W)), _const((1, S5_STATE_W)), _const((1, S5_STATE_W)),
                  _const((2 * S5_STATE_W, GW)), _const((1, GW)), _const((GW, GW)), _const((1, GW))],
        out_specs=pl.BlockSpec((b, tt, GW), lambda j: (0, j, 0)),
        scratch_shapes=[slab, slab, wide, wide, slab, pltpu.VMEM((2, b, S5_STATE_W), F32)],
        compiler_params=_params(("arbitrary",)),
        name="s5",
    )(xn3, w_in, prm["bbd"], prm["ar"], prm["ai"], prm["cbd"], prm["d"], prm["w_glu"], prm["b_glu"])


def _mamba_body(x_ref, win_ref, cw_ref, cb_ref, dtb_ref, aneg_ref, dexp_ref, nw_ref,
                tri_ref, exp_ref, gsel_ref, o_ref, carry_ref, st_ref):
    nb, tt, d = x_ref.shape
    seqs = range(nb)

    @pl.when(pl.program_id(1) == 0)
    def _():
        carry_ref[...] = jnp.zeros_like(carry_ref)
        st_ref[...] = jnp.zeros_like(st_ref)

    proj = jnp.dot(x_ref[...].reshape(nb * tt, d), win_ref[...], preferred_element_type=F32)
    cw = 2 * GW
    rows = lax.broadcasted_iota(jnp.int32, (tt, cw), 0)
    pad = jnp.zeros((tt - SUBLANES, cw), F32)
    xcs = []
    for b in seqs:
        xbc = proj[b * tt:(b + 1) * tt, GW:GW + cw]
        tail = carry_ref[b]
        conv = xbc * cw_ref[3:4, :]
        for s in (1, 2, 3):
            head = jnp.concatenate([pltpu.roll(tail, s, axis=0), pad], axis=0)
            shifted = jnp.where(rows < s, head, pltpu.roll(xbc, s, axis=0))
            conv = conv + shifted * cw_ref[3 - s:4 - s, :]
        carry_ref[b] = xbc[tt - SUBLANES:, :]
        xcs.append(_silu(conv + cb_ref[...]))
    xs = [xc[:, :GW] for xc in xcs]
    bm = [xc[:, GW:GW + 128] for xc in xcs]
    cm = [xc[:, GW + 128:] for xc in xcs]
    expand = exp_ref[...]
    dt = [_softplus(proj[b * tt:(b + 1) * tt, GW + cw:] + dtb_ref[...]) for b in seqs]
    cs = [_mm_x(tri_ref[...], x * aneg_ref[...], 1, 3) for x in dt]
    cs_t = [c.T for c in cs]
    cs_end = [c[tt - 1:tt, :] for c in cs]
    wide = [_mm_x(jnp.concatenate([x, c, ce - c, jnp.broadcast_to(ce, (SUBLANES, 128))], axis=0), expand, 3, 1)
            for x, c, ce in zip(dt, cs, cs_end)]
    xdt = [x * w[:tt] for x, w in zip(xs, wide)]
    lane = lax.broadcasted_iota(jnp.int32, (tt, 128), 1)
    tril = lax.broadcasted_iota(jnp.int32, (tt, tt), 0) >= lax.broadcasted_iota(jnp.int32, (tt, tt), 1)
    lane_w = lax.broadcasted_iota(jnp.int32, (tt, GW), 1)
    state = [st_ref[b] for b in seqs]
    y = [_mm(c, s) * jnp.exp(w[tt:2 * tt]) + dexp_ref[...] * x for c, s, w, x in zip(cm, state, wide, xs)]
    for g in range(2):
        cbm = [_mm_nt(jnp.where(lane // 64 == g, c, 0.0), b_) for c, b_ in zip(cm, bm)]
        for h in (2 * g, 2 * g + 1):
            decay = [jnp.where(tril, jnp.exp(jnp.minimum(c[:, h:h + 1] - ct[h:h + 1, :], 0.0)), 0.0)
                     for c, ct in zip(cs, cs_t)]
            y = [yy + jnp.where(lane_w // HEAD_DIM == h, _mm(m * dc, xd), 0.0)
                 for yy, m, dc, xd in zip(y, cbm, decay, xdt)]
    upd = [_mm_tn(b_, xd * jnp.exp(w[2 * tt:3 * tt])) for b_, xd, w in zip(bm, xdt, wide)]
    nw = nw_ref[...]
    half = GW // 2
    for b in seqs:
        st_ref[b] = state[b] * jnp.exp(wide[b][3 * tt:3 * tt + 1]) + upd[b] * gsel_ref[...]
        yb = y[b] * _silu(proj[b * tt:(b + 1) * tt, :GW])
        o_ref[b] = jnp.concatenate([_rms(yb[:, :half], nw[:, :half]), _rms(yb[:, half:], nw[:, half:])],
                                   axis=-1).astype(BF16)


def _mamba(xn3, w_in, prm, consts, nb=4):
    b, l, d = xn3.shape
    tt = M_CHUNK
    wcols = w_in.shape[1]
    return pl.pallas_call(
        _mamba_body,
        out_shape=jax.ShapeDtypeStruct((b, l, GW), BF16),
        grid=(b // nb, l // tt),
        in_specs=[pl.BlockSpec((nb, tt, d), lambda i, j: (i, j, 0)), _const((d, wcols)),
                  _const((4, 2 * GW)), _const((1, 2 * GW)), _const((1, 128)), _const((1, 128)),
                  _const((1, GW)), _const((1, GW)),
                  _const((tt, tt)), _const((128, GW)), _const((128, GW))],
        out_specs=pl.BlockSpec((nb, tt, GW), lambda i, j: (i, j, 0)),
        scratch_shapes=[pltpu.VMEM((nb, SUBLANES, 2 * GW), F32), pltpu.VMEM((nb, 128, GW), F32)],
        compiler_params=_params(("parallel", "arbitrary")),
        name="mamba2",
    )(xn3, w_in, prm["conv_w"], prm["conv_b"], prm["dt_bias"], prm["a_neg"], prm["d_exp"],
      prm["norm_w"], consts["tri128"], consts["expand"], consts["gsel"])


HG_BLOCK = 16


def _hgrn_body(x_ref, win_ref, lb_ref, nw_ref, bd_ref, hm_ref, hones_ref, tri_ref, o_ref,
               st_ref, q_s, k_s, v_s, lf_s, o_s):
    nb, tt, d = x_ref.shape

    @pl.when(pl.program_id(1) == 0)
    def _():
        st_ref[...] = jnp.zeros_like(st_ref)

    p = jnp.dot(x_ref[...].reshape(nb * tt, d), win_ref[...], preferred_element_type=F32)
    lb = lb_ref[...]
    hf = p[:, GW:2 * GW]
    q_s[...] = _silu(p[:, :GW])
    k_s[...] = (1.0 - lb) * _sigmoid(-hf)
    v_s[...] = p[:, 2 * GW:3 * GW]
    lf_s[...] = jnp.log(jnp.maximum(lb + (1.0 - lb) * _sigmoid(hf), HG_F_FLOOR))

    def block(n, carry):
        seqs = range(nb)
        sls = [pl.ds(pl.multiple_of(b * tt + n * HG_BLOCK, HG_BLOCK), HG_BLOCK) for b in seqs]
        ld = lambda s: [s[sl, :] for sl in sls]
        q_b, k_b, v_b, lf_b = (ld(s) for s in (q_s, k_s, v_s, lf_s))
        g = [_mm_x(tri_ref[...], lf, 1, 3) for lf in lf_b]
        g_end = [x[HG_BLOCK - 1:HG_BLOCK, :] for x in g]
        state = [st_ref[b] for b in seqs]
        o = [_mm_nt(q * jnp.exp(x), s) for q, x, s in zip(q_b, g, state)]
        prods = [jnp.concatenate([q * jnp.exp(jnp.minimum(x - x[j:j + 1, :], 0.0)) * k[j:j + 1, :]
                                  for j in range(HG_BLOCK)], axis=0) for q, k, x in zip(q_b, k_b, g)]
        att = [_mm(p, hones_ref[...]) for p in prods]
        upd = [_mm_tn(v, k * jnp.exp(ge - x)) for v, k, ge, x in zip(v_b, k_b, g_end, g)]
        row = lax.broadcasted_iota(jnp.int32, (HG_BLOCK, GW), 0)
        for b in seqs:
            ob = o[b]
            for j in range(HG_BLOCK):
                ob = ob + jnp.where(row >= j, att[b][HG_BLOCK * j:HG_BLOCK * (j + 1), :], 0.0) * v_b[b][j:j + 1, :]
            o_s[sls[b], :] = ob
            st_ref[b] = state[b] * jnp.exp(g_end[b]) + upd[b] * bd_ref[...]
        return carry

    lax.fori_loop(0, tt // HG_BLOCK, block, 0)
    o = o_s[...]
    ms = _mm_x(o * o, hm_ref[...], 2, 1)
    o_ref[...] = (o * lax.rsqrt(ms + NORM_EPS) * nw_ref[...] * _silu(p[:, 3 * GW:])).astype(BF16).reshape(nb, tt, GW)


def _hgrn(xn3, w_in, lb, nw, consts, nb=8, tt=128):
    b, l, d = xn3.shape
    return pl.pallas_call(
        _hgrn_body,
        out_shape=jax.ShapeDtypeStruct((b, l, GW), BF16),
        grid=(b // nb, l // tt),
        in_specs=[pl.BlockSpec((nb, tt, d), lambda i, j: (i, j, 0)), _const((d, 4 * GW)),
                  _const((1, GW)), _const((1, GW)), _const((GW, GW)), _const((GW, GW)),
                  _const((GW, GW)), _const((HG_BLOCK, HG_BLOCK))],
        out_specs=pl.BlockSpec((nb, tt, GW), lambda i, j: (i, j, 0)),
        scratch_shapes=[pltpu.VMEM((nb, GW, GW), F32)] + [pltpu.VMEM((nb * tt, GW), F32)] * 5,
        compiler_params=_params(("parallel", "arbitrary")),
        name="hgrn2",
    )(xn3, w_in, lb, nw, consts["bd"], consts["hm"], consts["hones"], consts["tri16"])


def _outproj_body(h_ref, y1_ref, y2_ref, y3_ref, y4_ref, wo_ref, ln2_ref, wr_ref, br_ref,
                  hn_ref, xn_ref, comb_ref):
    y = jnp.concatenate([y1_ref[...], y2_ref[...], y3_ref[...], y4_ref[...]], axis=-1)
    h = h_ref[...] + _mm(y, wo_ref[...])
    hn_ref[...] = h
    xn = _rms(h, ln2_ref[...])
    xn_ref[...] = xn.astype(BF16)
    xh, xl = _parts(xn, 2)
    wr = wr_ref[...]
    first = jnp.dot(xh, wr, preferred_element_type=F32)
    logits = (first[:, :ROUTE_LANES] + first[:, ROUTE_LANES:]
              + jnp.dot(xl, wr[:, :ROUTE_LANES], preferred_element_type=F32) + br_ref[...])
    lane = lax.broadcasted_iota(jnp.int32, logits.shape, 1)
    neg = -jnp.inf
    big = ROUTE_LANES
    glog = jnp.where(lane < N_EXPERT_GROUPS, logits, neg)
    gmax = jnp.max(glog, axis=-1, keepdims=True)
    g_w = 1.0 / jnp.sum(jnp.exp(glog - gmax), axis=-1, keepdims=True)
    g_idx = jnp.min(jnp.where(glog == gmax, lane, big), axis=-1, keepdims=True)
    lo = ROUTE_OFF + EXPERTS_PER_GROUP * g_idx
    elog = jnp.where((lane >= lo) & (lane < lo + EXPERTS_PER_GROUP), logits, neg)
    m1 = jnp.max(elog, axis=-1, keepdims=True)
    i1 = jnp.min(jnp.where(elog == m1, lane, big), axis=-1, keepdims=True)
    elog2 = jnp.where(lane == i1, neg, elog)
    m2 = jnp.max(elog2, axis=-1, keepdims=True)
    i2 = jnp.min(jnp.where(elog2 == m2, lane, big), axis=-1, keepdims=True)
    e2 = jnp.exp(m2 - m1)
    w1 = 1.0 / (1.0 + e2)
    w2 = e2 / (1.0 + e2)
    comb_ref[...] = (g_w * (jnp.where(lane == i1 - lo, w1, 0.0) + jnp.where(lane == i2 - lo, w2, 0.0))
                     + jnp.where(lane == EXPERTS_PER_GROUP, g_idx.astype(F32), 0.0))


def _outproj(h, ys, wo, ln2, wr, br, tm=512):
    t, d = h.shape
    row = lambda w: pl.BlockSpec((tm, w), lambda i: (i, 0))
    return pl.pallas_call(
        _outproj_body,
        out_shape=(jax.ShapeDtypeStruct((t, d), F32), jax.ShapeDtypeStruct((t, d), BF16),
                   jax.ShapeDtypeStruct((t, ROUTE_LANES), F32)),
        grid=(t // tm,),
        in_specs=[row(d), row(GW), row(GW), row(GW), row(GW), _const((d, d)), _const((1, d)),
                  _const((d, 2 * ROUTE_LANES)), _const((1, ROUTE_LANES))],
        out_specs=(row(d), row(d), row(ROUTE_LANES)),
        compiler_params=_params(("parallel",)),
        name="outproj_router",
    )(h, *ys, wo, ln2, wr, br)


MOE_SUB = 256
MOE_ROWS = 80
MOE_EXTRA = 16
MOE_HALF = EXPERTS_PER_GROUP // 2
COMB_GROUP_LANE = EXPERTS_PER_GROUP


def _moe_body(x_ref, comb_ref, wg_ref, wu_ref, wd_ref, o_ref, tri_s, key_s, cp_s, y_s, cnt_s):
    i = pl.program_id(0)
    g = pl.program_id(1)
    hf = pl.program_id(2)
    tm = x_ref.shape[0]
    nsub = tm // MOE_SUB
    subs = [slice(s * MOE_SUB, (s + 1) * MOE_SUB) for s in range(nsub)]

    @pl.when((i == 0) & (g == 0) & (hf == 0))
    def _():
        r = lax.broadcasted_iota(jnp.int32, (MOE_SUB, MOE_SUB), 0)
        c = lax.broadcasted_iota(jnp.int32, (MOE_SUB, MOE_SUB), 1)
        tri_s[...] = jnp.where(r > c, 1.0, 0.0).astype(BF16)

    @pl.when((g == 0) & (hf == 0))
    def _():
        comb = comb_ref[...]
        lane = lax.broadcasted_iota(jnp.int32, comb.shape, 1)
        gcol = comb[:, COMB_GROUP_LANE:COMB_GROUP_LANE + 1]
        onehot = jnp.where((lane < N_EXPERT_GROUPS) & (gcol == lane.astype(F32)), 1.0, 0.0)
        own = []
        for rows in subs:
            before = jnp.dot(tri_s[...], onehot[rows].astype(BF16), preferred_element_type=F32)
            own.append(jnp.sum(before * onehot[rows], axis=-1, keepdims=True))
        own = jnp.concatenate(own, axis=0)
        for grp in range(N_EXPERT_GROUPS):
            member = gcol == float(grp)
            key_s[grp] = jnp.broadcast_to(jnp.where(member, own, -1.0), (tm, ROUTE_LANES)).astype(BF16)
            count = jnp.sum(jnp.where(member[subs[0]], 1.0, 0.0))
            for rows in subs[1:]:
                count = jnp.maximum(count, jnp.sum(jnp.where(member[rows], 1.0, 0.0)))
            cnt_s[grp] = count.astype(jnp.int32)
        cp_s[...] = jnp.concatenate(_parts(comb, 3), axis=-1)
        o_ref[...] = jnp.zeros_like(o_ref)

    tn = lambda a, b: lax.dot_general(a, b, (((0,), (0,)), ((), ())), preferred_element_type=F32)
    second = hf == 1
    nl = ROUTE_LANES

    def run_pass(base, nrows, first_pass):
        slot = lax.broadcasted_iota(jnp.int32, (MOE_SUB, nrows), 1).astype(F32)
        pts = [jnp.where(key_s[g, rows, :nrows].astype(F32) - base == slot, 1.0, 0.0).astype(BF16)
               for rows in subs]
        xg = jnp.concatenate([tn(pt, x_ref[rows, :]) for pt, rows in zip(pts, subs)], axis=0).astype(BF16)
        cg3 = jnp.concatenate([tn(pt, cp_s[rows, :]) for pt, rows in zip(pts, subs)], axis=0)
        cg = cg3[:, :nl] + cg3[:, nl:2 * nl] + cg3[:, 2 * nl:]
        y = jnp.zeros((nsub * nrows, x_ref.shape[1]), F32)
        for e in range(MOE_HALF):
            scale = jnp.where(second, cg[:, MOE_HALF + e:MOE_HALF + e + 1], cg[:, e:e + 1])
            act = (_silu(jnp.dot(xg, wg_ref[0, e], preferred_element_type=F32))
                   * jnp.dot(xg, wu_ref[0, e], preferred_element_type=F32) * scale)
            y = y + _mm(act, wd_ref[0, e])

        def scatter(total):
            parts = _parts(total, 3)
            for s, (pt, rows) in enumerate(zip(pts, subs)):
                part = slice(s * nrows, (s + 1) * nrows)
                o_ref[rows, :] += jnp.dot(jnp.concatenate([pt, pt, pt], axis=1),
                                          jnp.concatenate([p[part] for p in parts], axis=0),
                                          preferred_element_type=F32)

        if first_pass:
            @pl.when(jnp.logical_not(second))
            def _():
                y_s[...] = y

            @pl.when(second)
            def _():
                scatter(y + y_s[...])
        else:
            scatter(y)

    count = cnt_s[g]

    @pl.when(count > 0)
    def _():
        run_pass(jnp.float32(0.0), MOE_ROWS, True)

    def extra(p, carry):
        run_pass((MOE_ROWS + p * MOE_EXTRA).astype(F32), MOE_EXTRA, False)
        return carry

    lax.fori_loop(0, (jnp.maximum(count - MOE_ROWS, 0) + MOE_EXTRA - 1) // MOE_EXTRA, extra, 0)


def _moe(xn, comb, wg, wu, wd, layer, tm=2048):
    t, d = xn.shape
    _, ng, eg, _, de = wg.shape
    row = lambda w: pl.BlockSpec((tm, w), lambda i, g, hf: (i, 0))
    wspec = lambda a, b: pl.BlockSpec((None, 1, MOE_HALF, a, b), lambda i, g, hf: (layer, g, hf, 0, 0))
    return pl.pallas_call(
        _moe_body,
        out_shape=jax.ShapeDtypeStruct((t, d), F32),
        grid=(t // tm, ng, eg // MOE_HALF),
        in_specs=[row(d), row(ROUTE_LANES), wspec(d, de), wspec(d, de), wspec(de, d)],
        out_specs=row(d),
        scratch_shapes=[pltpu.VMEM((MOE_SUB, MOE_SUB), BF16), pltpu.VMEM((ng, tm, ROUTE_LANES), BF16),
                        pltpu.VMEM((tm, 3 * ROUTE_LANES), BF16),
                        pltpu.VMEM((tm // MOE_SUB * MOE_ROWS, d), F32),
                        pltpu.SMEM((ng,), jnp.int32)],
        compiler_params=_params(("arbitrary", "arbitrary", "arbitrary")),
        name="moe",
    )(xn, comb, wg, wu, wd)


def _norm_body(h_ref, dl_ref, w_ref, o_ref):
    o_ref[...] = _rms(h_ref[...] + dl_ref[...], w_ref[...])


def _final_norm(h, delta, w, tm=1024):
    t, d = h.shape
    row = pl.BlockSpec((tm, d), lambda i: (i, 0))
    return pl.pallas_call(
        _norm_body,
        out_shape=jax.ShapeDtypeStruct((t, d), F32),
        grid=(t // tm,),
        in_specs=[row, row, _const((1, d))],
        out_specs=row,
        compiler_params=_params(("parallel",)),
        name="final_norm",
    )(h, delta, w)


def _mask_consts():
    i256 = jnp.arange(GW)
    same_head = (i256[:, None] // HEAD_DIM) == (i256[None, :] // HEAD_DIM)
    t64 = jnp.arange(RW_CHUNK)
    s_w = i256 % RW_CHUNK
    h128 = jnp.arange(128)
    return {
        "bd": same_head.astype(F32),
        "hm": same_head.astype(F32) / HEAD_DIM,
        "hones": same_head.astype(BF16),
        "tri64": (t64[:, None] >= t64[None, :]).astype(F32),
        "lowi": (t64[:, None] >= s_w[None, :]).astype(F32),
        "lows": (t64[:, None] > s_w[None, :]).astype(F32),
        "eyew": (t64[:, None] == s_w[None, :]).astype(F32),
        "tri16": (jnp.arange(HG_BLOCK)[:, None] >= jnp.arange(HG_BLOCK)[None, :]).astype(F32),
        "tri128": (h128[:, None] >= h128[None, :]).astype(F32),
        "expand": (h128[:, None] == (i256[None, :] // HEAD_DIM)).astype(F32),
        "gsel": ((h128[:, None] // 64) == (i256[None, :] // 128)).astype(F32),
    }


def _s5_params(lam_re, lam_im, log_dt, b_re, b_im, c_re, c_im, d_skip, w_glu, b_glu):
    lr = jnp.minimum(lam_re, -1e-4)
    li = lam_im
    dt = jnp.exp(log_dt)[:, None]
    mag = jnp.exp(lr * dt)
    ar, ai = mag * jnp.cos(li * dt), mag * jnp.sin(li * dt)
    den = lr * lr + li * li
    nr = ar - 1.0
    er, ei = (nr * lr + ai * li) / den, (ai * lr - nr * li) / den
    bbr = er[..., None] * b_re - ei[..., None] * b_im
    bbi = er[..., None] * b_im + ei[..., None] * b_re
    eye = jnp.eye(lam_re.shape[0], dtype=F32)
    pack_b = lambda m: jnp.einsum("gph,gk->ghkp", m, eye).reshape(GW, S5_STATE_W)
    pack_c = lambda m: jnp.einsum("ghp,gk->gpkh", m, eye).reshape(S5_STATE_W, GW)
    return {
        "bbd": jnp.concatenate([pack_b(bbr), pack_b(bbi)], axis=1).astype(BF16),
        "cbd": jnp.concatenate([pack_c(c_re), -pack_c(c_im)], axis=0).astype(BF16),
        "ar": ar.reshape(1, S5_STATE_W), "ai": ai.reshape(1, S5_STATE_W),
        "d": d_skip.reshape(1, GW), "w_glu": w_glu.astype(BF16), "b_glu": b_glu.reshape(1, GW),
    }


def kernel(x, ln1_w, w_in, rw_mu, rw_w0, rw_w2, rw_a0, rw_a2, rw_g2, rw_k_k, rw_k_a, rw_r_k, rw_lnx_w, rw_lnx_b, s5_lam_re, s5_lam_im, s5_log_dt, s5_b_re, s5_b_im, s5_c_re, s5_c_im, s5_d, s5_w_glu, s5_b_glu, m_conv_w, m_conv_b, m_dt_bias, m_a_log, m_d, m_norm_w, hg_lb_logits, hg_norm_w, w_out, ln2_w, moe_w_rg, moe_b_rg, moe_w_re, moe_b_re, moe_w_gate, moe_w_up, moe_w_down, lnf_w):
    bsz, seq, d = x.shape
    depth = w_in.shape[0]
    consts = _mask_consts()
    lbs = jax.nn.softmax(hg_lb_logits.astype(F32), axis=0)
    lbs = jnp.cumsum(lbs, axis=0) - lbs[0:1]
    row = lambda v: v.reshape(1, -1).astype(F32)
    n_dt = N_HEADS
    h = x.reshape(bsz * seq, d)
    delta = None
    wg16, wu16, wd16 = (w.astype(BF16) for w in (moe_w_gate, moe_w_up, moe_w_down))
    for l in range(depth):
        c_s5, c_m, c_hg = 4 * GW, 5 * GW, 8 * GW + n_dt
        w_rw = w_in[l, :, :c_s5].astype(BF16)
        w_s5 = w_in[l, :, c_s5:c_m].astype(BF16)
        w_m = jnp.pad(w_in[l, :, c_m:c_hg].astype(BF16), ((0, 0), (0, LANES - n_dt)))
        w_hg = w_in[l, :, c_hg:].astype(BF16)
        h, xn1 = _prenorm(h, delta, row(ln1_w[l]))
        xn1 = xn1.reshape(bsz, seq, d)
        rw = {"mu": row(rw_mu[l]), "w0": row(rw_w0[l]), "w2": rw_w2[l], "a0": row(rw_a0[l]),
              "a2": rw_a2[l], "g2": rw_g2[l], "k_k": row(rw_k_k[l]), "k_a": row(rw_k_a[l]),
              "r_k": row(rw_r_k[l]), "lnx_w": row(rw_lnx_w[l]), "lnx_b": row(rw_lnx_b[l])}
        y_rw = _rwkv(xn1, w_rw, rw, consts)
        y_s5 = _s5(xn1, w_s5, _s5_params(s5_lam_re[l], s5_lam_im[l], s5_log_dt[l], s5_b_re[l], s5_b_im[l],
                                    s5_c_re[l], s5_c_im[l], s5_d[l], s5_w_glu[l], s5_b_glu[l]))
        pad_h = lambda v: jnp.pad(v.astype(F32), (0, LANES - n_dt)).reshape(1, LANES)
        mp = {"conv_w": m_conv_w[l], "conv_b": row(m_conv_b[l]), "dt_bias": pad_h(m_dt_bias[l]),
              "a_neg": pad_h(-jnp.exp(m_a_log[l].astype(F32))),
              "d_exp": row(jnp.repeat(m_d[l], HEAD_DIM)), "norm_w": row(m_norm_w[l])}
        y_m = _mamba(xn1, w_m, mp, consts)
        y_hg = _hgrn(xn1, w_hg, row(lbs[l]), row(hg_norm_w[l]), consts)
        ys = [y.reshape(bsz * seq, GW) for y in (y_rw, y_s5, y_m, y_hg)]
        n_route = N_EXPERT_GROUPS + N_EXPERTS
        wr = jnp.pad(jnp.concatenate([moe_w_rg[l], moe_w_re[l]], axis=1), ((0, 0), (0, ROUTE_LANES - n_route)))
        wr_hi = wr.astype(BF16)
        wr_lo = (wr - wr_hi.astype(F32)).astype(BF16)
        br = jnp.pad(jnp.concatenate([moe_b_rg[l], moe_b_re[l]]), (0, ROUTE_LANES - n_route)).reshape(1, -1)
        h, xn, comb = _outproj(h, ys, w_out[l].astype(BF16), row(ln2_w[l]),
                               jnp.concatenate([wr_hi, wr_lo], axis=1), br)
        delta = _moe(xn, comb, wg16, wu16, wd16, l)
    return _final_norm(h, delta, row(lnf_w)).reshape(bsz, seq, d)
```

```python
import functools
import math

import jax
import jax.numpy as jnp
from jax import lax
from jax.experimental import pallas as pl
from jax.experimental.pallas import tpu as pltpu

F32 = jnp.float32
BF16 = jnp.bfloat16

NORM_EPS = 1e-6
GW = 256
HEAD_DIM = 64
N_HEADS = GW // HEAD_DIM
LANES = 128
SUBLANES = 8
RW_GN_EPS = 64e-5
HG_F_FLOOR = 1e-20
S5_STATE_W = 1024
M_CHUNK = 128
N_EXPERTS = 32
EXPERTS_PER_GROUP = 8
N_EXPERT_GROUPS = 4
ROUTE_LANES = LANES
ROUTE_OFF = N_EXPERT_GROUPS

VMEM_LIMIT = 56 * 1024 * 1024


def _mm(a, b):
    return jnp.dot(a.astype(BF16), b.astype(BF16), preferred_element_type=F32)


def _mm_nt(a, b):
    return lax.dot_general(a.astype(BF16), b.astype(BF16), (((1,), (1,)), ((), ())),
                           preferred_element_type=F32)


def _mm_tn(a, b):
    return lax.dot_general(a.astype(BF16), b.astype(BF16), (((0,), (0,)), ((), ())),
                           preferred_element_type=F32)


def _parts(x, n):
    out, rem = [], x
    for i in range(n):
        p = rem.astype(BF16)
        out.append(p)
        if i + 1 < n:
            rem = rem - p.astype(F32)
    return out


def _mm_x(a, b, na, nb):
    pa, pb = _parts(a, na), _parts(b, nb)
    acc = None
    for i in range(na):
        for j in range(nb):
            if i + j < max(na, nb):
                t = jnp.dot(pa[i], pb[j], preferred_element_type=F32)
                acc = t if acc is None else acc + t
    return acc


def _sigmoid(x):
    return 1.0 / (1.0 + jnp.exp(-x))


def _silu(x):
    return x * _sigmoid(x)


def _softplus(x):
    return jnp.maximum(x, 0.0) + jnp.log1p(jnp.exp(-jnp.abs(x)))


def _rms(x, w):
    ms = jnp.mean(x * x, axis=-1, keepdims=True)
    return x * lax.rsqrt(ms + NORM_EPS) * w


def _stack4(x, bd16):
    xb = x.astype(BF16)
    return jnp.concatenate([xb, xb, xb, xb], axis=0) * bd16


def _const(shape):
    return pl.BlockSpec(shape, lambda *_: (0,) * len(shape))


def _params(sem):
    return pltpu.CompilerParams(dimension_semantics=sem, vmem_limit_bytes=VMEM_LIMIT)


def _prenorm_body(*refs, with_delta):
    if with_delta:
        h_ref, dl_ref, lnw_ref, hn_ref, xn_ref = refs
        h = h_ref[...] + dl_ref[...]
        hn_ref[...] = h
    else:
        h_ref, lnw_ref, xn_ref = refs
        h = h_ref[...]
    xn_ref[...] = _rms(h, lnw_ref[...]).astype(BF16)


def _prenorm(h, delta, lnw, tm=1024):
    t, d = h.shape
    row = pl.BlockSpec((tm, d), lambda i: (i, 0))
    xn_shape = jax.ShapeDtypeStruct((t, d), BF16)
    with_delta = delta is not None
    out = pl.pallas_call(
        functools.partial(_prenorm_body, with_delta=with_delta),
        out_shape=(jax.ShapeDtypeStruct((t, d), F32), xn_shape) if with_delta else xn_shape,
        grid=(t // tm,),
        in_specs=([row, row] if with_delta else [row]) + [_const((1, d))],
        out_specs=(row, row) if with_delta else row,
        compiler_params=_params(("parallel",)),
        name="prenorm",
    )(*((h, delta) if with_delta else (h,)), lnw)
    return out if with_delta else (h, out)


RW_CHUNK = 64


def _rwkv_body(x_ref, win_ref, mu_ref, w0_ref, w2_ref, a0_ref, a2_ref, g2_ref, kk_ref, ka_ref, rk_ref,
               lnw_ref, lnb_ref, bd_ref, bd16_ref, hm_ref, tri_ref, lowi_ref, lows_ref, eye_ref,
               o_ref, carry_ref, st_ref, r_s, k_s, v_s, kn_s, kb_s, wl_s, y_s):
    nb, tt, d = x_ref.shape

    @pl.when(pl.program_id(1) == 0)
    def _():
        carry_ref[...] = jnp.zeros_like(carry_ref)
        st_ref[...] = jnp.zeros_like(st_ref)

    proj = jnp.dot(x_ref[...].reshape(nb * tt, d), win_ref[...], preferred_element_type=F32)
    rows = lax.broadcasted_iota(jnp.int32, (tt, 4 * GW), 0)
    mixed = []
    for b in range(nb):
        p = proj[b * tt:(b + 1) * tt]
        prev = jnp.where(rows == 0, carry_ref[b, 0:1, :], pltpu.roll(p, 1, axis=0))
        carry_ref[b, 0:1, :] = p[tt - 1:tt, :]
        mixed.append(p + (prev - p) * mu_ref[...])
    p = jnp.concatenate(mixed, axis=0)
    r = p[:, 0:GW]
    k = p[:, GW:2 * GW]
    v = p[:, 2 * GW:3 * GW]
    wl = p[:, 3 * GW:3 * GW + 64]
    al = p[:, 3 * GW + 64:3 * GW + 128]
    gl = p[:, 3 * GW + 128:]
    hm = hm_ref[...]
    w_log = -jnp.exp(-_softplus(-(w0_ref[...] + _mm_x(jnp.tanh(wl), w2_ref[...], 2, 2))) - 0.5)
    a = _sigmoid(a0_ref[...] + _mm(al, a2_ref[...]))
    gate = _mm(_sigmoid(gl), g2_ref[...])
    kn = k * kk_ref[...]
    nrm = jnp.sqrt(_mm_x(kn * kn, hm, 2, 1) * float(HEAD_DIM))
    kn = kn / jnp.maximum(nrm, 1e-12)
    k = k * (1.0 + (a - 1.0) * ka_ref[...])
    r_s[...] = r
    k_s[...] = k
    v_s[...] = v
    kn_s[...] = kn
    kb_s[...] = kn * a
    wl_s[...] = w_log

    def chunk(c, carry):
        seqs = range(nb)
        sls = [pl.ds(pl.multiple_of(b * tt + c * RW_CHUNK, RW_CHUNK), RW_CHUNK) for b in seqs]
        ld = lambda s: [s[sl, :] for sl in sls]
        r_c, k_c, v_c, kn_c, kb_c, wl_c = (ld(s) for s in (r_s, k_s, v_s, kn_s, kb_s, wl_s))
        bd = bd_ref[...]
        bd16 = bd16_ref[...]
        lows = lows_ref[...]
        lowi = lowi_ref[...]
        st4 = lambda xs: [_stack4(x, bd16) for x in xs]
        gc = [_mm_x(tri_ref[...], w, 1, 3) for w in wl_c]
        g_end = [g[RW_CHUNK - 1:RW_CHUNK, :] for g in gc]
        inv = [jnp.exp(-g) for g in gc]
        lhs = [jnp.concatenate([-kn * jnp.exp(g - w), r * jnp.exp(g)], axis=0)
               for kn, r, g, w in zip(kn_c, r_c, gc, wl_c)]
        rhs = [jnp.concatenate([_stack4(kb * i, bd16), _stack4(k * i, bd16)], axis=0)
               for kb, k, i in zip(kb_c, k_c, inv)]
        gram = [_mm_nt(a, b) for a, b in zip(lhs, rhs)]
        state = [st_ref[b] for b in seqs]
        from_state = [_mm_nt(a, s) for a, s in zip(lhs, state)]
        a_ab = [g[:RW_CHUNK, :GW] * lows for g in gram]
        pw = [eye_ref[...] + a for a in a_ab]
        q = a_ab
        qs = st4(q)
        for _ in range(5):
            q = [_mm(a, b) for a, b in zip(q, qs)]
            qs = st4(q)
            pw = [p + _mm(p, b) for p, b in zip(pw, qs)]
        vst = st4(v_c)
        rhs_u = [f[:RW_CHUNK] + _mm(g[:RW_CHUNK, GW:] * lows, vs) for f, g, vs in zip(from_state, gram, vst)]
        u = [_mm(p, x) for p, x in zip(pw, st4(rhs_u))]
        ust = st4(u)
        for b in seqs:
            y_s[sls[b], :] = (from_state[b][RW_CHUNK:] + _mm(gram[b][RW_CHUNK:, :GW] * lowi, ust[b])
                              + _mm(gram[b][RW_CHUNK:, GW:] * lowi, vst[b]))
        for b in seqs:
            to_end = jnp.exp(g_end[b] - gc[b])
            upd = _mm_tn(jnp.concatenate([u[b], v_c[b]], axis=0),
                         jnp.concatenate([kb_c[b] * to_end, k_c[b] * to_end], axis=0))
            st_ref[b] = state[b] * jnp.exp(g_end[b]) + upd * bd
        return carry

    lax.fori_loop(0, tt // RW_CHUNK, chunk, 0)

    y = y_s[...]
    mean = _mm_x(y, hm, 2, 1)
    d = y - mean
    var = _mm_x(d * d, hm, 2, 1)
    yn = d * lax.rsqrt(var + RW_GN_EPS) * lnw_ref[...] + lnb_ref[...]
    bonus = _mm_x(r * k * rk_ref[...], hm, 2, 1) * float(HEAD_DIM) * v
    o_ref[...] = ((yn + bonus) * gate).astype(BF16).reshape(nb, tt, GW)


def _rwkv(xn3, w_in, prm, consts, nb=8, tt=128):
    b, l, d = xn3.shape
    vec = _const((1, GW))
    scr = pltpu.VMEM((nb * tt, GW), F32)
    return pl.pallas_call(
        _rwkv_body,
        out_shape=jax.ShapeDtypeStruct((b, l, GW), BF16),
        grid=(b // nb, l // tt),
        in_specs=[pl.BlockSpec((nb, tt, d), lambda i, j: (i, j, 0)), _const((d, 4 * GW)),
                  _const((1, 4 * GW)), vec, _const((64, GW)), vec, _const((64, GW)),
                  _const((128, GW)), vec, vec, vec, vec, vec,
                  _const((GW, GW)), _const((GW, GW)), _const((GW, GW)), _const((RW_CHUNK, RW_CHUNK)),
                  _const((RW_CHUNK, GW)), _const((RW_CHUNK, GW)), _const((RW_CHUNK, GW))],
        out_specs=pl.BlockSpec((nb, tt, GW), lambda i, j: (i, j, 0)),
        scratch_shapes=[pltpu.VMEM((nb, SUBLANES, 4 * GW), F32), pltpu.VMEM((nb, GW, GW), F32)] + [scr] * 7,
        compiler_params=_params(("parallel", "arbitrary")),
        name="rwkv7",
    )(xn3, w_in, prm["mu"], prm["w0"], prm["w2"], prm["a0"], prm["a2"], prm["g2"], prm["k_k"],
      prm["k_a"], prm["r_k"], prm["lnx_w"], prm["lnx_b"],
      consts["bd"], consts["hones"], consts["hm"], consts["tri64"], consts["lowi"], consts["lows"], consts["eyew"])


def _s5_body(x_ref, win_ref, bbd_ref, ar_ref, ai_ref, cbd_ref, d_ref, wglu_ref, bglu_ref, o_ref,
             ub_s, ut_s, xr_s, xi_s, ot_s, st_ref):
    nb, tt, d = x_ref.shape
    w = win_ref.shape[1]
    ncb = w // LANES
    assert nb == SUBLANES

    @pl.when(pl.program_id(0) == 0)
    def _():
        st_ref[...] = jnp.zeros_like(st_ref)

    u_bt = jnp.dot(x_ref[...].reshape(nb * tt, d), win_ref[...], preferred_element_type=F32)
    for cb in range(ncb):
        ub_s[cb] = u_bt[:, cb * LANES:(cb + 1) * LANES]

    def regroup(t, carry):
        for cb in range(ncb):
            ut_s[cb, pl.ds(pl.multiple_of(t * nb, nb), nb), :] = ub_s[cb, pl.ds(t, nb, stride=tt), :]
        return carry

    lax.fori_loop(0, tt, regroup, 0, unroll=8)
    u = jnp.concatenate([ut_s[cb] for cb in range(ncb)], axis=-1)
    bu = _mm(u, bbd_ref[...])
    xr_s[...] = bu[:, :S5_STATE_W]
    xi_s[...] = bu[:, S5_STATE_W:]
    ar = jnp.broadcast_to(ar_ref[...], (nb, S5_STATE_W))
    ai = jnp.broadcast_to(ai_ref[...], (nb, S5_STATE_W))

    def step(t, carry):
        xr, xi = carry
        rows = pl.ds(pl.multiple_of(t * nb, nb), nb)
        nr = ar * xr - ai * xi + xr_s[rows, :]
        ni = ar * xi + ai * xr + xi_s[rows, :]
        xr_s[rows, :] = nr
        xi_s[rows, :] = ni
        return nr, ni

    xr, xi = lax.fori_loop(0, tt, step, (st_ref[0], st_ref[1]), unroll=4)
    st_ref[0] = xr
    st_ref[1] = xi
    cbd = cbd_ref[...]
    y = _mm(xr_s[...], cbd[:S5_STATE_W]) + _mm(xi_s[...], cbd[S5_STATE_W:]) + d_ref[...] * u
    y = 0.5 * y * (1.0 + jnp.tanh(math.sqrt(2.0 / math.pi) * (y + 0.044715 * (y * y * y))))
    z = _mm(y, wglu_ref[...]) + bglu_ref[...]
    out = y * _sigmoid(z)
    for cb in range(ncb):
        ot_s[cb] = out[:, cb * LANES:(cb + 1) * LANES]
    for b in range(nb):
        for cb in range(ncb):
            o_ref[b, :, cb * LANES:(cb + 1) * LANES] = ot_s[cb, pl.ds(b, tt, stride=nb), :].astype(BF16)


def _s5(xn3, w_in, prm, tt=128):
    b, l, d = xn3.shape
    slab = pltpu.VMEM((GW // LANES, b * tt, LANES), F32)
    wide = pltpu.VMEM((b * tt, S5_STATE_W), F32)
    return pl.pallas_call(
        _s5_body,
        out_shape=jax.ShapeDtypeStruct((b, l, GW), BF16),
        grid=(l // tt,),
        in_specs=[pl.BlockSpec((b, tt, d), lambda j: (0, j, 0)), _const((d, GW)),
                  _const((GW, 2 * S5_STATE_W)), _const((1, S5_STATE_W)), _const((1, S5_STATE_W)),
                  _const((2 * S5_STATE_W, GW)), _const((1, GW)), _const((GW, GW)), _const((1, GW))],
        out_specs=pl.BlockSpec((b, tt, GW), lambda j: (0, j, 0)),
        scratch_shapes=[slab, slab, wide, wide, slab, pltpu.VMEM((2, b, S5_STATE_W), F32)],
        compiler_params=_params(("arbitrary",)),
        name="s5",
    )(xn3, w_in, prm["bbd"], prm["ar"], prm["ai"], prm["cbd"], prm["d"], prm["w_glu"], prm["b_glu"])


def _mamba_body(x_ref, win_ref, cw_ref, cb_ref, dtb_ref, aneg_ref, dexp_ref, nw_ref,
                tri_ref, exp_ref, gsel_ref, o_ref, carry_ref, st_ref):
    nb, tt, d = x_ref.shape
    seqs = range(nb)

    @pl.when(pl.program_id(1) == 0)
    def _():
        carry_ref[...] = jnp.zeros_like(carry_ref)
        st_ref[...] = jnp.zeros_like(st_ref)

    proj = jnp.dot(x_ref[...].reshape(nb * tt, d), win_ref[...], preferred_element_type=F32)
    cw = 2 * GW
    rows = lax.broadcasted_iota(jnp.int32, (tt, cw), 0)
    pad = jnp.zeros((tt - SUBLANES, cw), F32)
    xcs = []
    for b in seqs:
        xbc = proj[b * tt:(b + 1) * tt, GW:GW + cw]
        tail = carry_ref[b]
        conv = xbc * cw_ref[3:4, :]
        for s in (1, 2, 3):
            head = jnp.concatenate([pltpu.roll(tail, s, axis=0), pad], axis=0)
            shifted = jnp.where(rows < s, head, pltpu.roll(xbc, s, axis=0))
            conv = conv + shifted * cw_ref[3 - s:4 - s, :]
        carry_ref[b] = xbc[tt - SUBLANES:, :]
        xcs.append(_silu(conv + cb_ref[...]))
    xs = [xc[:, :GW] for xc in xcs]
    bm = [xc[:, GW:GW + 128] for xc in xcs]
    cm = [xc[:, GW + 128:] for xc in xcs]
    expand = exp_ref[...]
    dt = [_softplus(proj[b * tt:(b + 1) * tt, GW + cw:] + dtb_ref[...]) for b in seqs]
    cs = [_mm_x(tri_ref[...], x * aneg_ref[...], 1, 3) for x in dt]
    cs_t = [c.T for c in cs]
    cs_end = [c[tt - 1:tt, :] for c in cs]
    wide = [_mm_x(jnp.concatenate([x, c, ce - c, jnp.broadcast_to(ce, (SUBLANES, 128))], axis=0), expand, 3, 1)
            for x, c, ce in zip(dt, cs, cs_end)]
    xdt = [x * w[:tt] for x, w in zip(xs, wide)]
    lane = lax.broadcasted_iota(jnp.int32, (tt, 128), 1)
    tril = lax.broadcasted_iota(jnp.int32, (tt, tt), 0) >= lax.broadcasted_iota(jnp.int32, (tt, tt), 1)
    lane_w = lax.broadcasted_iota(jnp.int32, (tt, GW), 1)
    state = [st_ref[b] for b in seqs]
    y = [_mm(c, s) * jnp.exp(w[tt:2 * tt]) + dexp_ref[...] * x for c, s, w, x in zip(cm, state, wide, xs)]
    for g in range(2):
        cbm = [_mm_nt(jnp.where(lane // 64 == g, c, 0.0), b_) for c, b_ in zip(cm, bm)]
        for h in (2 * g, 2 * g + 1):
            decay = [jnp.where(tril, jnp.exp(jnp.minimum(c[:, h:h + 1] - ct[h:h + 1, :], 0.0)), 0.0)
                     for c, ct in zip(cs, cs_t)]
            y = [yy + jnp.where(lane_w // HEAD_DIM == h, _mm(m * dc, xd), 0.0)
                 for yy, m, dc, xd in zip(y, cbm, decay, xdt)]
    upd = [_mm_tn(b_, xd * jnp.exp(w[2 * tt:3 * tt])) for b_, xd, w in zip(bm, xdt, wide)]
    nw = nw_ref[...]
    half = GW // 2
    for b in seqs:
        st_ref[b] = state[b] * jnp.exp(wide[b][3 * tt:3 * tt + 1]) + upd[b] * gsel_ref[...]
        yb = y[b] * _silu(proj[b * tt:(b + 1) * tt, :GW])
        o_ref[b] = jnp.concatenate([_rms(yb[:, :half], nw[:, :half]), _rms(yb[:, half:], nw[:, half:])],
                                   axis=-1).astype(BF16)


def _mamba(xn3, w_in, prm, consts, nb=4):
    b, l, d = xn3.shape
    tt = M_CHUNK
    wcols = w_in.shape[1]
    return pl.pallas_call(
        _mamba_body,
        out_shape=jax.ShapeDtypeStruct((b, l, GW), BF16),
        grid=(b // nb, l // tt),
        in_specs=[pl.BlockSpec((nb, tt, d), lambda i, j: (i, j, 0)), _const((d, wcols)),
                  _const((4, 2 * GW)), _const((1, 2 * GW)), _const((1, 128)), _const((1, 128)),
                  _const((1, GW)), _const((1, GW)),
                  _const((tt, tt)), _const((128, GW)), _const((128, GW))],
        out_specs=pl.BlockSpec((nb, tt, GW), lambda i, j: (i, j, 0)),
        scratch_shapes=[pltpu.VMEM((nb, SUBLANES, 2 * GW), F32), pltpu.VMEM((nb, 128, GW), F32)],
        compiler_params=_params(("parallel", "arbitrary")),
        name="mamba2",
    )(xn3, w_in, prm["conv_w"], prm["conv_b"], prm["dt_bias"], prm["a_neg"], prm["d_exp"],
      prm["norm_w"], consts["tri128"], consts["expand"], consts["gsel"])


HG_BLOCK = 16


def _hgrn_body(x_ref, win_ref, lb_ref, nw_ref, bd_ref, hm_ref, hones_ref, tri_ref, o_ref,
               st_ref, q_s, k_s, v_s, lf_s, o_s):
    nb, tt, d = x_ref.shape

    @pl.when(pl.program_id(1) == 0)
    def _():
        st_ref[...] = jnp.zeros_like(st_ref)

    p = jnp.dot(x_ref[...].reshape(nb * tt, d), win_ref[...], preferred_element_type=F32)
    lb = lb_ref[...]
    hf = p[:, GW:2 * GW]
    q_s[...] = _silu(p[:, :GW])
    k_s[...] = (1.0 - lb) * _sigmoid(-hf)
    v_s[...] = p[:, 2 * GW:3 * GW]
    lf_s[...] = jnp.log(jnp.maximum(lb + (1.0 - lb) * _sigmoid(hf), HG_F_FLOOR))

    def block(n, carry):
        seqs = range(nb)
        sls = [pl.ds(pl.multiple_of(b * tt + n * HG_BLOCK, HG_BLOCK), HG_BLOCK) for b in seqs]
        ld = lambda s: [s[sl, :] for sl in sls]
        q_b, k_b, v_b, lf_b = (ld(s) for s in (q_s, k_s, v_s, lf_s))
        g = [_mm_x(tri_ref[...], lf, 1, 3) for lf in lf_b]
        g_end = [x[HG_BLOCK - 1:HG_BLOCK, :] for x in g]
        state = [st_ref[b] for b in seqs]
        o = [_mm_nt(q * jnp.exp(x), s) for q, x, s in zip(q_b, g, state)]
        prods = [jnp.concatenate([q * jnp.exp(jnp.minimum(x - x[j:j + 1, :], 0.0)) * k[j:j + 1, :]
                                  for j in range(HG_BLOCK)], axis=0) for q, k, x in zip(q_b, k_b, g)]
        att = [_mm(p, hones_ref[...]) for p in prods]
        upd = [_mm_tn(v, k * jnp.exp(ge - x)) for v, k, ge, x in zip(v_b, k_b, g_end, g)]
        row = lax.broadcasted_iota(jnp.int32, (HG_BLOCK, GW), 0)
        for b in seqs:
            ob = o[b]
            for j in range(HG_BLOCK):
                ob = ob + jnp.where(row >= j, att[b][HG_BLOCK * j:HG_BLOCK * (j + 1), :], 0.0) * v_b[b][j:j + 1, :]
            o_s[sls[b], :] = ob
            st_ref[b] = state[b] * jnp.exp(g_end[b]) + upd[b] * bd_ref[...]
        return carry

    lax.fori_loop(0, tt // HG_BLOCK, block, 0)
    o = o_s[...]
    ms = _mm_x(o * o, hm_ref[...], 2, 1)
    o_ref[...] = (o * lax.rsqrt(ms + NORM_EPS) * nw_ref[...] * _silu(p[:, 3 * GW:])).astype(BF16).reshape(nb, tt, GW)


def _hgrn(xn3, w_in, lb, nw, consts, nb=8, tt=128):
    b, l, d = xn3.shape
    return pl.pallas_call(
        _hgrn_body,
        out_shape=jax.ShapeDtypeStruct((b, l, GW), BF16),
        grid=(b // nb, l // tt),
        in_specs=[pl.BlockSpec((nb, tt, d), lambda i, j: (i, j, 0)), _const((d, 4 * GW)),
                  _const((1, GW)), _const((1, GW)), _const((GW, GW)), _const((GW, GW)),
                  _const((GW, GW)), _const((HG_BLOCK, HG_BLOCK))],
        out_specs=pl.BlockSpec((nb, tt, GW), lambda i, j: (i, j, 0)),
        scratch_shapes=[pltpu.VMEM((nb, GW, GW), F32)] + [pltpu.VMEM((nb * tt, GW), F32)] * 5,
        compiler_params=_params(("parallel", "arbitrary")),
        name="hgrn2",
    )(xn3, w_in, lb, nw, consts["bd"], consts["hm"], consts["hones"], consts["tri16"])


def _outproj_body(h_ref, y1_ref, y2_ref, y3_ref, y4_ref, wo_ref, ln2_ref, wr_ref, br_ref,
                  hn_ref, xn_ref, comb_ref):
    y = jnp.concatenate([y1_ref[...], y2_ref[...], y3_ref[...], y4_ref[...]], axis=-1)
    h = h_ref[...] + _mm(y, wo_ref[...])
    hn_ref[...] = h
    xn = _rms(h, ln2_ref[...])
    xn_ref[...] = xn.astype(BF16)
    xh, xl = _parts(xn, 2)
    wr = wr_ref[...]
    first = jnp.dot(xh, wr, preferred_element_type=F32)
    logits = (first[:, :ROUTE_LANES] + first[:, ROUTE_LANES:]
              + jnp.dot(xl, wr[:, :ROUTE_LANES], preferred_element_type=F32) + br_ref[...])
    lane = lax.broadcasted_iota(jnp.int32, logits.shape, 1)
    neg = -jnp.inf
    big = ROUTE_LANES
    glog = jnp.where(lane < N_EXPERT_GROUPS, logits, neg)
    gmax = jnp.max(glog, axis=-1, keepdims=True)
    g_w = 1.0 / jnp.sum(jnp.exp(glog - gmax), axis=-1, keepdims=True)
    g_idx = jnp.min(jnp.where(glog == gmax, lane, big), axis=-1, keepdims=True)
    lo = ROUTE_OFF + EXPERTS_PER_GROUP * g_idx
    elog = jnp.where((lane >= lo) & (lane < lo + EXPERTS_PER_GROUP), logits, neg)
    m1 = jnp.max(elog, axis=-1, keepdims=True)
    i1 = jnp.min(jnp.where(elog == m1, lane, big), axis=-1, keepdims=True)
    elog2 = jnp.where(lane == i1, neg, elog)
    m2 = jnp.max(elog2, axis=-1, keepdims=True)
    i2 = jnp.min(jnp.where(elog2 == m2, lane, big), axis=-1, keepdims=True)
    e2 = jnp.exp(m2 - m1)
    w1 = 1.0 / (1.0 + e2)
    w2 = e2 / (1.0 + e2)
    comb_ref[...] = (g_w * (jnp.where(lane == i1 - lo, w1, 0.0) + jnp.where(lane == i2 - lo, w2, 0.0))
                     + jnp.where(lane == EXPERTS_PER_GROUP, g_idx.astype(F32), 0.0))


def _outproj(h, ys, wo, ln2, wr, br, tm=512):
    t, d = h.shape
    row = lambda w: pl.BlockSpec((tm, w), lambda i: (i, 0))
    return pl.pallas_call(
        _outproj_body,
        out_shape=(jax.ShapeDtypeStruct((t, d), F32), jax.ShapeDtypeStruct((t, d), BF16),
                   jax.ShapeDtypeStruct((t, ROUTE_LANES), F32)),
        grid=(t // tm,),
        in_specs=[row(d), row(GW), row(GW), row(GW), row(GW), _const((d, d)), _const((1, d)),
                  _const((d, 2 * ROUTE_LANES)), _const((1, ROUTE_LANES))],
        out_specs=(row(d), row(d), row(ROUTE_LANES)),
        compiler_params=_params(("parallel",)),
        name="outproj_router",
    )(h, *ys, wo, ln2, wr, br)


MOE_SUB = 512
MOE_ROWS = 64
MOE_EXTRA = 16
MOE_HALF = EXPERTS_PER_GROUP // 2
COMB_GROUP_LANE = EXPERTS_PER_GROUP


def _moe_body(x_ref, comb_ref, wg_ref, wu_ref, wd_ref, o_ref, tri_s, kt_s, wt_s, cnt_s):
    i = pl.program_id(0)
    g = pl.program_id(1)
    hf = pl.program_id(2)
    tm = x_ref.shape[0]
    nsub = tm // MOE_SUB
    subs = [slice(s * MOE_SUB, (s + 1) * MOE_SUB) for s in range(nsub)]
    steps = N_EXPERTS // MOE_HALF

    @pl.when((i == 0) & (g == 0) & (hf == 0))
    def _():
        r = lax.broadcasted_iota(jnp.int32, (MOE_SUB, MOE_SUB), 0)
        c = lax.broadcasted_iota(jnp.int32, (MOE_SUB, MOE_SUB), 1)
        tri_s[...] = jnp.where(r < c, 1.0, 0.0).astype(BF16)

    @pl.when((g == 0) & (hf == 0))
    def _():
        comb = comb_ref[...]
        lane = lax.broadcasted_iota(jnp.int32, comb.shape, 1)
        gcol = comb[:, COMB_GROUP_LANE:COMB_GROUP_LANE + 1]
        local = jnp.where(lane < EXPERTS_PER_GROUP, comb, 0.0)
        w_tok = jnp.where(gcol == 0.0, local, 0.0)
        for grp in range(1, N_EXPERT_GROUPS):
            w_tok = w_tok + jnp.where(gcol == float(grp), pltpu.roll(local, EXPERTS_PER_GROUP * grp, axis=1), 0.0)
        wt = w_tok.T
        used = wt != 0.0
        wt_s[...] = wt
        most = jnp.zeros((ROUTE_LANES, 1), F32)
        for rows in subs:
            u = jnp.where(used[:, rows], 1.0, 0.0)
            before = jnp.dot(u.astype(BF16), tri_s[...], preferred_element_type=F32)
            kt_s[:, rows] = jnp.where(used[:, rows], before, -1.0)
            most = jnp.maximum(most, jnp.sum(u, axis=-1, keepdims=True))
        for q in range(steps):
            cnt_s[q] = jnp.max(most[MOE_HALF * q:MOE_HALF * (q + 1), :]).astype(jnp.int32)
        o_ref[...] = jnp.zeros_like(o_ref)

    second = hf == 1
    group_rows = pl.ds(pl.multiple_of(EXPERTS_PER_GROUP * g, EXPERTS_PER_GROUP), EXPERTS_PER_GROUP)
    kt8 = kt_s[group_rows, :]
    wt8 = wt_s[group_rows, :]
    pick = lambda a, j: jnp.where(second, a[MOE_HALF + j:MOE_HALF + j + 1, :], a[j:j + 1, :])
    keys = [pick(kt8, j) for j in range(MOE_HALF)]
    wrow = [pick(wt8, j) for j in range(MOE_HALF)]
    tn = lambda a, b: lax.dot_general(a, b, (((0,), (0,)), ((), ())), preferred_element_type=F32)

    def run_pass(base, nrows):
        ridx = lax.broadcasted_iota(jnp.int32, (nrows, MOE_SUB), 0).astype(F32)
        hit = [[keys[j][:, rows] - base == ridx for rows in subs] for j in range(MOE_HALF)]
        pts = [[jnp.where(m, 1.0, 0.0).astype(BF16) for m in row] for row in hit]
        pcat = [jnp.concatenate([pts[j][s] for j in range(MOE_HALF)], axis=0) for s in range(nsub)]
        xall = [jnp.dot(p, x_ref[rows, :], preferred_element_type=F32).astype(BF16)
                for p, rows in zip(pcat, subs)]
        ys = []
        for j in range(MOE_HALF):
            part = slice(j * nrows, (j + 1) * nrows)
            xg = jnp.concatenate([xa[part] for xa in xall], axis=0)
            cg = jnp.concatenate([jnp.sum(jnp.where(m, wrow[j][:, rows], 0.0), axis=-1, keepdims=True)
                                  for m, rows in zip(hit[j], subs)], axis=0)
            act = (_silu(jnp.dot(xg, wg_ref[0, j], preferred_element_type=F32))
                   * jnp.dot(xg, wu_ref[0, j], preferred_element_type=F32) * cg)
            ys.append(_mm(act, wd_ref[0, j]))
        for s, rows in enumerate(subs):
            part = slice(s * nrows, (s + 1) * nrows)
            ycat = jnp.concatenate([ys[j][part] for j in range(MOE_HALF)], axis=0)
            o_ref[rows, :] += tn(pcat[s], ycat.astype(BF16))

    count = cnt_s[2 * g + hf]

    @pl.when(count > 0)
    def _():
        run_pass(jnp.float32(0.0), MOE_ROWS)

    def extra(p, carry):
        run_pass((MOE_ROWS + p * MOE_EXTRA).astype(F32), MOE_EXTRA)
        return carry

    lax.fori_loop(0, (jnp.maximum(count - MOE_ROWS, 0) + MOE_EXTRA - 1) // MOE_EXTRA, extra, 0)


def _moe(xn, comb, wg, wu, wd, layer, tm=2048):
    t, d = xn.shape
    _, ng, eg, _, de = wg.shape
    row = lambda w: pl.BlockSpec((tm, w), lambda i, g, hf: (i, 0))
    wspec = lambda a, b: pl.BlockSpec((None, 1, MOE_HALF, a, b), lambda i, g, hf: (layer, g, hf, 0, 0))
    return pl.pallas_call(
        _moe_body,
        out_shape=jax.ShapeDtypeStruct((t, d), F32),
        grid=(t // tm, ng, eg // MOE_HALF),
        in_specs=[row(d), row(ROUTE_LANES), wspec(d, de), wspec(d, de), wspec(de, d)],
        out_specs=row(d),
        scratch_shapes=[pltpu.VMEM((MOE_SUB, MOE_SUB), BF16), pltpu.VMEM((ROUTE_LANES, tm), F32),
                        pltpu.VMEM((ROUTE_LANES, tm), F32), pltpu.SMEM((N_EXPERTS // MOE_HALF,), jnp.int32)],
        compiler_params=_params(("arbitrary", "arbitrary", "arbitrary")),
        name="moe",
    )(xn, comb, wg, wu, wd)


def _norm_body(h_ref, dl_ref, w_ref, o_ref):
    o_ref[...] = _rms(h_ref[...] + dl_ref[...], w_ref[...])


def _final_norm(h, delta, w, tm=1024):
    t, d = h.shape
    row = pl.BlockSpec((tm, d), lambda i: (i, 0))
    return pl.pallas_call(
        _norm_body,
        out_shape=jax.ShapeDtypeStruct((t, d), F32),
        grid=(t // tm,),
        in_specs=[row, row, _const((1, d))],
        out_specs=row,
        compiler_params=_params(("parallel",)),
        name="final_norm",
    )(h, delta, w)


def _mask_consts():
    i256 = jnp.arange(GW)
    same_head = (i256[:, None] // HEAD_DIM) == (i256[None, :] // HEAD_DIM)
    t64 = jnp.arange(RW_CHUNK)
    s_w = i256 % RW_CHUNK
    h128 = jnp.arange(128)
    return {
        "bd": same_head.astype(F32),
        "hm": same_head.astype(F32) / HEAD_DIM,
        "hones": same_head.astype(BF16),
        "tri64": (t64[:, None] >= t64[None, :]).astype(F32),
        "lowi": (t64[:, None] >= s_w[None, :]).astype(F32),
        "lows": (t64[:, None] > s_w[None, :]).astype(F32),
        "eyew": (t64[:, None] == s_w[None, :]).astype(F32),
        "tri16": (jnp.arange(HG_BLOCK)[:, None] >= jnp.arange(HG_BLOCK)[None, :]).astype(F32),
        "tri128": (h128[:, None] >= h128[None, :]).astype(F32),
        "expand": (h128[:, None] == (i256[None, :] // HEAD_DIM)).astype(F32),
        "gsel": ((h128[:, None] // 64) == (i256[None, :] // 128)).astype(F32),
    }


def _s5_params(lam_re, lam_im, log_dt, b_re, b_im, c_re, c_im, d_skip, w_glu, b_glu):
    lr = jnp.minimum(lam_re, -1e-4)
    li = lam_im
    dt = jnp.exp(log_dt)[:, None]
    mag = jnp.exp(lr * dt)
    ar, ai = mag * jnp.cos(li * dt), mag * jnp.sin(li * dt)
    den = lr * lr + li * li
    nr = ar - 1.0
    er, ei = (nr * lr + ai * li) / den, (ai * lr - nr * li) / den
    bbr = er[..., None] * b_re - ei[..., None] * b_im
    bbi = er[..., None] * b_im + ei[..., None] * b_re
    eye = jnp.eye(lam_re.shape[0], dtype=F32)
    pack_b = lambda m: jnp.einsum("gph,gk->ghkp", m, eye).reshape(GW, S5_STATE_W)
    pack_c = lambda m: jnp.einsum("ghp,gk->gpkh", m, eye).reshape(S5_STATE_W, GW)
    return {
        "bbd": jnp.concatenate([pack_b(bbr), pack_b(bbi)], axis=1).astype(BF16),
        "cbd": jnp.concatenate([pack_c(c_re), -pack_c(c_im)], axis=0).astype(BF16),
        "ar": ar.reshape(1, S5_STATE_W), "ai": ai.reshape(1, S5_STATE_W),
        "d": d_skip.reshape(1, GW), "w_glu": w_glu.astype(BF16), "b_glu": b_glu.reshape(1, GW),
    }


def kernel(x, ln1_w, w_in, rw_mu, rw_w0, rw_w2, rw_a0, rw_a2, rw_g2, rw_k_k, rw_k_a, rw_r_k, rw_lnx_w, rw_lnx_b, s5_lam_re, s5_lam_im, s5_log_dt, s5_b_re, s5_b_im, s5_c_re, s5_c_im, s5_d, s5_w_glu, s5_b_glu, m_conv_w, m_conv_b, m_dt_bias, m_a_log, m_d, m_norm_w, hg_lb_logits, hg_norm_w, w_out, ln2_w, moe_w_rg, moe_b_rg, moe_w_re, moe_b_re, moe_w_gate, moe_w_up, moe_w_down, lnf_w):
    bsz, seq, d = x.shape
    depth = w_in.shape[0]
    consts = _mask_consts()
    lbs = jax.nn.softmax(hg_lb_logits.astype(F32), axis=0)
    lbs = jnp.cumsum(lbs, axis=0) - lbs[0:1]
    row = lambda v: v.reshape(1, -1).astype(F32)
    n_dt = N_HEADS
    h = x.reshape(bsz * seq, d)
    delta = None
    wg16, wu16, wd16 = (w.astype(BF16) for w in (moe_w_gate, moe_w_up, moe_w_down))
    for l in range(depth):
        c_s5, c_m, c_hg = 4 * GW, 5 * GW, 8 * GW + n_dt
        w_rw = w_in[l, :, :c_s5].astype(BF16)
        w_s5 = w_in[l, :, c_s5:c_m].astype(BF16)
        w_m = jnp.pad(w_in[l, :, c_m:c_hg].astype(BF16), ((0, 0), (0, LANES - n_dt)))
        w_hg = w_in[l, :, c_hg:].astype(BF16)
        h, xn1 = _prenorm(h, delta, row(ln1_w[l]))
        xn1 = xn1.reshape(bsz, seq, d)
        rw = {"mu": row(rw_mu[l]), "w0": row(rw_w0[l]), "w2": rw_w2[l], "a0": row(rw_a0[l]),
              "a2": rw_a2[l], "g2": rw_g2[l], "k_k": row(rw_k_k[l]), "k_a": row(rw_k_a[l]),
              "r_k": row(rw_r_k[l]), "lnx_w": row(rw_lnx_w[l]), "lnx_b": row(rw_lnx_b[l])}
        y_rw = _rwkv(xn1, w_rw, rw, consts)
        y_s5 = _s5(xn1, w_s5, _s5_params(s5_lam_re[l], s5_lam_im[l], s5_log_dt[l], s5_b_re[l], s5_b_im[l],
                                    s5_c_re[l], s5_c_im[l], s5_d[l], s5_w_glu[l], s5_b_glu[l]))
        pad_h = lambda v: jnp.pad(v.astype(F32), (0, LANES - n_dt)).reshape(1, LANES)
        mp = {"conv_w": m_conv_w[l], "conv_b": row(m_conv_b[l]), "dt_bias": pad_h(m_dt_bias[l]),
              "a_neg": pad_h(-jnp.exp(m_a_log[l].astype(F32))),
              "d_exp": row(jnp.repeat(m_d[l], HEAD_DIM)), "norm_w": row(m_norm_w[l])}
        y_m = _mamba(xn1, w_m, mp, consts)
        y_hg = _hgrn(xn1, w_hg, row(lbs[l]), row(hg_norm_w[l]), consts)
        ys = [y.reshape(bsz * seq, GW) for y in (y_rw, y_s5, y_m, y_hg)]
        n_route = N_EXPERT_GROUPS + N_EXPERTS
        wr = jnp.pad(jnp.concatenate([moe_w_rg[l], moe_w_re[l]], axis=1), ((0, 0), (0, ROUTE_LANES - n_route)))
        wr_hi = wr.astype(BF16)
        wr_lo = (wr - wr_hi.astype(F32)).astype(BF16)
        br = jnp.pad(jnp.concatenate([moe_b_rg[l], moe_b_re[l]]), (0, ROUTE_LANES - n_route)).reshape(1, -1)
        h, xn, comb = _outproj(h, ys, w_out[l].astype(BF16), row(ln2_w[l]),
                               jnp.concatenate([wr_hi, wr_lo], axis=1), br)
        delta = _moe(xn, comb, wg16, wu16, wd16, l)
    return _final_norm(h, delta, row(lnf_w)).reshape(bsz, seq, d)
```

```python
import functools
import math

import jax
import jax.numpy as jnp
from jax import lax
from jax.experimental import pallas as pl
from jax.experimental.pallas import tpu as pltpu

F32 = jnp.float32
BF16 = jnp.bfloat16

NORM_EPS = 1e-6
GW = 256
HEAD_DIM = 64
N_HEADS = GW // HEAD_DIM
LANES = 128
SUBLANES = 8
RW_GN_EPS = 64e-5
HG_F_FLOOR = 1e-20
S5_STATE_W = 1024
M_CHUNK = 128
N_EXPERTS = 32
EXPERTS_PER_GROUP = 8
N_EXPERT_GROUPS = 4
ROUTE_LANES = LANES
ROUTE_OFF = N_EXPERT_GROUPS

VMEM_LIMIT = 56 * 1024 * 1024


def _mm(a, b):
    return jnp.dot(a.astype(BF16), b.astype(BF16), preferred_element_type=F32)


def _mm_nt(a, b):
    return lax.dot_general(a.astype(BF16), b.astype(BF16), (((1,), (1,)), ((), ())),
                           preferred_element_type=F32)


def _mm_tn(a, b):
    return lax.dot_general(a.astype(BF16), b.astype(BF16), (((0,), (0,)), ((), ())),
                           preferred_element_type=F32)


def _parts(x, n):
    out, rem = [], x
    for i in range(n):
        p = rem.astype(BF16)
        out.append(p)
        if i + 1 < n:
            rem = rem - p.astype(F32)
    return out


def _mm_x(a, b, na, nb):
    pa, pb = _parts(a, na), _parts(b, nb)
    acc = None
    for i in range(na):
        for j in range(nb):
            if i + j < max(na, nb):
                t = jnp.dot(pa[i], pb[j], preferred_element_type=F32)
                acc = t if acc is None else acc + t
    return acc


def _sigmoid(x):
    return 1.0 / (1.0 + jnp.exp(-x))


def _silu(x):
    return x * _sigmoid(x)


def _softplus(x):
    return jnp.maximum(x, 0.0) + jnp.log1p(jnp.exp(-jnp.abs(x)))


def _rms(x, w):
    ms = jnp.mean(x * x, axis=-1, keepdims=True)
    return x * lax.rsqrt(ms + NORM_EPS) * w


def _stack4(x, bd16):
    xb = x.astype(BF16)
    return jnp.concatenate([xb, xb, xb, xb], axis=0) * bd16


def _const(shape):
    return pl.BlockSpec(shape, lambda *_: (0,) * len(shape))


def _params(sem):
    return pltpu.CompilerParams(dimension_semantics=sem, vmem_limit_bytes=VMEM_LIMIT)


def _prenorm_body(*refs, with_delta):
    if with_delta:
        h_ref, dl_ref, lnw_ref, hn_ref, xn_ref = refs
        h = h_ref[...] + dl_ref[...]
        hn_ref[...] = h
    else:
        h_ref, lnw_ref, xn_ref = refs
        h = h_ref[...]
    xn_ref[...] = _rms(h, lnw_ref[...]).astype(BF16)


def _prenorm(h, delta, lnw, tm=1024):
    t, d = h.shape
    row = pl.BlockSpec((tm, d), lambda i: (i, 0))
    xn_shape = jax.ShapeDtypeStruct((t, d), BF16)
    with_delta = delta is not None
    out = pl.pallas_call(
        functools.partial(_prenorm_body, with_delta=with_delta),
        out_shape=(jax.ShapeDtypeStruct((t, d), F32), xn_shape) if with_delta else xn_shape,
        grid=(t // tm,),
        in_specs=([row, row] if with_delta else [row]) + [_const((1, d))],
        out_specs=(row, row) if with_delta else row,
        compiler_params=_params(("parallel",)),
        name="prenorm",
    )(*((h, delta) if with_delta else (h,)), lnw)
    return out if with_delta else (h, out)


RW_CHUNK = 64


def _rwkv_body(x_ref, win_ref, mu_ref, w0_ref, w2_ref, a0_ref, a2_ref, g2_ref, kk_ref, ka_ref, rk_ref,
               lnw_ref, lnb_ref, bd_ref, bd16_ref, hm_ref, tri_ref, lowi_ref, lows_ref, eye_ref,
               o_ref, carry_ref, st_ref, r_s, k_s, v_s, kn_s, kb_s, wl_s, y_s):
    nb, tt, d = x_ref.shape

    @pl.when(pl.program_id(1) == 0)
    def _():
        carry_ref[...] = jnp.zeros_like(carry_ref)
        st_ref[...] = jnp.zeros_like(st_ref)

    proj = jnp.dot(x_ref[...].reshape(nb * tt, d), win_ref[...], preferred_element_type=F32)
    rows = lax.broadcasted_iota(jnp.int32, (tt, 4 * GW), 0)
    mixed = []
    for b in range(nb):
        p = proj[b * tt:(b + 1) * tt]
        prev = jnp.where(rows == 0, carry_ref[b, 0:1, :], pltpu.roll(p, 1, axis=0))
        carry_ref[b, 0:1, :] = p[tt - 1:tt, :]
        mixed.append(p + (prev - p) * mu_ref[...])
    p = jnp.concatenate(mixed, axis=0)
    r = p[:, 0:GW]
    k = p[:, GW:2 * GW]
    v = p[:, 2 * GW:3 * GW]
    wl = p[:, 3 * GW:3 * GW + 64]
    al = p[:, 3 * GW + 64:3 * GW + 128]
    gl = p[:, 3 * GW + 128:]
    hm = hm_ref[...]
    w_log = -math.exp(-0.5) * _sigmoid(w0_ref[...] + _mm_x(jnp.tanh(wl), w2_ref[...], 2, 2))
    a = _sigmoid(a0_ref[...] + _mm(al, a2_ref[...]))
    gate = _mm(_sigmoid(gl), g2_ref[...])
    kn = k * kk_ref[...]
    nrm = jnp.sqrt(_mm_x(kn * kn, hm, 2, 1) * float(HEAD_DIM))
    kn = kn / jnp.maximum(nrm, 1e-12)
    k = k * (1.0 + (a - 1.0) * ka_ref[...])
    r_s[...] = r
    k_s[...] = k
    v_s[...] = v
    kn_s[...] = kn
    kb_s[...] = kn * a
    wl_s[...] = w_log

    def chunk(c, carry):
        seqs = range(nb)
        sls = [pl.ds(pl.multiple_of(b * tt + c * RW_CHUNK, RW_CHUNK), RW_CHUNK) for b in seqs]
        ld = lambda s: [s[sl, :] for sl in sls]
        r_c, k_c, v_c, kn_c, kb_c, wl_c = (ld(s) for s in (r_s, k_s, v_s, kn_s, kb_s, wl_s))
        bd = bd_ref[...]
        bd16 = bd16_ref[...]
        lows = lows_ref[...]
        lowi = lowi_ref[...]
        st4 = lambda xs: [_stack4(x, bd16) for x in xs]
        gc = [_mm_x(tri_ref[...], w, 1, 3) for w in wl_c]
        g_end = [g[RW_CHUNK - 1:RW_CHUNK, :] for g in gc]
        inv = [jnp.exp(-g) for g in gc]
        lhs = [jnp.concatenate([-kn * jnp.exp(g - w), r * jnp.exp(g)], axis=0)
               for kn, r, g, w in zip(kn_c, r_c, gc, wl_c)]
        rhs = [jnp.concatenate([_stack4(kb * i, bd16), _stack4(k * i, bd16)], axis=0)
               for kb, k, i in zip(kb_c, k_c, inv)]
        gram = [_mm_nt(a, b) for a, b in zip(lhs, rhs)]
        state = [st_ref[b] for b in seqs]
        from_state = [_mm_nt(a, s) for a, s in zip(lhs, state)]
        a_ab = [g[:RW_CHUNK, :GW] * lows for g in gram]
        pw = [eye_ref[...] + a for a in a_ab]
        q = a_ab
        qs = st4(q)
        for _ in range(5):
            q = [_mm(a, b) for a, b in zip(q, qs)]
            qs = st4(q)
            pw = [p + _mm(p, b) for p, b in zip(pw, qs)]
        vst = st4(v_c)
        rhs_u = [f[:RW_CHUNK] + _mm(g[:RW_CHUNK, GW:] * lows, vs) for f, g, vs in zip(from_state, gram, vst)]
        u = [_mm(p, x) for p, x in zip(pw, st4(rhs_u))]
        ust = st4(u)
        for b in seqs:
            y_s[sls[b], :] = (from_state[b][RW_CHUNK:] + _mm(gram[b][RW_CHUNK:, :GW] * lowi, ust[b])
                              + _mm(gram[b][RW_CHUNK:, GW:] * lowi, vst[b]))
        for b in seqs:
            to_end = jnp.exp(g_end[b] - gc[b])
            upd = _mm_tn(jnp.concatenate([u[b], v_c[b]], axis=0),
                         jnp.concatenate([kb_c[b] * to_end, k_c[b] * to_end], axis=0))
            st_ref[b] = state[b] * jnp.exp(g_end[b]) + upd * bd
        return carry

    lax.fori_loop(0, tt // RW_CHUNK, chunk, 0)

    y = y_s[...]
    mean = _mm_x(y, hm, 2, 1)
    d = y - mean
    var = _mm_x(d * d, hm, 2, 1)
    yn = d * lax.rsqrt(var + RW_GN_EPS) * lnw_ref[...] + lnb_ref[...]
    bonus = _mm_x(r * k * rk_ref[...], hm, 2, 1) * float(HEAD_DIM) * v
    o_ref[...] = ((yn + bonus) * gate).astype(BF16).reshape(nb, tt, GW)


def _rwkv(xn3, w_in, prm, consts, nb=8, tt=128):
    b, l, d = xn3.shape
    vec = _const((1, GW))
    scr = pltpu.VMEM((nb * tt, GW), F32)
    return pl.pallas_call(
        _rwkv_body,
        out_shape=jax.ShapeDtypeStruct((b, l, GW), BF16),
        grid=(b // nb, l // tt),
        in_specs=[pl.BlockSpec((nb, tt, d), lambda i, j: (i, j, 0)), _const((d, 4 * GW)),
                  _const((1, 4 * GW)), vec, _const((64, GW)), vec, _const((64, GW)),
                  _const((128, GW)), vec, vec, vec, vec, vec,
                  _const((GW, GW)), _const((GW, GW)), _const((GW, GW)), _const((RW_CHUNK, RW_CHUNK)),
                  _const((RW_CHUNK, GW)), _const((RW_CHUNK, GW)), _const((RW_CHUNK, GW))],
        out_specs=pl.BlockSpec((nb, tt, GW), lambda i, j: (i, j, 0)),
        scratch_shapes=[pltpu.VMEM((nb, SUBLANES, 4 * GW), F32), pltpu.VMEM((nb, GW, GW), F32)] + [scr] * 7,
        compiler_params=_params(("parallel", "arbitrary")),
        name="rwkv7",
    )(xn3, w_in, prm["mu"], prm["w0"], prm["w2"], prm["a0"], prm["a2"], prm["g2"], prm["k_k"],
      prm["k_a"], prm["r_k"], prm["lnx_w"], prm["lnx_b"],
      consts["bd"], consts["hones"], consts["hm"], consts["tri64"], consts["lowi"], consts["lows"], consts["eyew"])


def _s5_body(x_ref, win_ref, bbd_ref, ar_ref, ai_ref, cbd_ref, d_ref, wglu_ref, bglu_ref, o_ref,
             ub_s, ut_s, xr_s, xi_s, ot_s, st_ref):
    nb, tt, d = x_ref.shape
    w = win_ref.shape[1]
    ncb = w // LANES
    assert nb == SUBLANES

    @pl.when(pl.program_id(0) == 0)
    def _():
        st_ref[...] = jnp.zeros_like(st_ref)

    u_bt = jnp.dot(x_ref[...].reshape(nb * tt, d), win_ref[...], preferred_element_type=F32)
    for cb in range(ncb):
        ub_s[cb] = u_bt[:, cb * LANES:(cb + 1) * LANES]

    def regroup(t, carry):
        for cb in range(ncb):
            ut_s[cb, pl.ds(pl.multiple_of(t * nb, nb), nb), :] = ub_s[cb, pl.ds(t, nb, stride=tt), :]
        return carry

    lax.fori_loop(0, tt, regroup, 0, unroll=8)
    u = jnp.concatenate([ut_s[cb] for cb in range(ncb)], axis=-1)
    bu = _mm(u, bbd_ref[...])
    xr_s[...] = bu[:, :S5_STATE_W]
    xi_s[...] = bu[:, S5_STATE_W:]
    ar = jnp.broadcast_to(ar_ref[...], (nb, S5_STATE_W))
    ai = jnp.broadcast_to(ai_ref[...], (nb, S5_STATE_W))

    def step(t, carry):
        xr, xi = carry
        rows = pl.ds(pl.multiple_of(t * nb, nb), nb)
        nr = ar * xr - ai * xi + xr_s[rows, :]
        ni = ar * xi + ai * xr + xi_s[rows, :]
        xr_s[rows, :] = nr
        xi_s[rows, :] = ni
        return nr, ni

    xr, xi = lax.fori_loop(0, tt, step, (st_ref[0], st_ref[1]), unroll=4)
    st_ref[0] = xr
    st_ref[1] = xi
    cbd = cbd_ref[...]
    y = _mm(xr_s[...], cbd[:S5_STATE_W]) + _mm(xi_s[...], cbd[S5_STATE_W:]) + d_ref[...] * u
    y = 0.5 * y * (1.0 + jnp.tanh(math.sqrt(2.0 / math.pi) * (y + 0.044715 * (y * y * y))))
    z = _mm(y, wglu_ref[...]) + bglu_ref[...]
    out = y * _sigmoid(z)
    for cb in range(ncb):
        ot_s[cb] = out[:, cb * LANES:(cb + 1) * LANES]
    for b in range(nb):
        for cb in range(ncb):
            o_ref[b, :, cb * LANES:(cb + 1) * LANES] = ot_s[cb, pl.ds(b, tt, stride=nb), :].astype(BF16)


def _s5(xn3, w_in, prm, tt=128):
    b, l, d = xn3.shape
    slab = pltpu.VMEM((GW // LANES, b * tt, LANES), F32)
    wide = pltpu.VMEM((b * tt, S5_STATE_W), F32)
    return pl.pallas_call(
        _s5_body,
        out_shape=jax.ShapeDtypeStruct((b, l, GW), BF16),
        grid=(l // tt,),
        in_specs=[pl.BlockSpec((b, tt, d), lambda j: (0, j, 0)), _const((d, GW)),
                  _const((GW, 2 * S5_STATE_W)), _const((1, S5_STATE_W)), _const((1, S5_STATE_W)),
                  _const((2 * S5_STATE_W, GW)), _const((1, GW)), _const((GW, GW)), _const((1, GW))],
        out_specs=pl.BlockSpec((b, tt, GW), lambda j: (0, j, 0)),
        scratch_shapes=[slab, slab, wide, wide, slab, pltpu.VMEM((2, b, S5_STATE_W), F32)],
        compiler_params=_params(("arbitrary",)),
        name="s5",
    )(xn3, w_in, prm["bbd"], prm["ar"], prm["ai"], prm["cbd"], prm["d"], prm["w_glu"], prm["b_glu"])


def _mamba_body(x_ref, win_ref, cw_ref, cb_ref, dtb_ref, aneg_ref, dexp_ref, nw_ref,
                tri_ref, exp_ref, gsel_ref, o_ref, carry_ref, st_ref):
    nb, tt, d = x_ref.shape
    seqs = range(nb)

    @pl.when(pl.program_id(1) == 0)
    def _():
        carry_ref[...] = jnp.zeros_like(carry_ref)
        st_ref[...] = jnp.zeros_like(st_ref)

    proj = jnp.dot(x_ref[...].reshape(nb * tt, d), win_ref[...], preferred_element_type=F32)
    cw = 2 * GW
    rows = lax.broadcasted_iota(jnp.int32, (tt, cw), 0)
    pad = jnp.zeros((tt - SUBLANES, cw), F32)
    xcs = []
    for b in seqs:
        xbc = proj[b * tt:(b + 1) * tt, GW:GW + cw]
        tail = carry_ref[b]
        conv = xbc * cw_ref[3:4, :]
        for s in (1, 2, 3):
            head = jnp.concatenate([pltpu.roll(tail, s, axis=0), pad], axis=0)
            shifted = jnp.where(rows < s, head, pltpu.roll(xbc, s, axis=0))
            conv = conv + shifted * cw_ref[3 - s:4 - s, :]
        carry_ref[b] = xbc[tt - SUBLANES:, :]
        xcs.append(_silu(conv + cb_ref[...]))
    xs = [xc[:, :GW] for xc in xcs]
    bm = [xc[:, GW:GW + 128] for xc in xcs]
    cm = [xc[:, GW + 128:] for xc in xcs]
    expand = exp_ref[...]
    dt = [_softplus(proj[b * tt:(b + 1) * tt, GW + cw:] + dtb_ref[...]) for b in seqs]
    cs = [_mm_x(tri_ref[...], x * aneg_ref[...], 1, 3) for x in dt]
    cs_t = [c.T for c in cs]
    cs_end = [c[tt - 1:tt, :] for c in cs]
    wide = [_mm_x(jnp.concatenate([x, c, ce - c, jnp.broadcast_to(ce, (SUBLANES, 128))], axis=0), expand, 3, 1)
            for x, c, ce in zip(dt, cs, cs_end)]
    xdt = [x * w[:tt] for x, w in zip(xs, wide)]
    lane = lax.broadcasted_iota(jnp.int32, (tt, 128), 1)
    tril = lax.broadcasted_iota(jnp.int32, (tt, tt), 0) >= lax.broadcasted_iota(jnp.int32, (tt, tt), 1)
    lane_w = lax.broadcasted_iota(jnp.int32, (tt, GW), 1)
    state = [st_ref[b] for b in seqs]
    y = [_mm(c, s) * jnp.exp(w[tt:2 * tt]) + dexp_ref[...] * x for c, s, w, x in zip(cm, state, wide, xs)]
    for g in range(2):
        cbm = [_mm_nt(jnp.where(lane // 64 == g, c, 0.0), b_) for c, b_ in zip(cm, bm)]
        for h in (2 * g, 2 * g + 1):
            decay = [jnp.where(tril, jnp.exp(jnp.minimum(c[:, h:h + 1] - ct[h:h + 1, :], 0.0)), 0.0)
                     for c, ct in zip(cs, cs_t)]
            y = [yy + jnp.where(lane_w // HEAD_DIM == h, _mm(m * dc, xd), 0.0)
                 for yy, m, dc, xd in zip(y, cbm, decay, xdt)]
    upd = [_mm_tn(b_, xd * jnp.exp(w[2 * tt:3 * tt])) for b_, xd, w in zip(bm, xdt, wide)]
    nw = nw_ref[...]
    half = GW // 2
    for b in seqs:
        st_ref[b] = state[b] * jnp.exp(wide[b][3 * tt:3 * tt + 1]) + upd[b] * gsel_ref[...]
        yb = y[b] * _silu(proj[b * tt:(b + 1) * tt, :GW])
        o_ref[b] = jnp.concatenate([_rms(yb[:, :half], nw[:, :half]), _rms(yb[:, half:], nw[:, half:])],
                                   axis=-1).astype(BF16)


def _mamba(xn3, w_in, prm, consts, nb=4):
    b, l, d = xn3.shape
    tt = M_CHUNK
    wcols = w_in.shape[1]
    return pl.pallas_call(
        _mamba_body,
        out_shape=jax.ShapeDtypeStruct((b, l, GW), BF16),
        grid=(b // nb, l // tt),
        in_specs=[pl.BlockSpec((nb, tt, d), lambda i, j: (i, j, 0)), _const((d, wcols)),
                  _const((4, 2 * GW)), _const((1, 2 * GW)), _const((1, 128)), _const((1, 128)),
                  _const((1, GW)), _const((1, GW)),
                  _const((tt, tt)), _const((128, GW)), _const((128, GW))],
        out_specs=pl.BlockSpec((nb, tt, GW), lambda i, j: (i, j, 0)),
        scratch_shapes=[pltpu.VMEM((nb, SUBLANES, 2 * GW), F32), pltpu.VMEM((nb, 128, GW), F32)],
        compiler_params=_params(("parallel", "arbitrary")),
        name="mamba2",
    )(xn3, w_in, prm["conv_w"], prm["conv_b"], prm["dt_bias"], prm["a_neg"], prm["d_exp"],
      prm["norm_w"], consts["tri128"], consts["expand"], consts["gsel"])


HG_BLOCK = 16


def _hgrn_body(x_ref, win_ref, lb_ref, nw_ref, bd_ref, hm_ref, hones_ref, tri_ref, o_ref,
               st_ref, q_s, k_s, v_s, lf_s, o_s):
    nb, tt, d = x_ref.shape

    @pl.when(pl.program_id(1) == 0)
    def _():
        st_ref[...] = jnp.zeros_like(st_ref)

    p = jnp.dot(x_ref[...].reshape(nb * tt, d), win_ref[...], preferred_element_type=F32)
    lb = lb_ref[...]
    hf = p[:, GW:2 * GW]
    q_s[...] = _silu(p[:, :GW])
    k_s[...] = (1.0 - lb) * _sigmoid(-hf)
    v_s[...] = p[:, 2 * GW:3 * GW]
    lf_s[...] = jnp.log(jnp.maximum(lb + (1.0 - lb) * _sigmoid(hf), HG_F_FLOOR))

    def block(n, carry):
        seqs = range(nb)
        sls = [pl.ds(pl.multiple_of(b * tt + n * HG_BLOCK, HG_BLOCK), HG_BLOCK) for b in seqs]
        ld = lambda s: [s[sl, :] for sl in sls]
        q_b, k_b, v_b, lf_b = (ld(s) for s in (q_s, k_s, v_s, lf_s))
        g = [_mm_x(tri_ref[...], lf, 1, 3) for lf in lf_b]
        g_end = [x[HG_BLOCK - 1:HG_BLOCK, :] for x in g]
        state = [st_ref[b] for b in seqs]
        o = [_mm_nt(q * jnp.exp(x), s) for q, x, s in zip(q_b, g, state)]
        half = HG_BLOCK // 2
        lower = lambda a: a.reshape(2, half, GW)[1]
        prods, att = [], []
        for q, k, x in zip(q_b, k_b, g):
            ql, xl = lower(q), lower(x)
            full = [q * jnp.exp(jnp.minimum(x - x[j:j + 1, :], 0.0)) * k[j:j + 1, :] for j in range(half)]
            low = [ql * jnp.exp(jnp.minimum(xl - x[j:j + 1, :], 0.0)) * k[j:j + 1, :] for j in range(half, HG_BLOCK)]
            prods.append(jnp.concatenate(full + low, axis=0))
        att = [_mm(p, hones_ref[...]) for p in prods]
        upd = [_mm_tn(v, k * jnp.exp(ge - x)) for v, k, ge, x in zip(v_b, k_b, g_end, g)]
        row = lax.broadcasted_iota(jnp.int32, (HG_BLOCK, GW), 0)
        rowl = lax.broadcasted_iota(jnp.int32, (half, GW), 0) + half
        for b in seqs:
            a_full = att[b][:HG_BLOCK * half].reshape(half, HG_BLOCK, GW)
            a_low = att[b][HG_BLOCK * half:].reshape(half, half, GW)
            ob = o[b]
            for j in range(half):
                ob = ob + jnp.where(row >= j, a_full[j], 0.0) * v_b[b][j:j + 1, :]
            add = jnp.zeros((half, GW), F32)
            for j in range(half, HG_BLOCK):
                add = add + jnp.where(rowl >= j, a_low[j - half], 0.0) * v_b[b][j:j + 1, :]
            o_s[sls[b], :] = ob + jnp.concatenate([jnp.zeros((half, GW), F32), add], axis=0)
            st_ref[b] = state[b] * jnp.exp(g_end[b]) + upd[b] * bd_ref[...]
        return carry

    lax.fori_loop(0, tt // HG_BLOCK, block, 0)
    o = o_s[...]
    ms = _mm_x(o * o, hm_ref[...], 2, 1)
    o_ref[...] = (o * lax.rsqrt(ms + NORM_EPS) * nw_ref[...] * _silu(p[:, 3 * GW:])).astype(BF16).reshape(nb, tt, GW)


def _hgrn(xn3, w_in, lb, nw, consts, nb=8, tt=128):
    b, l, d = xn3.shape
    return pl.pallas_call(
        _hgrn_body,
        out_shape=jax.ShapeDtypeStruct((b, l, GW), BF16),
        grid=(b // nb, l // tt),
        in_specs=[pl.BlockSpec((nb, tt, d), lambda i, j: (i, j, 0)), _const((d, 4 * GW)),
                  _const((1, GW)), _const((1, GW)), _const((GW, GW)), _const((GW, GW)),
                  _const((GW, GW)), _const((HG_BLOCK, HG_BLOCK))],
        out_specs=pl.BlockSpec((nb, tt, GW), lambda i, j: (i, j, 0)),
        scratch_shapes=[pltpu.VMEM((nb, GW, GW), F32)] + [pltpu.VMEM((nb * tt, GW), F32)] * 5,
        compiler_params=_params(("parallel", "arbitrary")),
        name="hgrn2",
    )(xn3, w_in, lb, nw, consts["bd"], consts["hm"], consts["hones"], consts["tri16"])


def _outproj_body(h_ref, y1_ref, y2_ref, y3_ref, y4_ref, wo_ref, ln2_ref, wr_ref, br_ref,
                  hn_ref, xn_ref, comb_ref):
    y = jnp.concatenate([y1_ref[...], y2_ref[...], y3_ref[...], y4_ref[...]], axis=-1)
    h = h_ref[...] + _mm(y, wo_ref[...])
    hn_ref[...] = h
    xn = _rms(h, ln2_ref[...])
    xn_ref[...] = xn.astype(BF16)
    xh, xl = _parts(xn, 2)
    wr = wr_ref[...]
    first = jnp.dot(xh, wr, preferred_element_type=F32)
    logits = (first[:, :ROUTE_LANES] + first[:, ROUTE_LANES:]
              + jnp.dot(xl, wr[:, :ROUTE_LANES], preferred_element_type=F32) + br_ref[...])
    lane = lax.broadcasted_iota(jnp.int32, logits.shape, 1)
    neg = -jnp.inf
    big = ROUTE_LANES
    glog = jnp.where(lane < N_EXPERT_GROUPS, logits, neg)
    gmax = jnp.max(glog, axis=-1, keepdims=True)
    g_w = 1.0 / jnp.sum(jnp.exp(glog - gmax), axis=-1, keepdims=True)
    g_idx = jnp.min(jnp.where(glog == gmax, lane, big), axis=-1, keepdims=True)
    lo = ROUTE_OFF + EXPERTS_PER_GROUP * g_idx
    elog = jnp.where((lane >= lo) & (lane < lo + EXPERTS_PER_GROUP), logits, neg)
    m1 = jnp.max(elog, axis=-1, keepdims=True)
    i1 = jnp.min(jnp.where(elog == m1, lane, big), axis=-1, keepdims=True)
    elog2 = jnp.where(lane == i1, neg, elog)
    m2 = jnp.max(elog2, axis=-1, keepdims=True)
    i2 = jnp.min(jnp.where(elog2 == m2, lane, big), axis=-1, keepdims=True)
    e2 = jnp.exp(m2 - m1)
    w1 = 1.0 / (1.0 + e2)
    w2 = e2 / (1.0 + e2)
    comb_ref[...] = (g_w * (jnp.where(lane == i1 - lo, w1, 0.0) + jnp.where(lane == i2 - lo, w2, 0.0))
                     + jnp.where(lane == EXPERTS_PER_GROUP, g_idx.astype(F32), 0.0))


def _outproj(h, ys, wo, ln2, wr, br, tm=512):
    t, d = h.shape
    row = lambda w: pl.BlockSpec((tm, w), lambda i: (i, 0))
    return pl.pallas_call(
        _outproj_body,
        out_shape=(jax.ShapeDtypeStruct((t, d), F32), jax.ShapeDtypeStruct((t, d), BF16),
                   jax.ShapeDtypeStruct((t, ROUTE_LANES), F32)),
        grid=(t // tm,),
        in_specs=[row(d), row(GW), row(GW), row(GW), row(GW), _const((d, d)), _const((1, d)),
                  _const((d, 2 * ROUTE_LANES)), _const((1, ROUTE_LANES))],
        out_specs=(row(d), row(d), row(ROUTE_LANES)),
        compiler_params=_params(("parallel",)),
        name="outproj_router",
    )(h, *ys, wo, ln2, wr, br)


MOE_SUB = 512
MOE_ROWS = 64
MOE_EXTRA = 16
MOE_HALF = EXPERTS_PER_GROUP // 2
COMB_GROUP_LANE = EXPERTS_PER_GROUP


def _moe_body(x_ref, comb_ref, wg_ref, wu_ref, wd_ref, o_ref, tri_s, kt_s, wt_s, cnt_s):
    i = pl.program_id(0)
    g = pl.program_id(1)
    hf = pl.program_id(2)
    tm = x_ref.shape[0]
    nsub = tm // MOE_SUB
    subs = [slice(s * MOE_SUB, (s + 1) * MOE_SUB) for s in range(nsub)]
    steps = N_EXPERTS // MOE_HALF

    @pl.when((i == 0) & (g == 0) & (hf == 0))
    def _():
        r = lax.broadcasted_iota(jnp.int32, (MOE_SUB, MOE_SUB), 0)
        c = lax.broadcasted_iota(jnp.int32, (MOE_SUB, MOE_SUB), 1)
        tri_s[...] = jnp.where(r < c, 1.0, 0.0).astype(BF16)

    @pl.when((g == 0) & (hf == 0))
    def _():
        comb = comb_ref[...]
        lane = lax.broadcasted_iota(jnp.int32, comb.shape, 1)
        gcol = comb[:, COMB_GROUP_LANE:COMB_GROUP_LANE + 1]
        local = jnp.where(lane < EXPERTS_PER_GROUP, comb, 0.0)
        w_tok = jnp.where(gcol == 0.0, local, 0.0)
        for grp in range(1, N_EXPERT_GROUPS):
            w_tok = w_tok + jnp.where(gcol == float(grp), pltpu.roll(local, EXPERTS_PER_GROUP * grp, axis=1), 0.0)
        wt = w_tok.T
        used = wt != 0.0
        wt_s[...] = wt
        most = jnp.zeros((ROUTE_LANES, 1), F32)
        for rows in subs:
            u = jnp.where(used[:, rows], 1.0, 0.0)
            before = jnp.dot(u.astype(BF16), tri_s[...], preferred_element_type=F32)
            kt_s[:, rows] = jnp.where(used[:, rows], before, -1.0)
            most = jnp.maximum(most, jnp.sum(u, axis=-1, keepdims=True))
        for q in range(steps):
            cnt_s[q] = jnp.max(most[MOE_HALF * q:MOE_HALF * (q + 1), :]).astype(jnp.int32)
        o_ref[...] = jnp.zeros_like(o_ref)

    second = hf == 1
    group_rows = pl.ds(pl.multiple_of(EXPERTS_PER_GROUP * g, EXPERTS_PER_GROUP), EXPERTS_PER_GROUP)
    kt8 = kt_s[group_rows, :]
    wt8 = wt_s[group_rows, :]
    pick = lambda a, j: jnp.where(second, a[MOE_HALF + j:MOE_HALF + j + 1, :], a[j:j + 1, :])
    keys = [pick(kt8, j) for j in range(MOE_HALF)]
    wrow = [pick(wt8, j) for j in range(MOE_HALF)]
    tn = lambda a, b: lax.dot_general(a, b, (((0,), (0,)), ((), ())), preferred_element_type=F32)

    def run_pass(base, nrows):
        ridx = lax.broadcasted_iota(jnp.int32, (nrows, MOE_SUB), 0).astype(F32)
        hit = [[keys[j][:, rows] - base == ridx for rows in subs] for j in range(MOE_HALF)]
        pts = [[jnp.where(m, 1.0, 0.0).astype(BF16) for m in row] for row in hit]
        pcat = [jnp.concatenate([pts[j][s] for j in range(MOE_HALF)], axis=0) for s in range(nsub)]
        xall = [jnp.dot(p, x_ref[rows, :], preferred_element_type=F32).astype(BF16)
                for p, rows in zip(pcat, subs)]
        ys = []
        for j in range(MOE_HALF):
            part = slice(j * nrows, (j + 1) * nrows)
            xg = jnp.concatenate([xa[part] for xa in xall], axis=0)
            cg = jnp.concatenate([jnp.sum(jnp.where(m, wrow[j][:, rows], 0.0), axis=-1, keepdims=True)
                                  for m, rows in zip(hit[j], subs)], axis=0)
            act = (_silu(jnp.dot(xg, wg_ref[0, j], preferred_element_type=F32))
                   * jnp.dot(xg, wu_ref[0, j], preferred_element_type=F32) * cg)
            ys.append(_mm(act, wd_ref[0, j]))
        for s, rows in enumerate(subs):
            part = slice(s * nrows, (s + 1) * nrows)
            ycat = jnp.concatenate([ys[j][part] for j in range(MOE_HALF)], axis=0)
            o_ref[rows, :] += tn(pcat[s], ycat.astype(BF16))

    count = cnt_s[2 * g + hf]

    @pl.when(count > 0)
    def _():
        run_pass(jnp.float32(0.0), MOE_ROWS)

    def extra(p, carry):
        run_pass((MOE_ROWS + p * MOE_EXTRA).astype(F32), MOE_EXTRA)
        return carry

    lax.fori_loop(0, (jnp.maximum(count - MOE_ROWS, 0) + MOE_EXTRA - 1) // MOE_EXTRA, extra, 0)


def _moe(xn, comb, wg, wu, wd, layer, tm=2048):
    t, d = xn.shape
    _, ng, eg, _, de = wg.shape
    row = lambda w: pl.BlockSpec((tm, w), lambda i, g, hf: (i, 0))
    wspec = lambda a, b: pl.BlockSpec((None, 1, MOE_HALF, a, b), lambda i, g, hf: (layer, g, hf, 0, 0))
    return pl.pallas_call(
        _moe_body,
        out_shape=jax.ShapeDtypeStruct((t, d), F32),
        grid=(t // tm, ng, eg // MOE_HALF),
        in_specs=[row(d), row(ROUTE_LANES), wspec(d, de), wspec(d, de), wspec(de, d)],
        out_specs=row(d),
        scratch_shapes=[pltpu.VMEM((MOE_SUB, MOE_SUB), BF16), pltpu.VMEM((ROUTE_LANES, tm), F32),
                        pltpu.VMEM((ROUTE_LANES, tm), F32), pltpu.SMEM((N_EXPERTS // MOE_HALF,), jnp.int32)],
        compiler_params=_params(("arbitrary", "arbitrary", "arbitrary")),
        name="moe",
    )(xn, comb, wg, wu, wd)


def _norm_body(h_ref, dl_ref, w_ref, o_ref):
    o_ref[...] = _rms(h_ref[...] + dl_ref[...], w_ref[...])


def _final_norm(h, delta, w, tm=1024):
    t, d = h.shape
    row = pl.BlockSpec((tm, d), lambda i: (i, 0))
    return pl.pallas_call(
        _norm_body,
        out_shape=jax.ShapeDtypeStruct((t, d), F32),
        grid=(t // tm,),
        in_specs=[row, row, _const((1, d))],
        out_specs=row,
        compiler_params=_params(("parallel",)),
        name="final_norm",
    )(h, delta, w)


def _mask_consts():
    i256 = jnp.arange(GW)
    same_head = (i256[:, None] // HEAD_DIM) == (i256[None, :] // HEAD_DIM)
    t64 = jnp.arange(RW_CHUNK)
    s_w = i256 % RW_CHUNK
    h128 = jnp.arange(128)
    return {
        "bd": same_head.astype(F32),
        "hm": same_head.astype(F32) / HEAD_DIM,
        "hones": same_head.astype(BF16),
        "tri64": (t64[:, None] >= t64[None, :]).astype(F32),
        "lowi": (t64[:, None] >= s_w[None, :]).astype(F32),
        "lows": (t64[:, None] > s_w[None, :]).astype(F32),
        "eyew": (t64[:, None] == s_w[None, :]).astype(F32),
        "tri16": (jnp.arange(HG_BLOCK)[:, None] >= jnp.arange(HG_BLOCK)[None, :]).astype(F32),
        "tri128": (h128[:, None] >= h128[None, :]).astype(F32),
        "expand": (h128[:, None] == (i256[None, :] // HEAD_DIM)).astype(F32),
        "gsel": ((h128[:, None] // 64) == (i256[None, :] // 128)).astype(F32),
    }


def _s5_params(lam_re, lam_im, log_dt, b_re, b_im, c_re, c_im, d_skip, w_glu, b_glu):
    lr = jnp.minimum(lam_re, -1e-4)
    li = lam_im
    dt = jnp.exp(log_dt)[:, None]
    mag = jnp.exp(lr * dt)
    ar, ai = mag * jnp.cos(li * dt), mag * jnp.sin(li * dt)
    den = lr * lr + li * li
    nr = ar - 1.0
    er, ei = (nr * lr + ai * li) / den, (ai * lr - nr * li) / den
    bbr = er[..., None] * b_re - ei[..., None] * b_im
    bbi = er[..., None] * b_im + ei[..., None] * b_re
    eye = jnp.eye(lam_re.shape[0], dtype=F32)
    pack_b = lambda m: jnp.einsum("gph,gk->ghkp", m, eye).reshape(GW, S5_STATE_W)
    pack_c = lambda m: jnp.einsum("ghp,gk->gpkh", m, eye).reshape(S5_STATE_W, GW)
    return {
        "bbd": jnp.concatenate([pack_b(bbr), pack_b(bbi)], axis=1).astype(BF16),
        "cbd": jnp.concatenate([pack_c(c_re), -pack_c(c_im)], axis=0).astype(BF16),
        "ar": ar.reshape(1, S5_STATE_W), "ai": ai.reshape(1, S5_STATE_W),
        "d": d_skip.reshape(1, GW), "w_glu": w_glu.astype(BF16), "b_glu": b_glu.reshape(1, GW),
    }


def kernel(x, ln1_w, w_in, rw_mu, rw_w0, rw_w2, rw_a0, rw_a2, rw_g2, rw_k_k, rw_k_a, rw_r_k, rw_lnx_w, rw_lnx_b, s5_lam_re, s5_lam_im, s5_log_dt, s5_b_re, s5_b_im, s5_c_re, s5_c_im, s5_d, s5_w_glu, s5_b_glu, m_conv_w, m_conv_b, m_dt_bias, m_a_log, m_d, m_norm_w, hg_lb_logits, hg_norm_w, w_out, ln2_w, moe_w_rg, moe_b_rg, moe_w_re, moe_b_re, moe_w_gate, moe_w_up, moe_w_down, lnf_w):
    bsz, seq, d = x.shape
    depth = w_in.shape[0]
    consts = _mask_consts()
    lbs = jax.nn.softmax(hg_lb_logits.astype(F32), axis=0)
    lbs = jnp.cumsum(lbs, axis=0) - lbs[0:1]
    row = lambda v: v.reshape(1, -1).astype(F32)
    n_dt = N_HEADS
    h = x.reshape(bsz * seq, d)
    delta = None
    wg16, wu16, wd16 = (w.astype(BF16) for w in (moe_w_gate, moe_w_up, moe_w_down))
    for l in range(depth):
        c_s5, c_m, c_hg = 4 * GW, 5 * GW, 8 * GW + n_dt
        w_rw = w_in[l, :, :c_s5].astype(BF16)
        w_s5 = w_in[l, :, c_s5:c_m].astype(BF16)
        w_m = jnp.pad(w_in[l, :, c_m:c_hg].astype(BF16), ((0, 0), (0, LANES - n_dt)))
        w_hg = w_in[l, :, c_hg:].astype(BF16)
        h, xn1 = _prenorm(h, delta, row(ln1_w[l]))
        xn1 = xn1.reshape(bsz, seq, d)
        rw = {"mu": row(rw_mu[l]), "w0": row(rw_w0[l]), "w2": rw_w2[l], "a0": row(rw_a0[l]),
              "a2": rw_a2[l], "g2": rw_g2[l], "k_k": row(rw_k_k[l]), "k_a": row(rw_k_a[l]),
              "r_k": row(rw_r_k[l]), "lnx_w": row(rw_lnx_w[l]), "lnx_b": row(rw_lnx_b[l])}
        y_rw = _rwkv(xn1, w_rw, rw, consts)
        y_s5 = _s5(xn1, w_s5, _s5_params(s5_lam_re[l], s5_lam_im[l], s5_log_dt[l], s5_b_re[l], s5_b_im[l],
                                    s5_c_re[l], s5_c_im[l], s5_d[l], s5_w_glu[l], s5_b_glu[l]))
        pad_h = lambda v: jnp.pad(v.astype(F32), (0, LANES - n_dt)).reshape(1, LANES)
        mp = {"conv_w": m_conv_w[l], "conv_b": row(m_conv_b[l]), "dt_bias": pad_h(m_dt_bias[l]),
              "a_neg": pad_h(-jnp.exp(m_a_log[l].astype(F32))),
              "d_exp": row(jnp.repeat(m_d[l], HEAD_DIM)), "norm_w": row(m_norm_w[l])}
        y_m = _mamba(xn1, w_m, mp, consts)
        y_hg = _hgrn(xn1, w_hg, row(lbs[l]), row(hg_norm_w[l]), consts)
        ys = [y.reshape(bsz * seq, GW) for y in (y_rw, y_s5, y_m, y_hg)]
        n_route = N_EXPERT_GROUPS + N_EXPERTS
        wr = jnp.pad(jnp.concatenate([moe_w_rg[l], moe_w_re[l]], axis=1), ((0, 0), (0, ROUTE_LANES - n_route)))
        wr_hi = wr.astype(BF16)
        wr_lo = (wr - wr_hi.astype(F32)).astype(BF16)
        br = jnp.pad(jnp.concatenate([moe_b_rg[l], moe_b_re[l]]), (0, ROUTE_LANES - n_route)).reshape(1, -1)
        h, xn, comb = _outproj(h, ys, w_out[l].astype(BF16), row(ln2_w[l]),
                               jnp.concatenate([wr_hi, wr_lo], axis=1), br)
        delta = _moe(xn, comb, wg16, wu16, wd16, l)
    return _final_norm(h, delta, row(lnf_w)).reshape(bsz, seq, d)
```

```python
import functools
import math

import jax
import jax.numpy as jnp
from jax import lax
from jax.experimental import pallas as pl
from jax.experimental.pallas import tpu as pltpu

F32 = jnp.float32
BF16 = jnp.bfloat16

NORM_EPS = 1e-6
GW = 256
HEAD_DIM = 64
N_HEADS = GW // HEAD_DIM
LANES = 128
SUBLANES = 8
RW_GN_EPS = 64e-5
HG_F_FLOOR = 1e-20
S5_STATE_W = 1024
M_CHUNK = 128
N_EXPERTS = 32
EXPERTS_PER_GROUP = 8
N_EXPERT_GROUPS = 4
ROUTE_LANES = LANES
ROUTE_OFF = N_EXPERT_GROUPS

VMEM_LIMIT = 56 * 1024 * 1024


def _mm(a, b):
    return jnp.dot(a.astype(BF16), b.astype(BF16), preferred_element_type=F32)


def _mm_nt(a, b):
    return lax.dot_general(a.astype(BF16), b.astype(BF16), (((1,), (1,)), ((), ())),
                           preferred_element_type=F32)


def _mm_tn(a, b):
    return lax.dot_general(a.astype(BF16), b.astype(BF16), (((0,), (0,)), ((), ())),
                           preferred_element_type=F32)


def _parts(x, n):
    out, rem = [], x
    for i in range(n):
        p = rem.astype(BF16)
        out.append(p)
        if i + 1 < n:
            rem = rem - p.astype(F32)
    return out


def _mm_x(a, b, na, nb):
    pa, pb = _parts(a, na), _parts(b, nb)
    acc = None
    for i in range(na):
        for j in range(nb):
            if i + j < max(na, nb):
                t = jnp.dot(pa[i], pb[j], preferred_element_type=F32)
                acc = t if acc is None else acc + t
    return acc


def _sigmoid(x):
    return 1.0 / (1.0 + jnp.exp(-x))


def _silu(x):
    return x * _sigmoid(x)


def _softplus(x):
    return jnp.maximum(x, 0.0) + jnp.log1p(jnp.exp(-jnp.abs(x)))


def _rms(x, w):
    ms = jnp.mean(x * x, axis=-1, keepdims=True)
    return x * lax.rsqrt(ms + NORM_EPS) * w


def _stack4(x, bd16):
    xb = x.astype(BF16)
    return jnp.concatenate([xb, xb, xb, xb], axis=0) * bd16


def _const(shape):
    return pl.BlockSpec(shape, lambda *_: (0,) * len(shape))


def _params(sem):
    return pltpu.CompilerParams(dimension_semantics=sem, vmem_limit_bytes=VMEM_LIMIT)


def _prenorm_body(*refs, with_delta):
    if with_delta:
        h_ref, dl_ref, lnw_ref, hn_ref, xn_ref = refs
        h = h_ref[...] + dl_ref[...]
        hn_ref[...] = h
    else:
        h_ref, lnw_ref, xn_ref = refs
        h = h_ref[...]
    xn_ref[...] = _rms(h, lnw_ref[...]).astype(BF16)


def _prenorm(h, delta, lnw, tm=1024):
    t, d = h.shape
    row = pl.BlockSpec((tm, d), lambda i: (i, 0))
    xn_shape = jax.ShapeDtypeStruct((t, d), BF16)
    with_delta = delta is not None
    out = pl.pallas_call(
        functools.partial(_prenorm_body, with_delta=with_delta),
        out_shape=(jax.ShapeDtypeStruct((t, d), F32), xn_shape) if with_delta else xn_shape,
        grid=(t // tm,),
        in_specs=([row, row] if with_delta else [row]) + [_const((1, d))],
        out_specs=(row, row) if with_delta else row,
        compiler_params=_params(("parallel",)),
        name="prenorm",
    )(*((h, delta) if with_delta else (h,)), lnw)
    return out if with_delta else (h, out)


RW_CHUNK = 64


def _rwkv_body(x_ref, win_ref, mu_ref, w0_ref, w2_ref, a0_ref, a2_ref, g2_ref, kk_ref, ka_ref, rk_ref,
               lnw_ref, lnb_ref, bd_ref, bd16_ref, hm_ref, tri_ref, lowi_ref, lows_ref, eye_ref,
               o_ref, carry_ref, st_ref, r_s, k_s, v_s, kn_s, kb_s, wl_s, y_s):
    nb, tt, d = x_ref.shape

    @pl.when(pl.program_id(1) == 0)
    def _():
        carry_ref[...] = jnp.zeros_like(carry_ref)
        st_ref[...] = jnp.zeros_like(st_ref)

    proj = jnp.dot(x_ref[...].reshape(nb * tt, d), win_ref[...], preferred_element_type=F32)
    rows = lax.broadcasted_iota(jnp.int32, (tt, 4 * GW), 0)
    mixed = []
    for b in range(nb):
        p = proj[b * tt:(b + 1) * tt]
        prev = jnp.where(rows == 0, carry_ref[b, 0:1, :], pltpu.roll(p, 1, axis=0))
        carry_ref[b, 0:1, :] = p[tt - 1:tt, :]
        mixed.append(p + (prev - p) * mu_ref[...])
    p = jnp.concatenate(mixed, axis=0)
    r = p[:, 0:GW]
    k = p[:, GW:2 * GW]
    v = p[:, 2 * GW:3 * GW]
    wl = p[:, 3 * GW:3 * GW + 64]
    al = p[:, 3 * GW + 64:3 * GW + 128]
    gl = p[:, 3 * GW + 128:]
    hm = hm_ref[...]
    w_log = -math.exp(-0.5) * _sigmoid(w0_ref[...] + _mm_x(jnp.tanh(wl), w2_ref[...], 2, 2))
    a = _sigmoid(a0_ref[...] + _mm(al, a2_ref[...]))
    gate = _mm(_sigmoid(gl), g2_ref[...])
    kn = k * kk_ref[...]
    kn = kn * lax.rsqrt(jnp.maximum(_mm_x(kn * kn, hm, 2, 1) * float(HEAD_DIM), 1e-24))
    k = k * (1.0 + (a - 1.0) * ka_ref[...])
    r_s[...] = r
    k_s[...] = k
    v_s[...] = v
    kn_s[...] = kn
    kb_s[...] = kn * a
    wl_s[...] = w_log

    def chunk(c, carry):
        seqs = range(nb)
        sls = [pl.ds(pl.multiple_of(b * tt + c * RW_CHUNK, RW_CHUNK), RW_CHUNK) for b in seqs]
        ld = lambda s: [s[sl, :] for sl in sls]
        r_c, k_c, v_c, kn_c, kb_c, wl_c = (ld(s) for s in (r_s, k_s, v_s, kn_s, kb_s, wl_s))
        bd = bd_ref[...]
        bd16 = bd16_ref[...]
        lows = lows_ref[...]
        lowi = lowi_ref[...]
        st4 = lambda xs: [_stack4(x, bd16) for x in xs]
        gc = [_mm_x(tri_ref[...], w, 1, 3) for w in wl_c]
        g_end = [g[RW_CHUNK - 1:RW_CHUNK, :] for g in gc]
        inv = [jnp.exp(-g) for g in gc]
        lhs = [jnp.concatenate([-kn * jnp.exp(g - w), r * jnp.exp(g)], axis=0)
               for kn, r, g, w in zip(kn_c, r_c, gc, wl_c)]
        rhs = [jnp.concatenate([_stack4(kb * i, bd16), _stack4(k * i, bd16)], axis=0)
               for kb, k, i in zip(kb_c, k_c, inv)]
        gram = [_mm_nt(a, b) for a, b in zip(lhs, rhs)]
        state = [st_ref[b] for b in seqs]
        from_state = [_mm_nt(a, s) for a, s in zip(lhs, state)]
        a_ab = [g[:RW_CHUNK, :GW] * lows for g in gram]
        pw = [eye_ref[...] + a for a in a_ab]
        q = [_mm(a, b) for a, b in zip(a_ab, st4(a_ab))]
        for step in range(4):
            both = [_mm(jnp.concatenate([p, a], axis=0), b) for p, a, b in zip(pw, q, st4(q))]
            pw = [p + pq[:RW_CHUNK] for p, pq in zip(pw, both)]
            q = [pq[RW_CHUNK:] for pq in both]
        pw = [p + _mm(p, b) for p, b in zip(pw, st4(q))]
        vst = st4(v_c)
        mask_k = jnp.concatenate([lows, lowi], axis=0)
        from_v = [_mm(g[:, GW:] * mask_k, vs) for g, vs in zip(gram, vst)]
        rhs_u = [f[:RW_CHUNK] + fv[:RW_CHUNK] for f, fv in zip(from_state, from_v)]
        u = [_mm(p, x) for p, x in zip(pw, st4(rhs_u))]
        ust = st4(u)
        for b in seqs:
            y_s[sls[b], :] = (from_state[b][RW_CHUNK:] + _mm(gram[b][RW_CHUNK:, :GW] * lowi, ust[b])
                              + from_v[b][RW_CHUNK:])
        for b in seqs:
            to_end = jnp.exp(g_end[b] - gc[b])
            upd = _mm_tn(jnp.concatenate([u[b], v_c[b]], axis=0),
                         jnp.concatenate([kb_c[b] * to_end, k_c[b] * to_end], axis=0))
            st_ref[b] = state[b] * jnp.exp(g_end[b]) + upd * bd
        return carry

    lax.fori_loop(0, tt // RW_CHUNK, chunk, 0)

    y = y_s[...]
    mean = _mm_x(y, hm, 2, 1)
    d = y - mean
    var = _mm_x(d * d, hm, 2, 1)
    yn = d * lax.rsqrt(var + RW_GN_EPS) * lnw_ref[...] + lnb_ref[...]
    bonus = _mm_x(r * k * rk_ref[...], hm, 2, 1) * float(HEAD_DIM) * v
    o_ref[...] = ((yn + bonus) * gate).astype(BF16).reshape(nb, tt, GW)


def _rwkv(xn3, w_in, prm, consts, nb=8, tt=128):
    b, l, d = xn3.shape
    vec = _const((1, GW))
    scr = pltpu.VMEM((nb * tt, GW), F32)
    return pl.pallas_call(
        _rwkv_body,
        out_shape=jax.ShapeDtypeStruct((b, l, GW), BF16),
        grid=(b // nb, l // tt),
        in_specs=[pl.BlockSpec((nb, tt, d), lambda i, j: (i, j, 0)), _const((d, 4 * GW)),
                  _const((1, 4 * GW)), vec, _const((64, GW)), vec, _const((64, GW)),
                  _const((128, GW)), vec, vec, vec, vec, vec,
                  _const((GW, GW)), _const((GW, GW)), _const((GW, GW)), _const((RW_CHUNK, RW_CHUNK)),
                  _const((RW_CHUNK, GW)), _const((RW_CHUNK, GW)), _const((RW_CHUNK, GW))],
        out_specs=pl.BlockSpec((nb, tt, GW), lambda i, j: (i, j, 0)),
        scratch_shapes=[pltpu.VMEM((nb, SUBLANES, 4 * GW), F32), pltpu.VMEM((nb, GW, GW), F32)] + [scr] * 7,
        compiler_params=_params(("parallel", "arbitrary")),
        name="rwkv7",
    )(xn3, w_in, prm["mu"], prm["w0"], prm["w2"], prm["a0"], prm["a2"], prm["g2"], prm["k_k"],
      prm["k_a"], prm["r_k"], prm["lnx_w"], prm["lnx_b"],
      consts["bd"], consts["hones"], consts["hm"], consts["tri64"], consts["lowi"], consts["lows"], consts["eyew"])


def _s5_body(x_ref, win_ref, bbd_ref, ar_ref, ai_ref, cbd_ref, d_ref, wglu_ref, bglu_ref, o_ref,
             ub_s, ut_s, xr_s, xi_s, ot_s, st_ref):
    nb, tt, d = x_ref.shape
    w = win_ref.shape[1]
    ncb = w // LANES
    assert nb == SUBLANES

    @pl.when(pl.program_id(0) == 0)
    def _():
        st_ref[...] = jnp.zeros_like(st_ref)

    u_bt = jnp.dot(x_ref[...].reshape(nb * tt, d), win_ref[...], preferred_element_type=F32)
    for cb in range(ncb):
        ub_s[cb] = u_bt[:, cb * LANES:(cb + 1) * LANES]

    def regroup(t, carry):
        for cb in range(ncb):
            ut_s[cb, pl.ds(pl.multiple_of(t * nb, nb), nb), :] = ub_s[cb, pl.ds(t, nb, stride=tt), :]
        return carry

    lax.fori_loop(0, tt, regroup, 0, unroll=8)
    u = jnp.concatenate([ut_s[cb] for cb in range(ncb)], axis=-1)
    bu = _mm(u, bbd_ref[...])
    xr_s[...] = bu[:, :S5_STATE_W]
    xi_s[...] = bu[:, S5_STATE_W:]
    ar = jnp.broadcast_to(ar_ref[...], (nb, S5_STATE_W))
    ai = jnp.broadcast_to(ai_ref[...], (nb, S5_STATE_W))

    def step(t, carry):
        xr, xi = carry
        rows = pl.ds(pl.multiple_of(t * nb, nb), nb)
        nr = ar * xr - ai * xi + xr_s[rows, :]
        ni = ar * xi + ai * xr + xi_s[rows, :]
        xr_s[rows, :] = nr
        xi_s[rows, :] = ni
        return nr, ni

    xr, xi = lax.fori_loop(0, tt, step, (st_ref[0], st_ref[1]), unroll=4)
    st_ref[0] = xr
    st_ref[1] = xi
    cbd = cbd_ref[...]
    y = _mm(xr_s[...], cbd[:S5_STATE_W]) + _mm(xi_s[...], cbd[S5_STATE_W:]) + d_ref[...] * u
    y = 0.5 * y * (1.0 + jnp.tanh(math.sqrt(2.0 / math.pi) * (y + 0.044715 * (y * y * y))))
    z = _mm(y, wglu_ref[...]) + bglu_ref[...]
    out = y * _sigmoid(z)
    for cb in range(ncb):
        ot_s[cb] = out[:, cb * LANES:(cb + 1) * LANES]
    for b in range(nb):
        for cb in range(ncb):
            o_ref[b, :, cb * LANES:(cb + 1) * LANES] = ot_s[cb, pl.ds(b, tt, stride=nb), :].astype(BF16)


def _s5(xn3, w_in, prm, tt=128):
    b, l, d = xn3.shape
    slab = pltpu.VMEM((GW // LANES, b * tt, LANES), F32)
    wide = pltpu.VMEM((b * tt, S5_STATE_W), F32)
    return pl.pallas_call(
        _s5_body,
        out_shape=jax.ShapeDtypeStruct((b, l, GW), BF16),
        grid=(l // tt,),
        in_specs=[pl.BlockSpec((b, tt, d), lambda j: (0, j, 0)), _const((d, GW)),
                  _const((GW, 2 * S5_STATE_W)), _const((1, S5_STATE_W)), _const((1, S5_STATE_W)),
                  _const((2 * S5_STATE_W, GW)), _const((1, GW)), _const((GW, GW)), _const((1, GW))],
        out_specs=pl.BlockSpec((b, tt, GW), lambda j: (0, j, 0)),
        scratch_shapes=[slab, slab, wide, wide, slab, pltpu.VMEM((2, b, S5_STATE_W), F32)],
        compiler_params=_params(("arbitrary",)),
        name="s5",
    )(xn3, w_in, prm["bbd"], prm["ar"], prm["ai"], prm["cbd"], prm["d"], prm["w_glu"], prm["b_glu"])


def _mamba_body(x_ref, win_ref, cw_ref, cb_ref, dtb_ref, aneg_ref, dexp_ref, nw_ref,
                tri_ref, exp_ref, gsel_ref, o_ref, carry_ref, st_ref):
    nb, tt, d = x_ref.shape
    seqs = range(nb)

    @pl.when(pl.program_id(1) == 0)
    def _():
        carry_ref[...] = jnp.zeros_like(carry_ref)
        st_ref[...] = jnp.zeros_like(st_ref)

    proj = jnp.dot(x_ref[...].reshape(nb * tt, d), win_ref[...], preferred_element_type=F32)
    cw = 2 * GW
    rows = lax.broadcasted_iota(jnp.int32, (tt, cw), 0)
    pad = jnp.zeros((tt - SUBLANES, cw), F32)
    xcs = []
    for b in seqs:
        xbc = proj[b * tt:(b + 1) * tt, GW:GW + cw]
        tail = carry_ref[b]
        conv = xbc * cw_ref[3:4, :]
        for s in (1, 2, 3):
            head = jnp.concatenate([pltpu.roll(tail, s, axis=0), pad], axis=0)
            shifted = jnp.where(rows < s, head, pltpu.roll(xbc, s, axis=0))
            conv = conv + shifted * cw_ref[3 - s:4 - s, :]
        carry_ref[b] = xbc[tt - SUBLANES:, :]
        xcs.append(_silu(conv + cb_ref[...]))
    xs = [xc[:, :GW] for xc in xcs]
    bm = [xc[:, GW:GW + 128] for xc in xcs]
    cm = [xc[:, GW + 128:] for xc in xcs]
    expand = exp_ref[...]
    dt = [_softplus(proj[b * tt:(b + 1) * tt, GW + cw:] + dtb_ref[...]) for b in seqs]
    cs = [_mm_x(tri_ref[...], x * aneg_ref[...], 1, 3) for x in dt]
    cs_t = [c.T for c in cs]
    cs_end = [c[tt - 1:tt, :] for c in cs]
    wide = [_mm_x(jnp.concatenate([x, c, ce - c, jnp.broadcast_to(ce, (SUBLANES, 128))], axis=0), expand, 3, 1)
            for x, c, ce in zip(dt, cs, cs_end)]
    xdt = [x * w[:tt] for x, w in zip(xs, wide)]
    lane = lax.broadcasted_iota(jnp.int32, (tt, 128), 1)
    tril = lax.broadcasted_iota(jnp.int32, (tt, tt), 0) >= lax.broadcasted_iota(jnp.int32, (tt, tt), 1)
    lane_w = lax.broadcasted_iota(jnp.int32, (tt, GW), 1)
    state = [st_ref[b] for b in seqs]
    y = [_mm(c, s) * jnp.exp(w[tt:2 * tt]) + dexp_ref[...] * x for c, s, w, x in zip(cm, state, wide, xs)]
    for g in range(2):
        cbm = [_mm_nt(jnp.where(lane // 64 == g, c, 0.0), b_) for c, b_ in zip(cm, bm)]
        for h in (2 * g, 2 * g + 1):
            decay = [jnp.where(tril, jnp.exp(jnp.minimum(c[:, h:h + 1] - ct[h:h + 1, :], 0.0)), 0.0)
                     for c, ct in zip(cs, cs_t)]
            y = [yy + jnp.where(lane_w // HEAD_DIM == h, _mm(m * dc, xd), 0.0)
                 for yy, m, dc, xd in zip(y, cbm, decay, xdt)]
    upd = [_mm_tn(b_, xd * jnp.exp(w[2 * tt:3 * tt])) for b_, xd, w in zip(bm, xdt, wide)]
    nw = nw_ref[...]
    half = GW // 2
    for b in seqs:
        st_ref[b] = state[b] * jnp.exp(wide[b][3 * tt:3 * tt + 1]) + upd[b] * gsel_ref[...]
        yb = y[b] * _silu(proj[b * tt:(b + 1) * tt, :GW])
        o_ref[b] = jnp.concatenate([_rms(yb[:, :half], nw[:, :half]), _rms(yb[:, half:], nw[:, half:])],
                                   axis=-1).astype(BF16)


def _mamba(xn3, w_in, prm, consts, nb=4):
    b, l, d = xn3.shape
    tt = M_CHUNK
    wcols = w_in.shape[1]
    return pl.pallas_call(
        _mamba_body,
        out_shape=jax.ShapeDtypeStruct((b, l, GW), BF16),
        grid=(b // nb, l // tt),
        in_specs=[pl.BlockSpec((nb, tt, d), lambda i, j: (i, j, 0)), _const((d, wcols)),
                  _const((4, 2 * GW)), _const((1, 2 * GW)), _const((1, 128)), _const((1, 128)),
                  _const((1, GW)), _const((1, GW)),
                  _const((tt, tt)), _const((128, GW)), _const((128, GW))],
        out_specs=pl.BlockSpec((nb, tt, GW), lambda i, j: (i, j, 0)),
        scratch_shapes=[pltpu.VMEM((nb, SUBLANES, 2 * GW), F32), pltpu.VMEM((nb, 128, GW), F32)],
        compiler_params=_params(("parallel", "arbitrary")),
        name="mamba2",
    )(xn3, w_in, prm["conv_w"], prm["conv_b"], prm["dt_bias"], prm["a_neg"], prm["d_exp"],
      prm["norm_w"], consts["tri128"], consts["expand"], consts["gsel"])


HG_BLOCK = 16


def _hgrn_body(x_ref, win_ref, lb_ref, nw_ref, bd_ref, hm_ref, hones_ref, tri_ref, o_ref,
               st_ref, q_s, k_s, v_s, lf_s, o_s):
    nb, tt, d = x_ref.shape

    @pl.when(pl.program_id(1) == 0)
    def _():
        st_ref[...] = jnp.zeros_like(st_ref)

    p = jnp.dot(x_ref[...].reshape(nb * tt, d), win_ref[...], preferred_element_type=F32)
    lb = lb_ref[...]
    hf = p[:, GW:2 * GW]
    q_s[...] = _silu(p[:, :GW])
    k_s[...] = (1.0 - lb) * _sigmoid(-hf)
    v_s[...] = p[:, 2 * GW:3 * GW]
    lf_s[...] = jnp.log(jnp.maximum(lb + (1.0 - lb) * _sigmoid(hf), HG_F_FLOOR))

    def block(n, carry):
        seqs = range(nb)
        sls = [pl.ds(pl.multiple_of(b * tt + n * HG_BLOCK, HG_BLOCK), HG_BLOCK) for b in seqs]
        ld = lambda s: [s[sl, :] for sl in sls]
        q_b, k_b, v_b, lf_b = (ld(s) for s in (q_s, k_s, v_s, lf_s))
        g = [_mm_x(tri_ref[...], lf, 1, 3) for lf in lf_b]
        g_end = [x[HG_BLOCK - 1:HG_BLOCK, :] for x in g]
        state = [st_ref[b] for b in seqs]
        o = [_mm_nt(q * jnp.exp(x), s) for q, x, s in zip(q_b, g, state)]
        half = HG_BLOCK // 2
        lower = lambda a: a.reshape(2, half, GW)[1]
        prods, att = [], []
        for q, k, x in zip(q_b, k_b, g):
            ql, xl = lower(q), lower(x)
            full = [q * jnp.exp(jnp.minimum(x - x[j:j + 1, :], 0.0)) * k[j:j + 1, :] for j in range(half)]
            low = [ql * jnp.exp(jnp.minimum(xl - x[j:j + 1, :], 0.0)) * k[j:j + 1, :] for j in range(half, HG_BLOCK)]
            prods.append(jnp.concatenate(full + low, axis=0))
        att = [_mm(p, hones_ref[...]) for p in prods]
        upd = [_mm_tn(v, k * jnp.exp(ge - x)) for v, k, ge, x in zip(v_b, k_b, g_end, g)]
        row = lax.broadcasted_iota(jnp.int32, (HG_BLOCK, GW), 0)
        rowl = lax.broadcasted_iota(jnp.int32, (half, GW), 0) + half
        for b in seqs:
            a_full = att[b][:HG_BLOCK * half].reshape(half, HG_BLOCK, GW)
            a_low = att[b][HG_BLOCK * half:].reshape(half, half, GW)
            ob = o[b]
            for j in range(half):
                ob = ob + jnp.where(row >= j, a_full[j], 0.0) * v_b[b][j:j + 1, :]
            add = jnp.zeros((half, GW), F32)
            for j in range(half, HG_BLOCK):
                add = add + jnp.where(rowl >= j, a_low[j - half], 0.0) * v_b[b][j:j + 1, :]
            o_s[sls[b], :] = ob + jnp.concatenate([jnp.zeros((half, GW), F32), add], axis=0)
            st_ref[b] = state[b] * jnp.exp(g_end[b]) + upd[b] * bd_ref[...]
        return carry

    lax.fori_loop(0, tt // HG_BLOCK, block, 0)
    o = o_s[...]
    ms = _mm_x(o * o, hm_ref[...], 2, 1)
    o_ref[...] = (o * lax.rsqrt(ms + NORM_EPS) * nw_ref[...] * _silu(p[:, 3 * GW:])).astype(BF16).reshape(nb, tt, GW)


def _hgrn(xn3, w_in, lb, nw, consts, nb=8, tt=128):
    b, l, d = xn3.shape
    return pl.pallas_call(
        _hgrn_body,
        out_shape=jax.ShapeDtypeStruct((b, l, GW), BF16),
        grid=(b // nb, l // tt),
        in_specs=[pl.BlockSpec((nb, tt, d), lambda i, j: (i, j, 0)), _const((d, 4 * GW)),
                  _const((1, GW)), _const((1, GW)), _const((GW, GW)), _const((GW, GW)),
                  _const((GW, GW)), _const((HG_BLOCK, HG_BLOCK))],
        out_specs=pl.BlockSpec((nb, tt, GW), lambda i, j: (i, j, 0)),
        scratch_shapes=[pltpu.VMEM((nb, GW, GW), F32)] + [pltpu.VMEM((nb * tt, GW), F32)] * 5,
        compiler_params=_params(("parallel", "arbitrary")),
        name="hgrn2",
    )(xn3, w_in, lb, nw, consts["bd"], consts["hm"], consts["hones"], consts["tri16"])


def _outproj_body(h_ref, y1_ref, y2_ref, y3_ref, y4_ref, wo_ref, ln2_ref, wr_ref, br_ref,
                  hn_ref, xn_ref, comb_ref):
    y = jnp.concatenate([y1_ref[...], y2_ref[...], y3_ref[...], y4_ref[...]], axis=-1)
    h = h_ref[...] + _mm(y, wo_ref[...])
    hn_ref[...] = h
    xn = _rms(h, ln2_ref[...])
    xn_ref[...] = xn.astype(BF16)
    xh, xl = _parts(xn, 2)
    wr = wr_ref[...]
    first = jnp.dot(xh, wr, preferred_element_type=F32)
    logits = (first[:, :ROUTE_LANES] + first[:, ROUTE_LANES:]
              + jnp.dot(xl, wr[:, :ROUTE_LANES], preferred_element_type=F32) + br_ref[...])
    lane = lax.broadcasted_iota(jnp.int32, logits.shape, 1)
    neg = -jnp.inf
    big = ROUTE_LANES
    glog = jnp.where(lane < N_EXPERT_GROUPS, logits, neg)
    gmax = jnp.max(glog, axis=-1, keepdims=True)
    g_w = 1.0 / jnp.sum(jnp.exp(glog - gmax), axis=-1, keepdims=True)
    g_idx = jnp.min(jnp.where(glog == gmax, lane, big), axis=-1, keepdims=True)
    lo = ROUTE_OFF + EXPERTS_PER_GROUP * g_idx
    elog = jnp.where((lane >= lo) & (lane < lo + EXPERTS_PER_GROUP), logits, neg)
    m1 = jnp.max(elog, axis=-1, keepdims=True)
    i1 = jnp.min(jnp.where(elog == m1, lane, big), axis=-1, keepdims=True)
    elog2 = jnp.where(lane == i1, neg, elog)
    m2 = jnp.max(elog2, axis=-1, keepdims=True)
    i2 = jnp.min(jnp.where(elog2 == m2, lane, big), axis=-1, keepdims=True)
    e2 = jnp.exp(m2 - m1)
    w1 = 1.0 / (1.0 + e2)
    w2 = e2 / (1.0 + e2)
    comb_ref[...] = (g_w * (jnp.where(lane == i1 - lo, w1, 0.0) + jnp.where(lane == i2 - lo, w2, 0.0))
                     + jnp.where(lane == EXPERTS_PER_GROUP, g_idx.astype(F32), 0.0))


def _outproj(h, ys, wo, ln2, wr, br, tm=1024):
    t, d = h.shape
    row = lambda w: pl.BlockSpec((tm, w), lambda i: (i, 0))
    return pl.pallas_call(
        _outproj_body,
        out_shape=(jax.ShapeDtypeStruct((t, d), F32), jax.ShapeDtypeStruct((t, d), BF16),
                   jax.ShapeDtypeStruct((t, ROUTE_LANES), F32)),
        grid=(t // tm,),
        in_specs=[row(d), row(GW), row(GW), row(GW), row(GW), _const((d, d)), _const((1, d)),
                  _const((d, 2 * ROUTE_LANES)), _const((1, ROUTE_LANES))],
        out_specs=(row(d), row(d), row(ROUTE_LANES)),
        compiler_params=_params(("parallel",)),
        name="outproj_router",
    )(h, *ys, wo, ln2, wr, br)


MOE_SUB = 512
MOE_ROWS = 64
MOE_EXTRA = 16
MOE_HALF = EXPERTS_PER_GROUP // 2
COMB_GROUP_LANE = EXPERTS_PER_GROUP


def _moe_body(x_ref, comb_ref, wg_ref, wu_ref, wd_ref, o_ref, tri_s, kt_s, wt_s, cnt_s):
    i = pl.program_id(0)
    g = pl.program_id(1)
    hf = pl.program_id(2)
    tm = x_ref.shape[0]
    nsub = tm // MOE_SUB
    subs = [slice(s * MOE_SUB, (s + 1) * MOE_SUB) for s in range(nsub)]
    steps = N_EXPERTS // MOE_HALF

    @pl.when((i == 0) & (g == 0) & (hf == 0))
    def _():
        r = lax.broadcasted_iota(jnp.int32, (MOE_SUB, MOE_SUB), 0)
        c = lax.broadcasted_iota(jnp.int32, (MOE_SUB, MOE_SUB), 1)
        tri_s[...] = jnp.where(r < c, 1.0, 0.0).astype(BF16)

    @pl.when((g == 0) & (hf == 0))
    def _():
        comb = comb_ref[...]
        lane = lax.broadcasted_iota(jnp.int32, comb.shape, 1)
        gcol = comb[:, COMB_GROUP_LANE:COMB_GROUP_LANE + 1]
        local = jnp.where(lane < EXPERTS_PER_GROUP, comb, 0.0)
        w_tok = jnp.where(gcol == 0.0, local, 0.0)
        for grp in range(1, N_EXPERT_GROUPS):
            w_tok = w_tok + jnp.where(gcol == float(grp), pltpu.roll(local, EXPERTS_PER_GROUP * grp, axis=1), 0.0)
        wt = w_tok.T
        used = wt != 0.0
        wt_s[...] = wt
        most = jnp.zeros((ROUTE_LANES, 1), F32)
        for rows in subs:
            u = jnp.where(used[:, rows], 1.0, 0.0)
            before = jnp.dot(u.astype(BF16), tri_s[...], preferred_element_type=F32)
            kt_s[:, rows] = jnp.where(used[:, rows], before, -1.0)
            most = jnp.maximum(most, jnp.sum(u, axis=-1, keepdims=True))
        for q in range(steps):
            cnt_s[q] = jnp.max(most[MOE_HALF * q:MOE_HALF * (q + 1), :]).astype(jnp.int32)
        o_ref[...] = jnp.zeros_like(o_ref)

    second = hf == 1
    group_rows = pl.ds(pl.multiple_of(EXPERTS_PER_GROUP * g, EXPERTS_PER_GROUP), EXPERTS_PER_GROUP)
    kt8 = kt_s[group_rows, :]
    wt8 = wt_s[group_rows, :]
    pick = lambda a, j: jnp.where(second, a[MOE_HALF + j:MOE_HALF + j + 1, :], a[j:j + 1, :])
    keys = [pick(kt8, j) for j in range(MOE_HALF)]
    wrow = [pick(wt8, j) for j in range(MOE_HALF)]
    tn = lambda a, b: lax.dot_general(a, b, (((0,), (0,)), ((), ())), preferred_element_type=F32)

    def run_pass(base, nrows):
        ridx = lax.broadcasted_iota(jnp.int32, (nrows, MOE_SUB), 0).astype(F32)
        hit = [[keys[j][:, rows] - base == ridx for rows in subs] for j in range(MOE_HALF)]
        pts = [[jnp.where(m, 1.0, 0.0).astype(BF16) for m in row] for row in hit]
        pcat = [jnp.concatenate([pts[j][s] for j in range(MOE_HALF)], axis=0) for s in range(nsub)]
        xall = [jnp.dot(p, x_ref[rows, :], preferred_element_type=F32).astype(BF16)
                for p, rows in zip(pcat, subs)]
        ys = []
        for j in range(MOE_HALF):
            part = slice(j * nrows, (j + 1) * nrows)
            xg = jnp.concatenate([xa[part] for xa in xall], axis=0)
            cg = jnp.concatenate([jnp.sum(jnp.where(m, wrow[j][:, rows], 0.0), axis=-1, keepdims=True)
                                  for m, rows in zip(hit[j], subs)], axis=0)
            act = (_silu(jnp.dot(xg, wg_ref[0, j], preferred_element_type=F32))
                   * jnp.dot(xg, wu_ref[0, j], preferred_element_type=F32) * cg)
            ys.append(_mm(act, wd_ref[0, j]))
        for s, rows in enumerate(subs):
            part = slice(s * nrows, (s + 1) * nrows)
            ycat = jnp.concatenate([ys[j][part] for j in range(MOE_HALF)], axis=0)
            o_ref[rows, :] += tn(pcat[s], ycat.astype(BF16))

    count = cnt_s[2 * g + hf]

    @pl.when(count > 0)
    def _():
        run_pass(jnp.float32(0.0), MOE_ROWS)

    def extra(p, carry):
        run_pass((MOE_ROWS + p * MOE_EXTRA).astype(F32), MOE_EXTRA)
        return carry

    lax.fori_loop(0, (jnp.maximum(count - MOE_ROWS, 0) + MOE_EXTRA - 1) // MOE_EXTRA, extra, 0)


def _moe(xn, comb, wg, wu, wd, layer, tm=2048):
    t, d = xn.shape
    _, ng, eg, _, de = wg.shape
    row = lambda w: pl.BlockSpec((tm, w), lambda i, g, hf: (i, 0))
    wspec = lambda a, b: pl.BlockSpec((None, 1, MOE_HALF, a, b), lambda i, g, hf: (layer, g, hf, 0, 0))
    return pl.pallas_call(
        _moe_body,
        out_shape=jax.ShapeDtypeStruct((t, d), F32),
        grid=(t // tm, ng, eg // MOE_HALF),
        in_specs=[row(d), row(ROUTE_LANES), wspec(d, de), wspec(d, de), wspec(de, d)],
        out_specs=row(d),
        scratch_shapes=[pltpu.VMEM((MOE_SUB, MOE_SUB), BF16), pltpu.VMEM((ROUTE_LANES, tm), F32),
                        pltpu.VMEM((ROUTE_LANES, tm), F32), pltpu.SMEM((N_EXPERTS // MOE_HALF,), jnp.int32)],
        compiler_params=_params(("arbitrary", "arbitrary", "arbitrary")),
        name="moe",
    )(xn, comb, wg, wu, wd)


def _norm_body(h_ref, dl_ref, w_ref, o_ref):
    o_ref[...] = _rms(h_ref[...] + dl_ref[...], w_ref[...])


def _final_norm(h, delta, w, tm=1024):
    t, d = h.shape
    row = pl.BlockSpec((tm, d), lambda i: (i, 0))
    return pl.pallas_call(
        _norm_body,
        out_shape=jax.ShapeDtypeStruct((t, d), F32),
        grid=(t // tm,),
        in_specs=[row, row, _const((1, d))],
        out_specs=row,
        compiler_params=_params(("parallel",)),
        name="final_norm",
    )(h, delta, w)


def _mask_consts():
    i256 = jnp.arange(GW)
    same_head = (i256[:, None] // HEAD_DIM) == (i256[None, :] // HEAD_DIM)
    t64 = jnp.arange(RW_CHUNK)
    s_w = i256 % RW_CHUNK
    h128 = jnp.arange(128)
    return {
        "bd": same_head.astype(F32),
        "hm": same_head.astype(F32) / HEAD_DIM,
        "hones": same_head.astype(BF16),
        "tri64": (t64[:, None] >= t64[None, :]).astype(F32),
        "lowi": (t64[:, None] >= s_w[None, :]).astype(F32),
        "lows": (t64[:, None] > s_w[None, :]).astype(F32),
        "eyew": (t64[:, None] == s_w[None, :]).astype(F32),
        "tri16": (jnp.arange(HG_BLOCK)[:, None] >= jnp.arange(HG_BLOCK)[None, :]).astype(F32),
        "tri128": (h128[:, None] >= h128[None, :]).astype(F32),
        "expand": (h128[:, None] == (i256[None, :] // HEAD_DIM)).astype(F32),
        "gsel": ((h128[:, None] // 64) == (i256[None, :] // 128)).astype(F32),
    }


def _s5_params(lam_re, lam_im, log_dt, b_re, b_im, c_re, c_im, d_skip, w_glu, b_glu):
    lr = jnp.minimum(lam_re, -1e-4)
    li = lam_im
    dt = jnp.exp(log_dt)[:, None]
    mag = jnp.exp(lr * dt)
    ar, ai = mag * jnp.cos(li * dt), mag * jnp.sin(li * dt)
    den = lr * lr + li * li
    nr = ar - 1.0
    er, ei = (nr * lr + ai * li) / den, (ai * lr - nr * li) / den
    bbr = er[..., None] * b_re - ei[..., None] * b_im
    bbi = er[..., None] * b_im + ei[..., None] * b_re
    eye = jnp.eye(lam_re.shape[0], dtype=F32)
    pack_b = lambda m: jnp.einsum("gph,gk->ghkp", m, eye).reshape(GW, S5_STATE_W)
    pack_c = lambda m: jnp.einsum("ghp,gk->gpkh", m, eye).reshape(S5_STATE_W, GW)
    return {
        "bbd": jnp.concatenate([pack_b(bbr), pack_b(bbi)], axis=1).astype(BF16),
        "cbd": jnp.concatenate([pack_c(c_re), -pack_c(c_im)], axis=0).astype(BF16),
        "ar": ar.reshape(1, S5_STATE_W), "ai": ai.reshape(1, S5_STATE_W),
        "d": d_skip.reshape(1, GW), "w_glu": w_glu.astype(BF16), "b_glu": b_glu.reshape(1, GW),
    }


def kernel(x, ln1_w, w_in, rw_mu, rw_w0, rw_w2, rw_a0, rw_a2, rw_g2, rw_k_k, rw_k_a, rw_r_k, rw_lnx_w, rw_lnx_b, s5_lam_re, s5_lam_im, s5_log_dt, s5_b_re, s5_b_im, s5_c_re, s5_c_im, s5_d, s5_w_glu, s5_b_glu, m_conv_w, m_conv_b, m_dt_bias, m_a_log, m_d, m_norm_w, hg_lb_logits, hg_norm_w, w_out, ln2_w, moe_w_rg, moe_b_rg, moe_w_re, moe_b_re, moe_w_gate, moe_w_up, moe_w_down, lnf_w):
    bsz, seq, d = x.shape
    depth = w_in.shape[0]
    consts = _mask_consts()
    lbs = jax.nn.softmax(hg_lb_logits.astype(F32), axis=0)
    lbs = jnp.cumsum(lbs, axis=0) - lbs[0:1]
    row = lambda v: v.reshape(1, -1).astype(F32)
    n_dt = N_HEADS
    h = x.reshape(bsz * seq, d)
    delta = None
    wg16, wu16, wd16 = (w.astype(BF16) for w in (moe_w_gate, moe_w_up, moe_w_down))
    for l in range(depth):
        c_s5, c_m, c_hg = 4 * GW, 5 * GW, 8 * GW + n_dt
        w_rw = w_in[l, :, :c_s5].astype(BF16)
        w_s5 = w_in[l, :, c_s5:c_m].astype(BF16)
        w_m = jnp.pad(w_in[l, :, c_m:c_hg].astype(BF16), ((0, 0), (0, LANES - n_dt)))
        w_hg = w_in[l, :, c_hg:].astype(BF16)
        h, xn1 = _prenorm(h, delta, row(ln1_w[l]))
        xn1 = xn1.reshape(bsz, seq, d)
        rw = {"mu": row(rw_mu[l]), "w0": row(rw_w0[l]), "w2": rw_w2[l], "a0": row(rw_a0[l]),
              "a2": rw_a2[l], "g2": rw_g2[l], "k_k": row(rw_k_k[l]), "k_a": row(rw_k_a[l]),
              "r_k": row(rw_r_k[l]), "lnx_w": row(rw_lnx_w[l]), "lnx_b": row(rw_lnx_b[l])}
        y_rw = _rwkv(xn1, w_rw, rw, consts)
        y_s5 = _s5(xn1, w_s5, _s5_params(s5_lam_re[l], s5_lam_im[l], s5_log_dt[l], s5_b_re[l], s5_b_im[l],
                                    s5_c_re[l], s5_c_im[l], s5_d[l], s5_w_glu[l], s5_b_glu[l]))
        pad_h = lambda v: jnp.pad(v.astype(F32), (0, LANES - n_dt)).reshape(1, LANES)
        mp = {"conv_w": m_conv_w[l], "conv_b": row(m_conv_b[l]), "dt_bias": pad_h(m_dt_bias[l]),
              "a_neg": pad_h(-jnp.exp(m_a_log[l].astype(F32))),
              "d_exp": row(jnp.repeat(m_d[l], HEAD_DIM)), "norm_w": row(m_norm_w[l])}
        y_m = _mamba(xn1, w_m, mp, consts)
        y_hg = _hgrn(xn1, w_hg, row(lbs[l]), row(hg_norm_w[l]), consts)
        ys = [y.reshape(bsz * seq, GW) for y in (y_rw, y_s5, y_m, y_hg)]
        n_route = N_EXPERT_GROUPS + N_EXPERTS
        wr = jnp.pad(jnp.concatenate([moe_w_rg[l], moe_w_re[l]], axis=1), ((0, 0), (0, ROUTE_LANES - n_route)))
        wr_hi = wr.astype(BF16)
        wr_lo = (wr - wr_hi.astype(F32)).astype(BF16)
        br = jnp.pad(jnp.concatenate([moe_b_rg[l], moe_b_re[l]]), (0, ROUTE_LANES - n_route)).reshape(1, -1)
        h, xn, comb = _outproj(h, ys, w_out[l].astype(BF16), row(ln2_w[l]),
                               jnp.concatenate([wr_hi, wr_lo], axis=1), br)
        delta = _moe(xn, comb, wg16, wu16, wd16, l)
    return _final_norm(h, delta, row(lnf_w)).reshape(bsz, seq, d)
```

```python
import functools
import math

import jax
import jax.numpy as jnp
from jax import lax
from jax.experimental import pallas as pl
from jax.experimental.pallas import tpu as pltpu

F32 = jnp.float32
BF16 = jnp.bfloat16

NORM_EPS = 1e-6
GW = 256
HEAD_DIM = 64
N_HEADS = GW // HEAD_DIM
LANES = 128
SUBLANES = 8
RW_GN_EPS = 64e-5
HG_F_FLOOR = 1e-20
S5_STATE_W = 1024
M_CHUNK = 128
N_EXPERTS = 32
EXPERTS_PER_GROUP = 8
N_EXPERT_GROUPS = 4
ROUTE_LANES = LANES
ROUTE_OFF = N_EXPERT_GROUPS

VMEM_LIMIT = 56 * 1024 * 1024


def _mm(a, b):
    return jnp.dot(a.astype(BF16), b.astype(BF16), preferred_element_type=F32)


def _mm_nt(a, b):
    return lax.dot_general(a.astype(BF16), b.astype(BF16), (((1,), (1,)), ((), ())),
                           preferred_element_type=F32)


def _mm_tn(a, b):
    return lax.dot_general(a.astype(BF16), b.astype(BF16), (((0,), (0,)), ((), ())),
                           preferred_element_type=F32)


def _parts(x, n):
    out, rem = [], x
    for i in range(n):
        p = rem.astype(BF16)
        out.append(p)
        if i + 1 < n:
            rem = rem - p.astype(F32)
    return out


def _mm_x(a, b, na, nb):
    pa, pb = _parts(a, na), _parts(b, nb)
    acc = None
    for i in range(na):
        for j in range(nb):
            if i + j < max(na, nb):
                t = jnp.dot(pa[i], pb[j], preferred_element_type=F32)
                acc = t if acc is None else acc + t
    return acc


def _sigmoid(x):
    return 1.0 / (1.0 + jnp.exp(-x))


def _silu(x):
    return x * _sigmoid(x)


def _softplus(x):
    return jnp.maximum(x, 0.0) + jnp.log1p(jnp.exp(-jnp.abs(x)))


def _rms(x, w):
    ms = jnp.mean(x * x, axis=-1, keepdims=True)
    return x * lax.rsqrt(ms + NORM_EPS) * w


def _stack4(x, bd16):
    xb = x.astype(BF16)
    return jnp.concatenate([xb, xb, xb, xb], axis=0) * bd16


def _const(shape):
    return pl.BlockSpec(shape, lambda *_: (0,) * len(shape))


def _params(sem):
    return pltpu.CompilerParams(dimension_semantics=sem, vmem_limit_bytes=VMEM_LIMIT)


def _prenorm_body(*refs, with_delta):
    if with_delta:
        h_ref, dl_ref, lnw_ref, hn_ref, xn_ref = refs
        h = h_ref[...] + dl_ref[...]
        hn_ref[...] = h
    else:
        h_ref, lnw_ref, xn_ref = refs
        h = h_ref[...]
    xn_ref[...] = _rms(h, lnw_ref[...]).astype(BF16)


def _prenorm(h, delta, lnw, tm=1024):
    t, d = h.shape
    row = pl.BlockSpec((tm, d), lambda i: (i, 0))
    xn_shape = jax.ShapeDtypeStruct((t, d), BF16)
    with_delta = delta is not None
    out = pl.pallas_call(
        functools.partial(_prenorm_body, with_delta=with_delta),
        out_shape=(jax.ShapeDtypeStruct((t, d), F32), xn_shape) if with_delta else xn_shape,
        grid=(t // tm,),
        in_specs=([row, row] if with_delta else [row]) + [_const((1, d))],
        out_specs=(row, row) if with_delta else row,
        compiler_params=_params(("parallel",)),
        name="prenorm",
    )(*((h, delta) if with_delta else (h,)), lnw)
    return out if with_delta else (h, out)


RW_CHUNK = 64


def _rwkv_body(x_ref, win_ref, mu_ref, w0_ref, w2_ref, a0_ref, a2_ref, g2_ref, kk_ref, ka_ref, rk_ref,
               lnw_ref, lnb_ref, bd_ref, bd16_ref, hm_ref, tri_ref, lowi_ref, lows_ref, eye_ref,
               o_ref, carry_ref, st_ref, r_s, k_s, v_s, kn_s, kb_s, wl_s, y_s):
    nb, tt, d = x_ref.shape

    @pl.when(pl.program_id(1) == 0)
    def _():
        carry_ref[...] = jnp.zeros_like(carry_ref)
        st_ref[...] = jnp.zeros_like(st_ref)

    proj = jnp.dot(x_ref[...].reshape(nb * tt, d), win_ref[...], preferred_element_type=F32)
    rows = lax.broadcasted_iota(jnp.int32, (tt, 4 * GW), 0)
    mixed = []
    for b in range(nb):
        p = proj[b * tt:(b + 1) * tt]
        prev = jnp.where(rows == 0, carry_ref[b, 0:1, :], pltpu.roll(p, 1, axis=0))
        carry_ref[b, 0:1, :] = p[tt - 1:tt, :]
        mixed.append(p + (prev - p) * mu_ref[...])
    p = jnp.concatenate(mixed, axis=0)
    r = p[:, 0:GW]
    k = p[:, GW:2 * GW]
    v = p[:, 2 * GW:3 * GW]
    wl = p[:, 3 * GW:3 * GW + 64]
    al = p[:, 3 * GW + 64:3 * GW + 128]
    gl = p[:, 3 * GW + 128:]
    hm = hm_ref[...]
    w_log = -math.exp(-0.5) * _sigmoid(w0_ref[...] + _mm_x(jnp.tanh(wl), w2_ref[...], 2, 2))
    a = _sigmoid(a0_ref[...] + _mm(al, a2_ref[...]))
    gate = _mm(_sigmoid(gl), g2_ref[...])
    kn = k * kk_ref[...]
    kn = kn * lax.rsqrt(jnp.maximum(_mm_x(kn * kn, hm, 2, 1) * float(HEAD_DIM), 1e-24))
    k = k * (1.0 + (a - 1.0) * ka_ref[...])
    r_s[...] = r
    k_s[...] = k
    v_s[...] = v
    kn_s[...] = kn
    kb_s[...] = kn * a
    wl_s[...] = w_log

    def chunk(c, carry):
        seqs = range(nb)
        sls = [pl.ds(pl.multiple_of(b * tt + c * RW_CHUNK, RW_CHUNK), RW_CHUNK) for b in seqs]
        ld = lambda s: [s[sl, :] for sl in sls]
        r_c, k_c, v_c, kn_c, kb_c, wl_c = (ld(s) for s in (r_s, k_s, v_s, kn_s, kb_s, wl_s))
        bd = bd_ref[...]
        bd16 = bd16_ref[...]
        lows = lows_ref[...]
        lowi = lowi_ref[...]
        st4 = lambda xs: [_stack4(x, bd16) for x in xs]
        gc = [_mm_x(tri_ref[...], w, 1, 3) for w in wl_c]
        g_end = [g[RW_CHUNK - 1:RW_CHUNK, :] for g in gc]
        inv = [jnp.exp(-g) for g in gc]
        lhs = [jnp.concatenate([-kn * jnp.exp(g - w), r * jnp.exp(g)], axis=0)
               for kn, r, g, w in zip(kn_c, r_c, gc, wl_c)]
        rhs = [jnp.concatenate([_stack4(kb * i, bd16), _stack4(k * i, bd16)], axis=0)
               for kb, k, i in zip(kb_c, k_c, inv)]
        gram = [_mm_nt(a, b) for a, b in zip(lhs, rhs)]
        state = [st_ref[b] for b in seqs]
        from_state = [_mm_nt(a, s) for a, s in zip(lhs, state)]
        a_ab = [g[:RW_CHUNK, :GW] * lows for g in gram]
        pw = [eye_ref[...] + a for a in a_ab]
        q = [_mm(a, b) for a, b in zip(a_ab, st4(a_ab))]
        for step in range(4):
            both = [_mm(jnp.concatenate([p, a], axis=0), b) for p, a, b in zip(pw, q, st4(q))]
            pw = [p + pq[:RW_CHUNK] for p, pq in zip(pw, both)]
            q = [pq[RW_CHUNK:] for pq in both]
        pw = [p + _mm(p, b) for p, b in zip(pw, st4(q))]
        vst = st4(v_c)
        mask_k = jnp.concatenate([lows, lowi], axis=0)
        from_v = [_mm(g[:, GW:] * mask_k, vs) for g, vs in zip(gram, vst)]
        rhs_u = [f[:RW_CHUNK] + fv[:RW_CHUNK] for f, fv in zip(from_state, from_v)]
        u = [_mm(p, x) for p, x in zip(pw, st4(rhs_u))]
        ust = st4(u)
        for b in seqs:
            y_s[sls[b], :] = (from_state[b][RW_CHUNK:] + _mm(gram[b][RW_CHUNK:, :GW] * lowi, ust[b])
                              + from_v[b][RW_CHUNK:])
        for b in seqs:
            to_end = jnp.exp(g_end[b] - gc[b])
            upd = _mm_tn(jnp.concatenate([u[b], v_c[b]], axis=0),
                         jnp.concatenate([kb_c[b] * to_end, k_c[b] * to_end], axis=0))
            st_ref[b] = state[b] * jnp.exp(g_end[b]) + upd * bd
        return carry

    lax.fori_loop(0, tt // RW_CHUNK, chunk, 0)

    y = y_s[...]
    mean = _mm_x(y, hm, 2, 1)
    d = y - mean
    var = _mm_x(d * d, hm, 2, 1)
    yn = d * lax.rsqrt(var + RW_GN_EPS) * lnw_ref[...] + lnb_ref[...]
    bonus = _mm_x(r * k * rk_ref[...], hm, 2, 1) * float(HEAD_DIM) * v
    o_ref[...] = ((yn + bonus) * gate).astype(BF16).reshape(nb, tt, GW)


def _rwkv(xn3, w_in, prm, consts, nb=8, tt=128):
    b, l, d = xn3.shape
    vec = _const((1, GW))
    scr = pltpu.VMEM((nb * tt, GW), F32)
    return pl.pallas_call(
        _rwkv_body,
        out_shape=jax.ShapeDtypeStruct((b, l, GW), BF16),
        grid=(b // nb, l // tt),
        in_specs=[pl.BlockSpec((nb, tt, d), lambda i, j: (i, j, 0)), _const((d, 4 * GW)),
                  _const((1, 4 * GW)), vec, _const((64, GW)), vec, _const((64, GW)),
                  _const((128, GW)), vec, vec, vec, vec, vec,
                  _const((GW, GW)), _const((GW, GW)), _const((GW, GW)), _const((RW_CHUNK, RW_CHUNK)),
                  _const((RW_CHUNK, GW)), _const((RW_CHUNK, GW)), _const((RW_CHUNK, GW))],
        out_specs=pl.BlockSpec((nb, tt, GW), lambda i, j: (i, j, 0)),
        scratch_shapes=[pltpu.VMEM((nb, SUBLANES, 4 * GW), F32), pltpu.VMEM((nb, GW, GW), F32)] + [scr] * 7,
        compiler_params=_params(("parallel", "arbitrary")),
        name="rwkv7",
    )(xn3, w_in, prm["mu"], prm["w0"], prm["w2"], prm["a0"], prm["a2"], prm["g2"], prm["k_k"],
      prm["k_a"], prm["r_k"], prm["lnx_w"], prm["lnx_b"],
      consts["bd"], consts["hones"], consts["hm"], consts["tri64"], consts["lowi"], consts["lows"], consts["eyew"])


def _s5_body(x_ref, win_ref, bbd_ref, ar_ref, ai_ref, cbd_ref, d_ref, wglu_ref, bglu_ref, o_ref,
             ub_s, ut_s, xr_s, xi_s, ot_s, st_ref):
    nb, tt, d = x_ref.shape
    w = win_ref.shape[1]
    ncb = w // LANES
    assert nb == SUBLANES

    @pl.when(pl.program_id(0) == 0)
    def _():
        st_ref[...] = jnp.zeros_like(st_ref)

    u_bt = jnp.dot(x_ref[...].reshape(nb * tt, d), win_ref[...], preferred_element_type=F32)
    for cb in range(ncb):
        ub_s[cb] = u_bt[:, cb * LANES:(cb + 1) * LANES]

    def regroup(t, carry):
        for cb in range(ncb):
            ut_s[cb, pl.ds(pl.multiple_of(t * nb, nb), nb), :] = ub_s[cb, pl.ds(t, nb, stride=tt), :]
        return carry

    lax.fori_loop(0, tt, regroup, 0, unroll=8)
    u = jnp.concatenate([ut_s[cb] for cb in range(ncb)], axis=-1)
    bu = _mm(u, bbd_ref[...])
    xr_s[...] = bu[:, :S5_STATE_W]
    xi_s[...] = bu[:, S5_STATE_W:]
    ar = jnp.broadcast_to(ar_ref[...], (nb, S5_STATE_W))
    ai = jnp.broadcast_to(ai_ref[...], (nb, S5_STATE_W))

    def step(t, carry):
        xr, xi = carry
        rows = pl.ds(pl.multiple_of(t * nb, nb), nb)
        nr = ar * xr - ai * xi + xr_s[rows, :]
        ni = ar * xi + ai * xr + xi_s[rows, :]
        xr_s[rows, :] = nr
        xi_s[rows, :] = ni
        return nr, ni

    xr, xi = lax.fori_loop(0, tt, step, (st_ref[0], st_ref[1]), unroll=4)
    st_ref[0] = xr
    st_ref[1] = xi
    cbd = cbd_ref[...]
    y = _mm(xr_s[...], cbd[:S5_STATE_W]) + _mm(xi_s[...], cbd[S5_STATE_W:]) + d_ref[...] * u
    y = 0.5 * y * (1.0 + jnp.tanh(math.sqrt(2.0 / math.pi) * (y + 0.044715 * (y * y * y))))
    z = _mm(y, wglu_ref[...]) + bglu_ref[...]
    out = y * _sigmoid(z)
    for cb in range(ncb):
        ot_s[cb] = out[:, cb * LANES:(cb + 1) * LANES]
    for b in range(nb):
        for cb in range(ncb):
            o_ref[b, :, cb * LANES:(cb + 1) * LANES] = ot_s[cb, pl.ds(b, tt, stride=nb), :].astype(BF16)


def _s5(xn3, w_in, prm, tt=128):
    b, l, d = xn3.shape
    slab = pltpu.VMEM((GW // LANES, b * tt, LANES), F32)
    wide = pltpu.VMEM((b * tt, S5_STATE_W), F32)
    return pl.pallas_call(
        _s5_body,
        out_shape=jax.ShapeDtypeStruct((b, l, GW), BF16),
        grid=(l // tt,),
        in_specs=[pl.BlockSpec((b, tt, d), lambda j: (0, j, 0)), _const((d, GW)),
                  _const((GW, 2 * S5_STATE_W)), _const((1, S5_STATE_W)), _const((1, S5_STATE_W)),
                  _const((2 * S5_STATE_W, GW)), _const((1, GW)), _const((GW, GW)), _const((1, GW))],
        out_specs=pl.BlockSpec((b, tt, GW), lambda j: (0, j, 0)),
        scratch_shapes=[slab, slab, wide, wide, slab, pltpu.VMEM((2, b, S5_STATE_W), F32)],
        compiler_params=_params(("arbitrary",)),
        name="s5",
    )(xn3, w_in, prm["bbd"], prm["ar"], prm["ai"], prm["cbd"], prm["d"], prm["w_glu"], prm["b_glu"])


def _mamba_body(x_ref, win_ref, cw_ref, cb_ref, dtb_ref, aneg_ref, dexp_ref, nw_ref,
                tri_ref, exp_ref, gsel_ref, o_ref, carry_ref, st_ref):
    nb, tt, d = x_ref.shape
    seqs = range(nb)

    @pl.when(pl.program_id(1) == 0)
    def _():
        carry_ref[...] = jnp.zeros_like(carry_ref)
        st_ref[...] = jnp.zeros_like(st_ref)

    proj = jnp.dot(x_ref[...].reshape(nb * tt, d), win_ref[...], preferred_element_type=F32)
    cw = 2 * GW
    rows = lax.broadcasted_iota(jnp.int32, (tt, cw), 0)
    pad = jnp.zeros((tt - SUBLANES, cw), F32)
    xcs = []
    for b in seqs:
        xbc = proj[b * tt:(b + 1) * tt, GW:GW + cw]
        tail = carry_ref[b]
        conv = xbc * cw_ref[3:4, :]
        for s in (1, 2, 3):
            head = jnp.concatenate([pltpu.roll(tail, s, axis=0), pad], axis=0)
            shifted = jnp.where(rows < s, head, pltpu.roll(xbc, s, axis=0))
            conv = conv + shifted * cw_ref[3 - s:4 - s, :]
        carry_ref[b] = xbc[tt - SUBLANES:, :]
        xcs.append(_silu(conv + cb_ref[...]))
    xs = [xc[:, :GW] for xc in xcs]
    bm = [xc[:, GW:GW + 128] for xc in xcs]
    cm = [xc[:, GW + 128:] for xc in xcs]
    expand = exp_ref[...]
    dt = [_softplus(proj[b * tt:(b + 1) * tt, GW + cw:] + dtb_ref[...]) for b in seqs]
    cs = [_mm_x(tri_ref[...], x * aneg_ref[...], 1, 3) for x in dt]
    cs_t = [c.T for c in cs]
    cs_end = [c[tt - 1:tt, :] for c in cs]
    wide = [_mm_x(jnp.concatenate([x, c, ce - c, jnp.broadcast_to(ce, (SUBLANES, 128))], axis=0), expand, 3, 1)
            for x, c, ce in zip(dt, cs, cs_end)]
    xdt = [x * w[:tt] for x, w in zip(xs, wide)]
    lane = lax.broadcasted_iota(jnp.int32, (tt, 128), 1)
    tril = lax.broadcasted_iota(jnp.int32, (tt, tt), 0) >= lax.broadcasted_iota(jnp.int32, (tt, tt), 1)
    lane_w = lax.broadcasted_iota(jnp.int32, (tt, GW), 1)
    state = [st_ref[b] for b in seqs]
    y = [_mm(c, s) * jnp.exp(w[tt:2 * tt]) + dexp_ref[...] * x for c, s, w, x in zip(cm, state, wide, xs)]
    for g in range(2):
        cbm = [_mm_nt(jnp.where(lane // 64 == g, c, 0.0), b_) for c, b_ in zip(cm, bm)]
        for h in (2 * g, 2 * g + 1):
            decay = [jnp.where(tril, jnp.exp(jnp.minimum(c[:, h:h + 1] - ct[h:h + 1, :], 0.0)), 0.0)
                     for c, ct in zip(cs, cs_t)]
            y = [yy + jnp.where(lane_w // HEAD_DIM == h, _mm(m * dc, xd), 0.0)
                 for yy, m, dc, xd in zip(y, cbm, decay, xdt)]
    upd = [_mm_tn(b_, xd * jnp.exp(w[2 * tt:3 * tt])) for b_, xd, w in zip(bm, xdt, wide)]
    nw = nw_ref[...]
    half = GW // 2
    for b in seqs:
        st_ref[b] = state[b] * jnp.exp(wide[b][3 * tt:3 * tt + 1]) + upd[b] * gsel_ref[...]
        yb = y[b] * _silu(proj[b * tt:(b + 1) * tt, :GW])
        o_ref[b] = jnp.concatenate([_rms(yb[:, :half], nw[:, :half]), _rms(yb[:, half:], nw[:, half:])],
                                   axis=-1).astype(BF16)


def _mamba(xn3, w_in, prm, consts, nb=4):
    b, l, d = xn3.shape
    tt = M_CHUNK
    wcols = w_in.shape[1]
    return pl.pallas_call(
        _mamba_body,
        out_shape=jax.ShapeDtypeStruct((b, l, GW), BF16),
        grid=(b // nb, l // tt),
        in_specs=[pl.BlockSpec((nb, tt, d), lambda i, j: (i, j, 0)), _const((d, wcols)),
                  _const((4, 2 * GW)), _const((1, 2 * GW)), _const((1, 128)), _const((1, 128)),
                  _const((1, GW)), _const((1, GW)),
                  _const((tt, tt)), _const((128, GW)), _const((128, GW))],
        out_specs=pl.BlockSpec((nb, tt, GW), lambda i, j: (i, j, 0)),
        scratch_shapes=[pltpu.VMEM((nb, SUBLANES, 2 * GW), F32), pltpu.VMEM((nb, 128, GW), F32)],
        compiler_params=_params(("parallel", "arbitrary")),
        name="mamba2",
    )(xn3, w_in, prm["conv_w"], prm["conv_b"], prm["dt_bias"], prm["a_neg"], prm["d_exp"],
      prm["norm_w"], consts["tri128"], consts["expand"], consts["gsel"])


HG_BLOCK = 16
HG_CHUNK = 2 * HG_BLOCK


def _hgrn_body(x_ref, win_ref, lb_ref, nw_ref, bd_ref, bd64_ref, hm_ref, hones_ref, tri_ref, o_ref,
               st_ref, q_s, k_s, v_s, lf_s, o_s):
    nb, tt, d = x_ref.shape

    @pl.when(pl.program_id(1) == 0)
    def _():
        st_ref[...] = jnp.zeros_like(st_ref)

    p = jnp.dot(x_ref[...].reshape(nb * tt, d), win_ref[...], preferred_element_type=F32)
    lb = lb_ref[...]
    hf = p[:, GW:2 * GW]
    q_s[...] = _silu(p[:, :GW])
    k_s[...] = (1.0 - lb) * _sigmoid(-hf)
    v_s[...] = p[:, 2 * GW:3 * GW]
    lf_s[...] = jnp.log(jnp.maximum(lb + (1.0 - lb) * _sigmoid(hf), HG_F_FLOOR))

    def block(n, carry):
        seqs = range(nb)
        sls = [pl.ds(pl.multiple_of(b * tt + n * HG_CHUNK, HG_CHUNK), HG_CHUNK) for b in seqs]
        ld = lambda s: [s[sl, :] for sl in sls]
        q_c, k_c, v_c, lf_c = (ld(s) for s in (q_s, k_s, v_s, lf_s))
        g = [_mm_x(tri_ref[...], lf, 1, 3) for lf in lf_c]
        g_end = [x[HG_CHUNK - 1:HG_CHUNK, :] for x in g]
        state = [st_ref[b] for b in seqs]
        o = [_mm_nt(q * jnp.exp(x), s) for q, x, s in zip(q_c, g, state)]
        upd = [_mm_tn(v, k * jnp.exp(ge - x)) for v, k, ge, x in zip(v_c, k_c, g_end, g)]
        blk = lambda a, i: a.reshape(2, HG_BLOCK, GW)[i]
        half = HG_BLOCK // 2
        lower = lambda a: a.reshape(2, half, GW)[1]
        bd64 = bd64_ref[...]
        st4s = lambda a: jnp.concatenate([a.astype(BF16)] * N_HEADS, axis=0) * bd64
        g_mid = [blk(x, 0)[HG_BLOCK - 1:HG_BLOCK, :] for x in g]
        a_off = [_mm_nt(blk(q, 1) * jnp.exp(blk(x, 1) - gm), st4s(blk(k, 0) * jnp.exp(gm - blk(x, 0))))
                 for q, k, x, gm in zip(q_c, k_c, g, g_mid)]
        o_off = [_mm(a, st4s(blk(v, 0))) for a, v in zip(a_off, v_c)]
        prods = []
        for q32, k32, x32 in zip(q_c, k_c, g):
            pieces = []
            for i in range(2):
                q, k, x = blk(q32, i), blk(k32, i), blk(x32, i)
                ql, xl = lower(q), lower(x)
                pieces += [q * jnp.exp(jnp.minimum(x - x[j:j + 1, :], 0.0)) * k[j:j + 1, :] for j in range(half)]
                pieces += [ql * jnp.exp(jnp.minimum(xl - x[j:j + 1, :], 0.0)) * k[j:j + 1, :]
                           for j in range(half, HG_BLOCK)]
            prods.append(jnp.concatenate(pieces, axis=0))
        att = [_mm(p, hones_ref[...]) for p in prods]
        row = lax.broadcasted_iota(jnp.int32, (HG_BLOCK, GW), 0)
        rowl = lax.broadcasted_iota(jnp.int32, (half, GW), 0) + half
        per_blk = HG_BLOCK * half + half * half
        for b in seqs:
            outs = []
            for i in range(2):
                base = i * per_blk
                a_full = att[b][base:base + HG_BLOCK * half].reshape(half, HG_BLOCK, GW)
                a_low = att[b][base + HG_BLOCK * half:base + per_blk].reshape(half, half, GW)
                v16 = blk(v_c[b], i)
                ob = blk(o[b], i) if i == 0 else blk(o[b], i) + o_off[b]
                for j in range(half):
                    ob = ob + jnp.where(row >= j, a_full[j], 0.0) * v16[j:j + 1, :]
                add = jnp.zeros((half, GW), F32)
                for j in range(half, HG_BLOCK):
                    add = add + jnp.where(rowl >= j, a_low[j - half], 0.0) * v16[j:j + 1, :]
                outs.append(ob + jnp.concatenate([jnp.zeros((half, GW), F32), add], axis=0))
            o_s[sls[b], :] = jnp.concatenate(outs, axis=0)
            st_ref[b] = state[b] * jnp.exp(g_end[b]) + upd[b] * bd_ref[...]
        return carry

    lax.fori_loop(0, tt // HG_CHUNK, block, 0)
    o = o_s[...]
    ms = _mm_x(o * o, hm_ref[...], 2, 1)
    o_ref[...] = (o * lax.rsqrt(ms + NORM_EPS) * nw_ref[...] * _silu(p[:, 3 * GW:])).astype(BF16).reshape(nb, tt, GW)


def _hgrn(xn3, w_in, lb, nw, consts, nb=8, tt=128):
    b, l, d = xn3.shape
    return pl.pallas_call(
        _hgrn_body,
        out_shape=jax.ShapeDtypeStruct((b, l, GW), BF16),
        grid=(b // nb, l // tt),
        in_specs=[pl.BlockSpec((nb, tt, d), lambda i, j: (i, j, 0)), _const((d, 4 * GW)),
                  _const((1, GW)), _const((1, GW)), _const((GW, GW)), _const((N_HEADS * HG_BLOCK, GW)),
                  _const((GW, GW)), _const((GW, GW)), _const((HG_CHUNK, HG_CHUNK))],
        out_specs=pl.BlockSpec((nb, tt, GW), lambda i, j: (i, j, 0)),
        scratch_shapes=[pltpu.VMEM((nb, GW, GW), F32)] + [pltpu.VMEM((nb * tt, GW), F32)] * 5,
        compiler_params=_params(("parallel", "arbitrary")),
        name="hgrn2",
    )(xn3, w_in, lb, nw, consts["bd"], consts["bd64"], consts["hm"], consts["hones"], consts["tri32"])


def _outproj_body(h_ref, y1_ref, y2_ref, y3_ref, y4_ref, wo_ref, ln2_ref, wr_ref, br_ref,
                  hn_ref, xn_ref, comb_ref):
    y = jnp.concatenate([y1_ref[...], y2_ref[...], y3_ref[...], y4_ref[...]], axis=-1)
    h = h_ref[...] + _mm(y, wo_ref[...])
    hn_ref[...] = h
    xn = _rms(h, ln2_ref[...])
    xn_ref[...] = xn.astype(BF16)
    xh, xl = _parts(xn, 2)
    wr = wr_ref[...]
    first = jnp.dot(xh, wr, preferred_element_type=F32)
    logits = (first[:, :ROUTE_LANES] + first[:, ROUTE_LANES:]
              + jnp.dot(xl, wr[:, :ROUTE_LANES], preferred_element_type=F32) + br_ref[...])
    lane = lax.broadcasted_iota(jnp.int32, logits.shape, 1)
    neg = -jnp.inf
    big = ROUTE_LANES
    glog = jnp.where(lane < N_EXPERT_GROUPS, logits, neg)
    gmax = jnp.max(glog, axis=-1, keepdims=True)
    g_w = 1.0 / jnp.sum(jnp.exp(glog - gmax), axis=-1, keepdims=True)
    g_idx = jnp.min(jnp.where(glog == gmax, lane, big), axis=-1, keepdims=True)
    lo = ROUTE_OFF + EXPERTS_PER_GROUP * g_idx
    elog = jnp.where((lane >= lo) & (lane < lo + EXPERTS_PER_GROUP), logits, neg)
    m1 = jnp.max(elog, axis=-1, keepdims=True)
    i1 = jnp.min(jnp.where(elog == m1, lane, big), axis=-1, keepdims=True)
    elog2 = jnp.where(lane == i1, neg, elog)
    m2 = jnp.max(elog2, axis=-1, keepdims=True)
    i2 = jnp.min(jnp.where(elog2 == m2, lane, big), axis=-1, keepdims=True)
    e2 = jnp.exp(m2 - m1)
    w1 = 1.0 / (1.0 + e2)
    w2 = e2 / (1.0 + e2)
    comb_ref[...] = (g_w * (jnp.where(lane == i1 - lo, w1, 0.0) + jnp.where(lane == i2 - lo, w2, 0.0))
                     + jnp.where(lane == EXPERTS_PER_GROUP, g_idx.astype(F32), 0.0))


def _outproj(h, ys, wo, ln2, wr, br, tm=1024):
    t, d = h.shape
    row = lambda w: pl.BlockSpec((tm, w), lambda i: (i, 0))
    return pl.pallas_call(
        _outproj_body,
        out_shape=(jax.ShapeDtypeStruct((t, d), F32), jax.ShapeDtypeStruct((t, d), BF16),
                   jax.ShapeDtypeStruct((t, ROUTE_LANES), F32)),
        grid=(t // tm,),
        in_specs=[row(d), row(GW), row(GW), row(GW), row(GW), _const((d, d)), _const((1, d)),
                  _const((d, 2 * ROUTE_LANES)), _const((1, ROUTE_LANES))],
        out_specs=(row(d), row(d), row(ROUTE_LANES)),
        compiler_params=_params(("parallel",)),
        name="outproj_router",
    )(h, *ys, wo, ln2, wr, br)


MOE_SUB = 512
MOE_ROWS = 64
MOE_EXTRA = 16
MOE_HALF = EXPERTS_PER_GROUP // 2
COMB_GROUP_LANE = EXPERTS_PER_GROUP


def _moe_body(x_ref, comb_ref, wg_ref, wu_ref, wd_ref, o_ref, tri_s, kt_s, wt_s, cnt_s):
    i = pl.program_id(0)
    g = pl.program_id(1)
    hf = pl.program_id(2)
    tm = x_ref.shape[0]
    nsub = tm // MOE_SUB
    subs = [slice(s * MOE_SUB, (s + 1) * MOE_SUB) for s in range(nsub)]
    steps = N_EXPERTS // MOE_HALF

    @pl.when((i == 0) & (g == 0) & (hf == 0))
    def _():
        r = lax.broadcasted_iota(jnp.int32, (MOE_SUB, MOE_SUB), 0)
        c = lax.broadcasted_iota(jnp.int32, (MOE_SUB, MOE_SUB), 1)
        tri_s[...] = jnp.where(r < c, 1.0, 0.0).astype(BF16)

    @pl.when((g == 0) & (hf == 0))
    def _():
        comb = comb_ref[...]
        lane = lax.broadcasted_iota(jnp.int32, comb.shape, 1)
        gcol = comb[:, COMB_GROUP_LANE:COMB_GROUP_LANE + 1]
        local = jnp.where(lane < EXPERTS_PER_GROUP, comb, 0.0)
        w_tok = jnp.where(gcol == 0.0, local, 0.0)
        for grp in range(1, N_EXPERT_GROUPS):
            w_tok = w_tok + jnp.where(gcol == float(grp), pltpu.roll(local, EXPERTS_PER_GROUP * grp, axis=1), 0.0)
        wt = w_tok.T
        used = wt != 0.0
        wt_s[...] = wt
        most = jnp.zeros((ROUTE_LANES, 1), F32)
        for rows in subs:
            u = jnp.where(used[:, rows], 1.0, 0.0)
            before = jnp.dot(u.astype(BF16), tri_s[...], preferred_element_type=F32)
            kt_s[:, rows] = jnp.where(used[:, rows], before, -1.0)
            most = jnp.maximum(most, jnp.sum(u, axis=-1, keepdims=True))
        for q in range(steps):
            cnt_s[q] = jnp.max(most[MOE_HALF * q:MOE_HALF * (q + 1), :]).astype(jnp.int32)
        o_ref[...] = jnp.zeros_like(o_ref)

    second = hf == 1
    group_rows = pl.ds(pl.multiple_of(EXPERTS_PER_GROUP * g, EXPERTS_PER_GROUP), EXPERTS_PER_GROUP)
    kt8 = kt_s[group_rows, :]
    wt8 = wt_s[group_rows, :]
    pick = lambda a, j: jnp.where(second, a[MOE_HALF + j:MOE_HALF + j + 1, :], a[j:j + 1, :])
    keys = [pick(kt8, j) for j in range(MOE_HALF)]
    wrow = [pick(wt8, j) for j in range(MOE_HALF)]
    tn = lambda a, b: lax.dot_general(a, b, (((0,), (0,)), ((), ())), preferred_element_type=F32)

    def run_pass(base, nrows):
        ridx = lax.broadcasted_iota(jnp.int32, (nrows, MOE_SUB), 0).astype(F32)
        hit = [[keys[j][:, rows] - base == ridx for rows in subs] for j in range(MOE_HALF)]
        pts = [[jnp.where(m, 1.0, 0.0).astype(BF16) for m in row] for row in hit]
        pcat = [jnp.concatenate([pts[j][s] for j in range(MOE_HALF)], axis=0) for s in range(nsub)]
        xall = [jnp.dot(p, x_ref[rows, :], preferred_element_type=F32).astype(BF16)
                for p, rows in zip(pcat, subs)]
        ys = []
        for j in range(MOE_HALF):
            part = slice(j * nrows, (j + 1) * nrows)
            xg = jnp.concatenate([xa[part] for xa in xall], axis=0)
            cg = jnp.concatenate([jnp.sum(jnp.where(m, wrow[j][:, rows], 0.0), axis=-1, keepdims=True)
                                  for m, rows in zip(hit[j], subs)], axis=0)
            act = (_silu(jnp.dot(xg, wg_ref[0, j], preferred_element_type=F32))
                   * jnp.dot(xg, wu_ref[0, j], preferred_element_type=F32) * cg)
            ys.append(_mm(act, wd_ref[0, j]))
        for s, rows in enumerate(subs):
            part = slice(s * nrows, (s + 1) * nrows)
            ycat = jnp.concatenate([ys[j][part] for j in range(MOE_HALF)], axis=0)
            o_ref[rows, :] += tn(pcat[s], ycat.astype(BF16))

    count = cnt_s[2 * g + hf]

    @pl.when(count > 0)
    def _():
        run_pass(jnp.float32(0.0), MOE_ROWS)

    def extra(p, carry):
        run_pass((MOE_ROWS + p * MOE_EXTRA).astype(F32), MOE_EXTRA)
        return carry

    lax.fori_loop(0, (jnp.maximum(count - MOE_ROWS, 0) + MOE_EXTRA - 1) // MOE_EXTRA, extra, 0)


def _moe(xn, comb, wg, wu, wd, layer, tm=2048):
    t, d = xn.shape
    _, ng, eg, _, de = wg.shape
    row = lambda w: pl.BlockSpec((tm, w), lambda i, g, hf: (i, 0))
    wspec = lambda a, b: pl.BlockSpec((None, 1, MOE_HALF, a, b), lambda i, g, hf: (layer, g, hf, 0, 0))
    return pl.pallas_call(
        _moe_body,
        out_shape=jax.ShapeDtypeStruct((t, d), F32),
        grid=(t // tm, ng, eg // MOE_HALF),
        in_specs=[row(d), row(ROUTE_LANES), wspec(d, de), wspec(d, de), wspec(de, d)],
        out_specs=row(d),
        scratch_shapes=[pltpu.VMEM((MOE_SUB, MOE_SUB), BF16), pltpu.VMEM((ROUTE_LANES, tm), F32),
                        pltpu.VMEM((ROUTE_LANES, tm), F32), pltpu.SMEM((N_EXPERTS // MOE_HALF,), jnp.int32)],
        compiler_params=_params(("arbitrary", "arbitrary", "arbitrary")),
        name="moe",
    )(xn, comb, wg, wu, wd)


def _norm_body(h_ref, dl_ref, w_ref, o_ref):
    o_ref[...] = _rms(h_ref[...] + dl_ref[...], w_ref[...])


def _final_norm(h, delta, w, tm=1024):
    t, d = h.shape
    row = pl.BlockSpec((tm, d), lambda i: (i, 0))
    return pl.pallas_call(
        _norm_body,
        out_shape=jax.ShapeDtypeStruct((t, d), F32),
        grid=(t // tm,),
        in_specs=[row, row, _const((1, d))],
        out_specs=row,
        compiler_params=_params(("parallel",)),
        name="final_norm",
    )(h, delta, w)


def _mask_consts():
    i256 = jnp.arange(GW)
    same_head = (i256[:, None] // HEAD_DIM) == (i256[None, :] // HEAD_DIM)
    t64 = jnp.arange(RW_CHUNK)
    s_w = i256 % RW_CHUNK
    h128 = jnp.arange(128)
    return {
        "bd": same_head.astype(F32),
        "hm": same_head.astype(F32) / HEAD_DIM,
        "hones": same_head.astype(BF16),
        "tri64": (t64[:, None] >= t64[None, :]).astype(F32),
        "lowi": (t64[:, None] >= s_w[None, :]).astype(F32),
        "lows": (t64[:, None] > s_w[None, :]).astype(F32),
        "eyew": (t64[:, None] == s_w[None, :]).astype(F32),
        "tri32": (jnp.arange(HG_CHUNK)[:, None] >= jnp.arange(HG_CHUNK)[None, :]).astype(F32),
        "bd64": ((jnp.arange(N_HEADS * HG_BLOCK)[:, None] // HG_BLOCK) == (i256[None, :] // HEAD_DIM)).astype(BF16),
        "tri128": (h128[:, None] >= h128[None, :]).astype(F32),
        "expand": (h128[:, None] == (i256[None, :] // HEAD_DIM)).astype(F32),
        "gsel": ((h128[:, None] // 64) == (i256[None, :] // 128)).astype(F32),
    }


def _s5_params(lam_re, lam_im, log_dt, b_re, b_im, c_re, c_im, d_skip, w_glu, b_glu):
    lr = jnp.minimum(lam_re, -1e-4)
    li = lam_im
    dt = jnp.exp(log_dt)[:, None]
    mag = jnp.exp(lr * dt)
    ar, ai = mag * jnp.cos(li * dt), mag * jnp.sin(li * dt)
    den = lr * lr + li * li
    nr = ar - 1.0
    er, ei = (nr * lr + ai * li) / den, (ai * lr - nr * li) / den
    bbr = er[..., None] * b_re - ei[..., None] * b_im
    bbi = er[..., None] * b_im + ei[..., None] * b_re
    eye = jnp.eye(lam_re.shape[0], dtype=F32)
    pack_b = lambda m: jnp.einsum("gph,gk->ghkp", m, eye).reshape(GW, S5_STATE_W)
    pack_c = lambda m: jnp.einsum("ghp,gk->gpkh", m, eye).reshape(S5_STATE_W, GW)
    return {
        "bbd": jnp.concatenate([pack_b(bbr), pack_b(bbi)], axis=1).astype(BF16),
        "cbd": jnp.concatenate([pack_c(c_re), -pack_c(c_im)], axis=0).astype(BF16),
        "ar": ar.reshape(1, S5_STATE_W), "ai": ai.reshape(1, S5_STATE_W),
        "d": d_skip.reshape(1, GW), "w_glu": w_glu.astype(BF16), "b_glu": b_glu.reshape(1, GW),
    }


def kernel(x, ln1_w, w_in, rw_mu, rw_w0, rw_w2, rw_a0, rw_a2, rw_g2, rw_k_k, rw_k_a, rw_r_k, rw_lnx_w, rw_lnx_b, s5_lam_re, s5_lam_im, s5_log_dt, s5_b_re, s5_b_im, s5_c_re, s5_c_im, s5_d, s5_w_glu, s5_b_glu, m_conv_w, m_conv_b, m_dt_bias, m_a_log, m_d, m_norm_w, hg_lb_logits, hg_norm_w, w_out, ln2_w, moe_w_rg, moe_b_rg, moe_w_re, moe_b_re, moe_w_gate, moe_w_up, moe_w_down, lnf_w):
    bsz, seq, d = x.shape
    depth = w_in.shape[0]
    consts = _mask_consts()
    lbs = jax.nn.softmax(hg_lb_logits.astype(F32), axis=0)
    lbs = jnp.cumsum(lbs, axis=0) - lbs[0:1]
    row = lambda v: v.reshape(1, -1).astype(F32)
    n_dt = N_HEADS
    h = x.reshape(bsz * seq, d)
    delta = None
    wg16, wu16, wd16 = (w.astype(BF16) for w in (moe_w_gate, moe_w_up, moe_w_down))
    for l in range(depth):
        c_s5, c_m, c_hg = 4 * GW, 5 * GW, 8 * GW + n_dt
        w_rw = w_in[l, :, :c_s5].astype(BF16)
        w_s5 = w_in[l, :, c_s5:c_m].astype(BF16)
        w_m = jnp.pad(w_in[l, :, c_m:c_hg].astype(BF16), ((0, 0), (0, LANES - n_dt)))
        w_hg = w_in[l, :, c_hg:].astype(BF16)
        h, xn1 = _prenorm(h, delta, row(ln1_w[l]))
        xn1 = xn1.reshape(bsz, seq, d)
        rw = {"mu": row(rw_mu[l]), "w0": row(rw_w0[l]), "w2": rw_w2[l], "a0": row(rw_a0[l]),
              "a2": rw_a2[l], "g2": rw_g2[l], "k_k": row(rw_k_k[l]), "k_a": row(rw_k_a[l]),
              "r_k": row(rw_r_k[l]), "lnx_w": row(rw_lnx_w[l]), "lnx_b": row(rw_lnx_b[l])}
        y_rw = _rwkv(xn1, w_rw, rw, consts)
        y_s5 = _s5(xn1, w_s5, _s5_params(s5_lam_re[l], s5_lam_im[l], s5_log_dt[l], s5_b_re[l], s5_b_im[l],
                                    s5_c_re[l], s5_c_im[l], s5_d[l], s5_w_glu[l], s5_b_glu[l]))
        pad_h = lambda v: jnp.pad(v.astype(F32), (0, LANES - n_dt)).reshape(1, LANES)
        mp = {"conv_w": m_conv_w[l], "conv_b": row(m_conv_b[l]), "dt_bias": pad_h(m_dt_bias[l]),
              "a_neg": pad_h(-jnp.exp(m_a_log[l].astype(F32))),
              "d_exp": row(jnp.repeat(m_d[l], HEAD_DIM)), "norm_w": row(m_norm_w[l])}
        y_m = _mamba(xn1, w_m, mp, consts)
        y_hg = _hgrn(xn1, w_hg, row(lbs[l]), row(hg_norm_w[l]), consts)
        ys = [y.reshape(bsz * seq, GW) for y in (y_rw, y_s5, y_m, y_hg)]
        n_route = N_EXPERT_GROUPS + N_EXPERTS
        wr = jnp.pad(jnp.concatenate([moe_w_rg[l], moe_w_re[l]], axis=1), ((0, 0), (0, ROUTE_LANES - n_route)))
        wr_hi = wr.astype(BF16)
        wr_lo = (wr - wr_hi.astype(F32)).astype(BF16)
        br = jnp.pad(jnp.concatenate([moe_b_rg[l], moe_b_re[l]]), (0, ROUTE_LANES - n_route)).reshape(1, -1)
        h, xn, comb = _outproj(h, ys, w_out[l].astype(BF16), row(ln2_w[l]),
                               jnp.concatenate([wr_hi, wr_lo], axis=1), br)
        delta = _moe(xn, comb, wg16, wu16, wd16, l)
    return _final_norm(h, delta, row(lnf_w)).reshape(bsz, seq, d)
```

```python
import functools
import math

import jax
import jax.numpy as jnp
from jax import lax
from jax.experimental import pallas as pl
from jax.experimental.pallas import tpu as pltpu

F32 = jnp.float32
BF16 = jnp.bfloat16

NORM_EPS = 1e-6
GW = 256
HEAD_DIM = 64
N_HEADS = GW // HEAD_DIM
LANES = 128
SUBLANES = 8
RW_GN_EPS = 64e-5
HG_F_FLOOR = 1e-20
S5_STATE_W = 1024
M_CHUNK = 128
N_EXPERTS = 32
EXPERTS_PER_GROUP = 8
N_EXPERT_GROUPS = 4
ROUTE_LANES = LANES
ROUTE_OFF = N_EXPERT_GROUPS

VMEM_LIMIT = 56 * 1024 * 1024


def _mm(a, b):
    return jnp.dot(a.astype(BF16), b.astype(BF16), preferred_element_type=F32)


def _mm_nt(a, b):
    return lax.dot_general(a.astype(BF16), b.astype(BF16), (((1,), (1,)), ((), ())),
                           preferred_element_type=F32)


def _mm_tn(a, b):
    return lax.dot_general(a.astype(BF16), b.astype(BF16), (((0,), (0,)), ((), ())),
                           preferred_element_type=F32)


def _parts(x, n):
    out, rem = [], x
    for i in range(n):
        p = rem.astype(BF16)
        out.append(p)
        if i + 1 < n:
            rem = rem - p.astype(F32)
    return out


def _mm_x(a, b, na, nb):
    pa, pb = _parts(a, na), _parts(b, nb)
    acc = None
    for i in range(na):
        for j in range(nb):
            if i + j < max(na, nb):
                t = jnp.dot(pa[i], pb[j], preferred_element_type=F32)
                acc = t if acc is None else acc + t
    return acc


def _sigmoid(x):
    return 1.0 / (1.0 + jnp.exp(-x))


def _silu(x):
    return x * _sigmoid(x)


def _softplus(x):
    return jnp.maximum(x, 0.0) + jnp.log1p(jnp.exp(-jnp.abs(x)))


def _rms(x, w):
    ms = jnp.mean(x * x, axis=-1, keepdims=True)
    return x * lax.rsqrt(ms + NORM_EPS) * w


def _stack4(x, bd16):
    xb = x.astype(BF16)
    return jnp.concatenate([xb, xb, xb, xb], axis=0) * bd16


def _const(shape):
    return pl.BlockSpec(shape, lambda *_: (0,) * len(shape))


def _params(sem):
    return pltpu.CompilerParams(dimension_semantics=sem, vmem_limit_bytes=VMEM_LIMIT)


def _prenorm_body(*refs, with_delta):
    if with_delta:
        h_ref, dl_ref, lnw_ref, hn_ref, xn_ref = refs
        h = h_ref[...] + dl_ref[...]
        hn_ref[...] = h
    else:
        h_ref, lnw_ref, xn_ref = refs
        h = h_ref[...]
    xn_ref[...] = _rms(h, lnw_ref[...]).astype(BF16)


def _prenorm(h, delta, lnw, tm=1024):
    t, d = h.shape
    row = pl.BlockSpec((tm, d), lambda i: (i, 0))
    xn_shape = jax.ShapeDtypeStruct((t, d), BF16)
    with_delta = delta is not None
    out = pl.pallas_call(
        functools.partial(_prenorm_body, with_delta=with_delta),
        out_shape=(jax.ShapeDtypeStruct((t, d), F32), xn_shape) if with_delta else xn_shape,
        grid=(t // tm,),
        in_specs=([row, row] if with_delta else [row]) + [_const((1, d))],
        out_specs=(row, row) if with_delta else row,
        compiler_params=_params(("parallel",)),
        name="prenorm",
    )(*((h, delta) if with_delta else (h,)), lnw)
    return out if with_delta else (h, out)


RW_CHUNK = 64


def _rwkv_body(x_ref, win_ref, mu_ref, w0_ref, w2_ref, a0_ref, a2_ref, g2_ref, kk_ref, ka_ref, rk_ref,
               lnw_ref, lnb_ref, bd_ref, bd16_ref, hm_ref, tri_ref, lowi_ref, lows_ref, eye_ref,
               o_ref, carry_ref, st_ref, r_s, k_s, v_s, kn_s, kb_s, wl_s, y_s):
    nb, tt, d = x_ref.shape

    @pl.when(pl.program_id(1) == 0)
    def _():
        carry_ref[...] = jnp.zeros_like(carry_ref)
        st_ref[...] = jnp.zeros_like(st_ref)

    proj = jnp.dot(x_ref[...].reshape(nb * tt, d), win_ref[...], preferred_element_type=F32)
    rows = lax.broadcasted_iota(jnp.int32, (tt, 4 * GW), 0)
    mixed = []
    for b in range(nb):
        p = proj[b * tt:(b + 1) * tt]
        prev = jnp.where(rows == 0, carry_ref[b, 0:1, :], pltpu.roll(p, 1, axis=0))
        carry_ref[b, 0:1, :] = p[tt - 1:tt, :]
        mixed.append(p + (prev - p) * mu_ref[...])
    p = jnp.concatenate(mixed, axis=0)
    r = p[:, 0:GW]
    k = p[:, GW:2 * GW]
    v = p[:, 2 * GW:3 * GW]
    wl = p[:, 3 * GW:3 * GW + 64]
    al = p[:, 3 * GW + 64:3 * GW + 128]
    gl = p[:, 3 * GW + 128:]
    hm = hm_ref[...]
    w_log = -math.exp(-0.5) * _sigmoid(w0_ref[...] + _mm_x(jnp.tanh(wl), w2_ref[...], 2, 2))
    a = _sigmoid(a0_ref[...] + _mm(al, a2_ref[...]))
    gate = _mm(_sigmoid(gl), g2_ref[...])
    kn = k * kk_ref[...]
    kn = kn * lax.rsqrt(jnp.maximum(_mm_x(kn * kn, hm, 2, 1) * float(HEAD_DIM), 1e-24))
    k = k * (1.0 + (a - 1.0) * ka_ref[...])
    r_s[...] = r
    k_s[...] = k
    v_s[...] = v
    kn_s[...] = kn
    kb_s[...] = kn * a
    wl_s[...] = w_log

    def chunk(c, carry):
        seqs = range(nb)
        sls = [pl.ds(pl.multiple_of(b * tt + c * RW_CHUNK, RW_CHUNK), RW_CHUNK) for b in seqs]
        ld = lambda s: [s[sl, :] for sl in sls]
        r_c, k_c, v_c, kn_c, kb_c, wl_c = (ld(s) for s in (r_s, k_s, v_s, kn_s, kb_s, wl_s))
        bd = bd_ref[...]
        bd16 = bd16_ref[...]
        lows = lows_ref[...]
        lowi = lowi_ref[...]
        st4 = lambda xs: [_stack4(x, bd16) for x in xs]
        gc = [_mm_x(tri_ref[...], w, 1, 3) for w in wl_c]
        g_end = [g[RW_CHUNK - 1:RW_CHUNK, :] for g in gc]
        inv = [jnp.exp(-g) for g in gc]
        lhs = [jnp.concatenate([-kn * jnp.exp(g - w), r * jnp.exp(g)], axis=0)
               for kn, r, g, w in zip(kn_c, r_c, gc, wl_c)]
        rhs = [jnp.concatenate([_stack4(kb * i, bd16), _stack4(k * i, bd16)], axis=0)
               for kb, k, i in zip(kb_c, k_c, inv)]
        gram = [_mm_nt(a, b) for a, b in zip(lhs, rhs)]
        state = [st_ref[b] for b in seqs]
        from_state = [_mm_nt(a, s) for a, s in zip(lhs, state)]
        a_ab = [g[:RW_CHUNK, :GW] * lows for g in gram]
        pw = [eye_ref[...] + a for a in a_ab]
        q = [_mm(a, b) for a, b in zip(a_ab, st4(a_ab))]
        for step in range(4):
            both = [_mm(jnp.concatenate([p, a], axis=0), b) for p, a, b in zip(pw, q, st4(q))]
            pw = [p + pq[:RW_CHUNK] for p, pq in zip(pw, both)]
            q = [pq[RW_CHUNK:] for pq in both]
        pw = [p + _mm(p, b) for p, b in zip(pw, st4(q))]
        vst = st4(v_c)
        mask_k = jnp.concatenate([lows, lowi], axis=0)
        from_v = [_mm(g[:, GW:] * mask_k, vs) for g, vs in zip(gram, vst)]
        rhs_u = [f[:RW_CHUNK] + fv[:RW_CHUNK] for f, fv in zip(from_state, from_v)]
        u = [_mm(p, x) for p, x in zip(pw, st4(rhs_u))]
        ust = st4(u)
        for b in seqs:
            y_s[sls[b], :] = (from_state[b][RW_CHUNK:] + _mm(gram[b][RW_CHUNK:, :GW] * lowi, ust[b])
                              + from_v[b][RW_CHUNK:])
        for b in seqs:
            to_end = jnp.exp(g_end[b] - gc[b])
            upd = _mm_tn(jnp.concatenate([u[b], v_c[b]], axis=0),
                         jnp.concatenate([kb_c[b] * to_end, k_c[b] * to_end], axis=0))
            st_ref[b] = state[b] * jnp.exp(g_end[b]) + upd * bd
        return carry

    lax.fori_loop(0, tt // RW_CHUNK, chunk, 0)

    y = y_s[...]
    mean = _mm_x(y, hm, 2, 1)
    d = y - mean
    var = _mm_x(d * d, hm, 2, 1)
    yn = d * lax.rsqrt(var + RW_GN_EPS) * lnw_ref[...] + lnb_ref[...]
    bonus = _mm_x(r * k * rk_ref[...], hm, 2, 1) * float(HEAD_DIM) * v
    o_ref[...] = ((yn + bonus) * gate).astype(BF16).reshape(nb, tt, GW)


def _rwkv(xn3, w_in, prm, consts, nb=8, tt=128):
    b, l, d = xn3.shape
    vec = _const((1, GW))
    scr = pltpu.VMEM((nb * tt, GW), F32)
    return pl.pallas_call(
        _rwkv_body,
        out_shape=jax.ShapeDtypeStruct((b, l, GW), BF16),
        grid=(b // nb, l // tt),
        in_specs=[pl.BlockSpec((nb, tt, d), lambda i, j: (i, j, 0)), _const((d, 4 * GW)),
                  _const((1, 4 * GW)), vec, _const((64, GW)), vec, _const((64, GW)),
                  _const((128, GW)), vec, vec, vec, vec, vec,
                  _const((GW, GW)), _const((GW, GW)), _const((GW, GW)), _const((RW_CHUNK, RW_CHUNK)),
                  _const((RW_CHUNK, GW)), _const((RW_CHUNK, GW)), _const((RW_CHUNK, GW))],
        out_specs=pl.BlockSpec((nb, tt, GW), lambda i, j: (i, j, 0)),
        scratch_shapes=[pltpu.VMEM((nb, SUBLANES, 4 * GW), F32), pltpu.VMEM((nb, GW, GW), F32)] + [scr] * 7,
        compiler_params=_params(("parallel", "arbitrary")),
        name="rwkv7",
    )(xn3, w_in, prm["mu"], prm["w0"], prm["w2"], prm["a0"], prm["a2"], prm["g2"], prm["k_k"],
      prm["k_a"], prm["r_k"], prm["lnx_w"], prm["lnx_b"],
      consts["bd"], consts["hones"], consts["hm"], consts["tri64"], consts["lowi"], consts["lows"], consts["eyew"])


def _s5_body(x_ref, win_ref, bbd_ref, ar_ref, ai_ref, cbd_ref, d_ref, wglu_ref, bglu_ref, o_ref,
             ub_s, ut_s, xr_s, xi_s, ot_s, st_ref):
    nb, tt, d = x_ref.shape
    w = win_ref.shape[1]
    ncb = w // LANES
    assert nb == SUBLANES

    @pl.when(pl.program_id(0) == 0)
    def _():
        st_ref[...] = jnp.zeros_like(st_ref)

    u_bt = jnp.dot(x_ref[...].reshape(nb * tt, d), win_ref[...], preferred_element_type=F32)
    for cb in range(ncb):
        ub_s[cb] = u_bt[:, cb * LANES:(cb + 1) * LANES]

    def regroup(t, carry):
        for cb in range(ncb):
            ut_s[cb, pl.ds(pl.multiple_of(t * nb, nb), nb), :] = ub_s[cb, pl.ds(t, nb, stride=tt), :]
        return carry

    lax.fori_loop(0, tt, regroup, 0, unroll=8)
    u = jnp.concatenate([ut_s[cb] for cb in range(ncb)], axis=-1)
    bu = _mm(u, bbd_ref[...])
    xr_s[...] = bu[:, :S5_STATE_W]
    xi_s[...] = bu[:, S5_STATE_W:]
    ar = jnp.broadcast_to(ar_ref[...], (nb, S5_STATE_W))
    ai = jnp.broadcast_to(ai_ref[...], (nb, S5_STATE_W))

    def step(t, carry):
        xr, xi = carry
        rows = pl.ds(pl.multiple_of(t * nb, nb), nb)
        nr = ar * xr - ai * xi + xr_s[rows, :]
        ni = ar * xi + ai * xr + xi_s[rows, :]
        xr_s[rows, :] = nr
        xi_s[rows, :] = ni
        return nr, ni

    xr, xi = lax.fori_loop(0, tt, step, (st_ref[0], st_ref[1]), unroll=4)
    st_ref[0] = xr
    st_ref[1] = xi
    cbd = cbd_ref[...]
    y = _mm(xr_s[...], cbd[:S5_STATE_W]) + _mm(xi_s[...], cbd[S5_STATE_W:]) + d_ref[...] * u
    y = 0.5 * y * (1.0 + jnp.tanh(math.sqrt(2.0 / math.pi) * (y + 0.044715 * (y * y * y))))
    z = _mm(y, wglu_ref[...]) + bglu_ref[...]
    out = y * _sigmoid(z)
    for cb in range(ncb):
        ot_s[cb] = out[:, cb * LANES:(cb + 1) * LANES]
    for b in range(nb):
        for cb in range(ncb):
            o_ref[b, :, cb * LANES:(cb + 1) * LANES] = ot_s[cb, pl.ds(b, tt, stride=nb), :].astype(BF16)


def _s5(xn3, w_in, prm, tt=128):
    b, l, d = xn3.shape
    slab = pltpu.VMEM((GW // LANES, b * tt, LANES), F32)
    wide = pltpu.VMEM((b * tt, S5_STATE_W), F32)
    return pl.pallas_call(
        _s5_body,
        out_shape=jax.ShapeDtypeStruct((b, l, GW), BF16),
        grid=(l // tt,),
        in_specs=[pl.BlockSpec((b, tt, d), lambda j: (0, j, 0)), _const((d, GW)),
                  _const((GW, 2 * S5_STATE_W)), _const((1, S5_STATE_W)), _const((1, S5_STATE_W)),
                  _const((2 * S5_STATE_W, GW)), _const((1, GW)), _const((GW, GW)), _const((1, GW))],
        out_specs=pl.BlockSpec((b, tt, GW), lambda j: (0, j, 0)),
        scratch_shapes=[slab, slab, wide, wide, slab, pltpu.VMEM((2, b, S5_STATE_W), F32)],
        compiler_params=_params(("arbitrary",)),
        name="s5",
    )(xn3, w_in, prm["bbd"], prm["ar"], prm["ai"], prm["cbd"], prm["d"], prm["w_glu"], prm["b_glu"])


def _mamba_body(x_ref, win_ref, cw_ref, cb_ref, dtb_ref, aneg_ref, dexp_ref, nw_ref,
                tri_ref, exp_ref, gsel_ref, o_ref, carry_ref, st_ref):
    nb, tt, d = x_ref.shape
    seqs = range(nb)

    @pl.when(pl.program_id(1) == 0)
    def _():
        carry_ref[...] = jnp.zeros_like(carry_ref)
        st_ref[...] = jnp.zeros_like(st_ref)

    proj = jnp.dot(x_ref[...].reshape(nb * tt, d), win_ref[...], preferred_element_type=F32)
    cw = 2 * GW
    rows = lax.broadcasted_iota(jnp.int32, (tt, cw), 0)
    pad = jnp.zeros((tt - SUBLANES, cw), F32)
    xcs = []
    for b in seqs:
        xbc = proj[b * tt:(b + 1) * tt, GW:GW + cw]
        tail = carry_ref[b]
        conv = xbc * cw_ref[3:4, :]
        for s in (1, 2, 3):
            head = jnp.concatenate([pltpu.roll(tail, s, axis=0), pad], axis=0)
            shifted = jnp.where(rows < s, head, pltpu.roll(xbc, s, axis=0))
            conv = conv + shifted * cw_ref[3 - s:4 - s, :]
        carry_ref[b] = xbc[tt - SUBLANES:, :]
        xcs.append(_silu(conv + cb_ref[...]))
    xs = [xc[:, :GW] for xc in xcs]
    bm = [xc[:, GW:GW + 128] for xc in xcs]
    cm = [xc[:, GW + 128:] for xc in xcs]
    expand = exp_ref[...]
    dt = [_softplus(proj[b * tt:(b + 1) * tt, GW + cw:] + dtb_ref[...]) for b in seqs]
    cs = [_mm_x(tri_ref[...], x * aneg_ref[...], 1, 3) for x in dt]
    cs_t = [c.T for c in cs]
    wide = [_mm_x(jnp.concatenate([x, c], axis=0), expand, 3, 1) for x, c in zip(dt, cs)]
    cs_w = [w[tt:] for w in wide]
    end_w = [c[tt - 1:tt, :] for c in cs_w]
    xdt = [x * w[:tt] for x, w in zip(xs, wide)]
    lane = lax.broadcasted_iota(jnp.int32, (tt, 128), 1)
    tril = lax.broadcasted_iota(jnp.int32, (tt, tt), 0) >= lax.broadcasted_iota(jnp.int32, (tt, tt), 1)
    lane_w = lax.broadcasted_iota(jnp.int32, (tt, GW), 1)
    state = [st_ref[b] for b in seqs]
    y = [_mm(c, s) * jnp.exp(w) + dexp_ref[...] * x for c, s, w, x in zip(cm, state, cs_w, xs)]
    for g in range(2):
        cbm = [_mm_nt(jnp.where(lane // 64 == g, c, 0.0), b_) for c, b_ in zip(cm, bm)]
        for h in (2 * g, 2 * g + 1):
            decay = [jnp.where(tril, jnp.exp(jnp.minimum(c[:, h:h + 1] - ct[h:h + 1, :], 0.0)), 0.0)
                     for c, ct in zip(cs, cs_t)]
            y = [yy + jnp.where(lane_w // HEAD_DIM == h, _mm(m * dc, xd), 0.0)
                 for yy, m, dc, xd in zip(y, cbm, decay, xdt)]
    upd = [_mm_tn(b_, xd * jnp.exp(e - w)) for b_, xd, e, w in zip(bm, xdt, end_w, cs_w)]
    nw = nw_ref[...]
    half = GW // 2
    for b in seqs:
        st_ref[b] = state[b] * jnp.exp(end_w[b]) + upd[b] * gsel_ref[...]
        yb = y[b] * _silu(proj[b * tt:(b + 1) * tt, :GW])
        o_ref[b] = jnp.concatenate([_rms(yb[:, :half], nw[:, :half]), _rms(yb[:, half:], nw[:, half:])],
                                   axis=-1).astype(BF16)


def _mamba(xn3, w_in, prm, consts, nb=4):
    b, l, d = xn3.shape
    tt = M_CHUNK
    wcols = w_in.shape[1]
    return pl.pallas_call(
        _mamba_body,
        out_shape=jax.ShapeDtypeStruct((b, l, GW), BF16),
        grid=(b // nb, l // tt),
        in_specs=[pl.BlockSpec((nb, tt, d), lambda i, j: (i, j, 0)), _const((d, wcols)),
                  _const((4, 2 * GW)), _const((1, 2 * GW)), _const((1, 128)), _const((1, 128)),
                  _const((1, GW)), _const((1, GW)),
                  _const((tt, tt)), _const((128, GW)), _const((128, GW))],
        out_specs=pl.BlockSpec((nb, tt, GW), lambda i, j: (i, j, 0)),
        scratch_shapes=[pltpu.VMEM((nb, SUBLANES, 2 * GW), F32), pltpu.VMEM((nb, 128, GW), F32)],
        compiler_params=_params(("parallel", "arbitrary")),
        name="mamba2",
    )(xn3, w_in, prm["conv_w"], prm["conv_b"], prm["dt_bias"], prm["a_neg"], prm["d_exp"],
      prm["norm_w"], consts["tri128"], consts["expand"], consts["gsel"])


HG_BLOCK = 16
HG_CHUNK = 2 * HG_BLOCK


def _hgrn_body(x_ref, win_ref, lb_ref, nw_ref, bd_ref, bd64_ref, hm_ref, hones_ref, tri_ref, o_ref,
               st_ref, q_s, k_s, v_s, lf_s, o_s):
    nb, tt, d = x_ref.shape

    @pl.when(pl.program_id(1) == 0)
    def _():
        st_ref[...] = jnp.zeros_like(st_ref)

    p = jnp.dot(x_ref[...].reshape(nb * tt, d), win_ref[...], preferred_element_type=F32)
    lb = lb_ref[...]
    hf = p[:, GW:2 * GW]
    q_s[...] = _silu(p[:, :GW])
    k_s[...] = (1.0 - lb) * _sigmoid(-hf)
    v_s[...] = p[:, 2 * GW:3 * GW]
    lf_s[...] = jnp.log(jnp.maximum(lb + (1.0 - lb) * _sigmoid(hf), HG_F_FLOOR))

    def block(n, carry):
        seqs = range(nb)
        sls = [pl.ds(pl.multiple_of(b * tt + n * HG_CHUNK, HG_CHUNK), HG_CHUNK) for b in seqs]
        ld = lambda s: [s[sl, :] for sl in sls]
        q_c, k_c, v_c, lf_c = (ld(s) for s in (q_s, k_s, v_s, lf_s))
        g = [_mm_x(tri_ref[...], lf, 1, 3) for lf in lf_c]
        g_end = [x[HG_CHUNK - 1:HG_CHUNK, :] for x in g]
        state = [st_ref[b] for b in seqs]
        o = [_mm_nt(q * jnp.exp(x), s) for q, x, s in zip(q_c, g, state)]
        upd = [_mm_tn(v, k * jnp.exp(ge - x)) for v, k, ge, x in zip(v_c, k_c, g_end, g)]
        blk = lambda a, i: a.reshape(2, HG_BLOCK, GW)[i]
        half = HG_BLOCK // 2
        lower = lambda a: a.reshape(2, half, GW)[1]
        bd64 = bd64_ref[...]
        st4s = lambda a: jnp.concatenate([a.astype(BF16)] * N_HEADS, axis=0) * bd64
        g_mid = [blk(x, 0)[HG_BLOCK - 1:HG_BLOCK, :] for x in g]
        a_off = [_mm_nt(blk(q, 1) * jnp.exp(blk(x, 1) - gm), st4s(blk(k, 0) * jnp.exp(gm - blk(x, 0))))
                 for q, k, x, gm in zip(q_c, k_c, g, g_mid)]
        o_off = [_mm(a, st4s(blk(v, 0))) for a, v in zip(a_off, v_c)]
        prods = []
        for q32, k32, x32 in zip(q_c, k_c, g):
            pieces = []
            for i in range(2):
                q, k, x = blk(q32, i), blk(k32, i), blk(x32, i)
                ql, xl = lower(q), lower(x)
                pieces += [q * jnp.exp(jnp.minimum(x - x[j:j + 1, :], 0.0)) * k[j:j + 1, :] for j in range(half)]
                pieces += [ql * jnp.exp(jnp.minimum(xl - x[j:j + 1, :], 0.0)) * k[j:j + 1, :]
                           for j in range(half, HG_BLOCK)]
            prods.append(jnp.concatenate(pieces, axis=0))
        att = [_mm(p, hones_ref[...]) for p in prods]
        row = lax.broadcasted_iota(jnp.int32, (HG_BLOCK, GW), 0)
        rowl = lax.broadcasted_iota(jnp.int32, (half, GW), 0) + half
        per_blk = HG_BLOCK * half + half * half
        for b in seqs:
            outs = []
            for i in range(2):
                base = i * per_blk
                a_full = att[b][base:base + HG_BLOCK * half].reshape(half, HG_BLOCK, GW)
                a_low = att[b][base + HG_BLOCK * half:base + per_blk].reshape(half, half, GW)
                v16 = blk(v_c[b], i)
                ob = blk(o[b], i) if i == 0 else blk(o[b], i) + o_off[b]
                for j in range(half):
                    ob = ob + jnp.where(row >= j, a_full[j], 0.0) * v16[j:j + 1, :]
                add = jnp.zeros((half, GW), F32)
                for j in range(half, HG_BLOCK):
                    add = add + jnp.where(rowl >= j, a_low[j - half], 0.0) * v16[j:j + 1, :]
                outs.append(ob + jnp.concatenate([jnp.zeros((half, GW), F32), add], axis=0))
            o_s[sls[b], :] = jnp.concatenate(outs, axis=0)
            st_ref[b] = state[b] * jnp.exp(g_end[b]) + upd[b] * bd_ref[...]
        return carry

    lax.fori_loop(0, tt // HG_CHUNK, block, 0)
    o = o_s[...]
    ms = _mm_x(o * o, hm_ref[...], 2, 1)
    o_ref[...] = (o * lax.rsqrt(ms + NORM_EPS) * nw_ref[...] * _silu(p[:, 3 * GW:])).astype(BF16).reshape(nb, tt, GW)


def _hgrn(xn3, w_in, lb, nw, consts, nb=8, tt=128):
    b, l, d = xn3.shape
    return pl.pallas_call(
        _hgrn_body,
        out_shape=jax.ShapeDtypeStruct((b, l, GW), BF16),
        grid=(b // nb, l // tt),
        in_specs=[pl.BlockSpec((nb, tt, d), lambda i, j: (i, j, 0)), _const((d, 4 * GW)),
                  _const((1, GW)), _const((1, GW)), _const((GW, GW)), _const((N_HEADS * HG_BLOCK, GW)),
                  _const((GW, GW)), _const((GW, GW)), _const((HG_CHUNK, HG_CHUNK))],
        out_specs=pl.BlockSpec((nb, tt, GW), lambda i, j: (i, j, 0)),
        scratch_shapes=[pltpu.VMEM((nb, GW, GW), F32)] + [pltpu.VMEM((nb * tt, GW), F32)] * 5,
        compiler_params=_params(("parallel", "arbitrary")),
        name="hgrn2",
    )(xn3, w_in, lb, nw, consts["bd"], consts["bd64"], consts["hm"], consts["hones"], consts["tri32"])


def _outproj_body(h_ref, y1_ref, y2_ref, y3_ref, y4_ref, wo_ref, ln2_ref, wr_ref, br_ref,
                  hn_ref, xn_ref, comb_ref):
    y = jnp.concatenate([y1_ref[...], y2_ref[...], y3_ref[...], y4_ref[...]], axis=-1)
    h = h_ref[...] + _mm(y, wo_ref[...])
    hn_ref[...] = h
    xn = _rms(h, ln2_ref[...])
    xn_ref[...] = xn.astype(BF16)
    xh, xl = _parts(xn, 2)
    wr = wr_ref[...]
    first = jnp.dot(xh, wr, preferred_element_type=F32)
    logits = (first[:, :ROUTE_LANES] + first[:, ROUTE_LANES:]
              + jnp.dot(xl, wr[:, :ROUTE_LANES], preferred_element_type=F32) + br_ref[...])
    lane = lax.broadcasted_iota(jnp.int32, logits.shape, 1)
    neg = -jnp.inf
    big = ROUTE_LANES
    glog = jnp.where(lane < N_EXPERT_GROUPS, logits, neg)
    gmax = jnp.max(glog, axis=-1, keepdims=True)
    g_w = 1.0 / jnp.sum(jnp.exp(glog - gmax), axis=-1, keepdims=True)
    g_idx = jnp.min(jnp.where(glog == gmax, lane, big), axis=-1, keepdims=True)
    lo = ROUTE_OFF + EXPERTS_PER_GROUP * g_idx
    elog = jnp.where((lane >= lo) & (lane < lo + EXPERTS_PER_GROUP), logits, neg)
    m1 = jnp.max(elog, axis=-1, keepdims=True)
    i1 = jnp.min(jnp.where(elog == m1, lane, big), axis=-1, keepdims=True)
    elog2 = jnp.where(lane == i1, neg, elog)
    m2 = jnp.max(elog2, axis=-1, keepdims=True)
    i2 = jnp.min(jnp.where(elog2 == m2, lane, big), axis=-1, keepdims=True)
    e2 = jnp.exp(m2 - m1)
    w1 = 1.0 / (1.0 + e2)
    w2 = e2 / (1.0 + e2)
    comb_ref[...] = (g_w * (jnp.where(lane == i1 - lo, w1, 0.0) + jnp.where(lane == i2 - lo, w2, 0.0))
                     + jnp.where(lane == EXPERTS_PER_GROUP, g_idx.astype(F32), 0.0))


def _outproj(h, ys, wo, ln2, wr, br, tm=1024):
    t, d = h.shape
    row = lambda w: pl.BlockSpec((tm, w), lambda i: (i, 0))
    return pl.pallas_call(
        _outproj_body,
        out_shape=(jax.ShapeDtypeStruct((t, d), F32), jax.ShapeDtypeStruct((t, d), BF16),
                   jax.ShapeDtypeStruct((t, ROUTE_LANES), F32)),
        grid=(t // tm,),
        in_specs=[row(d), row(GW), row(GW), row(GW), row(GW), _const((d, d)), _const((1, d)),
                  _const((d, 2 * ROUTE_LANES)), _const((1, ROUTE_LANES))],
        out_specs=(row(d), row(d), row(ROUTE_LANES)),
        compiler_params=_params(("parallel",)),
        name="outproj_router",
    )(h, *ys, wo, ln2, wr, br)


MOE_SUB = 512
MOE_ROWS = 64
MOE_EXTRA = 16
MOE_HALF = EXPERTS_PER_GROUP // 2
COMB_GROUP_LANE = EXPERTS_PER_GROUP


def _moe_body(x_ref, comb_ref, wg_ref, wu_ref, wd_ref, o_ref, tri_s, kt_s, wt_s, cnt_s):
    i = pl.program_id(0)
    g = pl.program_id(1)
    hf = pl.program_id(2)
    tm = x_ref.shape[0]
    nsub = tm // MOE_SUB
    subs = [slice(s * MOE_SUB, (s + 1) * MOE_SUB) for s in range(nsub)]
    steps = N_EXPERTS // MOE_HALF

    @pl.when((i == 0) & (g == 0) & (hf == 0))
    def _():
        r = lax.broadcasted_iota(jnp.int32, (MOE_SUB, MOE_SUB), 0)
        c = lax.broadcasted_iota(jnp.int32, (MOE_SUB, MOE_SUB), 1)
        tri_s[...] = jnp.where(r < c, 1.0, 0.0).astype(BF16)

    @pl.when((g == 0) & (hf == 0))
    def _():
        comb = comb_ref[...]
        lane = lax.broadcasted_iota(jnp.int32, comb.shape, 1)
        gcol = comb[:, COMB_GROUP_LANE:COMB_GROUP_LANE + 1]
        local = jnp.where(lane < EXPERTS_PER_GROUP, comb, 0.0)
        w_tok = jnp.where(gcol == 0.0, local, 0.0)
        for grp in range(1, N_EXPERT_GROUPS):
            w_tok = w_tok + jnp.where(gcol == float(grp), pltpu.roll(local, EXPERTS_PER_GROUP * grp, axis=1), 0.0)
        wt = w_tok.T
        used = wt != 0.0
        wt_s[...] = wt
        most = jnp.zeros((ROUTE_LANES, 1), F32)
        for rows in subs:
            u = jnp.where(used[:, rows], 1.0, 0.0)
            before = jnp.dot(u.astype(BF16), tri_s[...], preferred_element_type=F32)
            kt_s[:, rows] = jnp.where(used[:, rows], before, -1.0)
            most = jnp.maximum(most, jnp.sum(u, axis=-1, keepdims=True))
        for q in range(steps):
            cnt_s[q] = jnp.max(most[MOE_HALF * q:MOE_HALF * (q + 1), :]).astype(jnp.int32)
        o_ref[...] = jnp.zeros_like(o_ref)

    second = hf == 1
    group_rows = pl.ds(pl.multiple_of(EXPERTS_PER_GROUP * g, EXPERTS_PER_GROUP), EXPERTS_PER_GROUP)
    kt8 = kt_s[group_rows, :]
    wt8 = wt_s[group_rows, :]
    pick = lambda a, j: jnp.where(second, a[MOE_HALF + j:MOE_HALF + j + 1, :], a[j:j + 1, :])
    keys = [pick(kt8, j) for j in range(MOE_HALF)]
    wrow = [pick(wt8, j) for j in range(MOE_HALF)]
    tn = lambda a, b: lax.dot_general(a, b, (((0,), (0,)), ((), ())), preferred_element_type=F32)

    def run_pass(base, nrows):
        ridx = lax.broadcasted_iota(jnp.int32, (nrows, MOE_SUB), 0).astype(F32)
        hit = [[keys[j][:, rows] - base == ridx for rows in subs] for j in range(MOE_HALF)]
        pts = [[jnp.where(m, 1.0, 0.0).astype(BF16) for m in row] for row in hit]
        pcat = [jnp.concatenate([pts[j][s] for j in range(MOE_HALF)], axis=0) for s in range(nsub)]
        xall = [jnp.dot(p, x_ref[rows, :], preferred_element_type=F32).astype(BF16)
                for p, rows in zip(pcat, subs)]
        ys = []
        for j in range(MOE_HALF):
            part = slice(j * nrows, (j + 1) * nrows)
            xg = jnp.concatenate([xa[part] for xa in xall], axis=0)
            cg = jnp.concatenate([jnp.sum(jnp.where(m, wrow[j][:, rows], 0.0), axis=-1, keepdims=True)
                                  for m, rows in zip(hit[j], subs)], axis=0)
            act = (_silu(jnp.dot(xg, wg_ref[0, j], preferred_element_type=F32))
                   * jnp.dot(xg, wu_ref[0, j], preferred_element_type=F32) * cg)
            ys.append(_mm(act, wd_ref[0, j]))
        for s, rows in enumerate(subs):
            part = slice(s * nrows, (s + 1) * nrows)
            ycat = jnp.concatenate([ys[j][part] for j in range(MOE_HALF)], axis=0)
            o_ref[rows, :] += tn(pcat[s], ycat.astype(BF16))

    count = cnt_s[2 * g + hf]

    @pl.when(count > 0)
    def _():
        run_pass(jnp.float32(0.0), MOE_ROWS)

    def extra(p, carry):
        run_pass((MOE_ROWS + p * MOE_EXTRA).astype(F32), MOE_EXTRA)
        return carry

    lax.fori_loop(0, (jnp.maximum(count - MOE_ROWS, 0) + MOE_EXTRA - 1) // MOE_EXTRA, extra, 0)


def _moe(xn, comb, wg, wu, wd, layer, tm=2048):
    t, d = xn.shape
    _, ng, eg, _, de = wg.shape
    row = lambda w: pl.BlockSpec((tm, w), lambda i, g, hf: (i, 0))
    wspec = lambda a, b: pl.BlockSpec((None, 1, MOE_HALF, a, b), lambda i, g, hf: (layer, g, hf, 0, 0))
    return pl.pallas_call(
        _moe_body,
        out_shape=jax.ShapeDtypeStruct((t, d), F32),
        grid=(t // tm, ng, eg // MOE_HALF),
        in_specs=[row(d), row(ROUTE_LANES), wspec(d, de), wspec(d, de), wspec(de, d)],
        out_specs=row(d),
        scratch_shapes=[pltpu.VMEM((MOE_SUB, MOE_SUB), BF16), pltpu.VMEM((ROUTE_LANES, tm), F32),
                        pltpu.VMEM((ROUTE_LANES, tm), F32), pltpu.SMEM((N_EXPERTS // MOE_HALF,), jnp.int32)],
        compiler_params=_params(("arbitrary", "arbitrary", "arbitrary")),
        name="moe",
    )(xn, comb, wg, wu, wd)


def _norm_body(h_ref, dl_ref, w_ref, o_ref):
    o_ref[...] = _rms(h_ref[...] + dl_ref[...], w_ref[...])


def _final_norm(h, delta, w, tm=1024):
    t, d = h.shape
    row = pl.BlockSpec((tm, d), lambda i: (i, 0))
    return pl.pallas_call(
        _norm_body,
        out_shape=jax.ShapeDtypeStruct((t, d), F32),
        grid=(t // tm,),
        in_specs=[row, row, _const((1, d))],
        out_specs=row,
        compiler_params=_params(("parallel",)),
        name="final_norm",
    )(h, delta, w)


def _mask_consts():
    i256 = jnp.arange(GW)
    same_head = (i256[:, None] // HEAD_DIM) == (i256[None, :] // HEAD_DIM)
    t64 = jnp.arange(RW_CHUNK)
    s_w = i256 % RW_CHUNK
    h128 = jnp.arange(128)
    return {
        "bd": same_head.astype(F32),
        "hm": same_head.astype(F32) / HEAD_DIM,
        "hones": same_head.astype(BF16),
        "tri64": (t64[:, None] >= t64[None, :]).astype(F32),
        "lowi": (t64[:, None] >= s_w[None, :]).astype(F32),
        "lows": (t64[:, None] > s_w[None, :]).astype(F32),
        "eyew": (t64[:, None] == s_w[None, :]).astype(F32),
        "tri32": (jnp.arange(HG_CHUNK)[:, None] >= jnp.arange(HG_CHUNK)[None, :]).astype(F32),
        "bd64": ((jnp.arange(N_HEADS * HG_BLOCK)[:, None] // HG_BLOCK) == (i256[None, :] // HEAD_DIM)).astype(BF16),
        "tri128": (h128[:, None] >= h128[None, :]).astype(F32),
        "expand": (h128[:, None] == (i256[None, :] // HEAD_DIM)).astype(F32),
        "gsel": ((h128[:, None] // 64) == (i256[None, :] // 128)).astype(F32),
    }


def _s5_params(lam_re, lam_im, log_dt, b_re, b_im, c_re, c_im, d_skip, w_glu, b_glu):
    lr = jnp.minimum(lam_re, -1e-4)
    li = lam_im
    dt = jnp.exp(log_dt)[:, None]
    mag = jnp.exp(lr * dt)
    ar, ai = mag * jnp.cos(li * dt), mag * jnp.sin(li * dt)
    den = lr * lr + li * li
    nr = ar - 1.0
    er, ei = (nr * lr + ai * li) / den, (ai * lr - nr * li) / den
    bbr = er[..., None] * b_re - ei[..., None] * b_im
    bbi = er[..., None] * b_im + ei[..., None] * b_re
    eye = jnp.eye(lam_re.shape[0], dtype=F32)
    pack_b = lambda m: jnp.einsum("gph,gk->ghkp", m, eye).reshape(GW, S5_STATE_W)
    pack_c = lambda m: jnp.einsum("ghp,gk->gpkh", m, eye).reshape(S5_STATE_W, GW)
    return {
        "bbd": jnp.concatenate([pack_b(bbr), pack_b(bbi)], axis=1).astype(BF16),
        "cbd": jnp.concatenate([pack_c(c_re), -pack_c(c_im)], axis=0).astype(BF16),
        "ar": ar.reshape(1, S5_STATE_W), "ai": ai.reshape(1, S5_STATE_W),
        "d": d_skip.reshape(1, GW), "w_glu": w_glu.astype(BF16), "b_glu": b_glu.reshape(1, GW),
    }


def kernel(x, ln1_w, w_in, rw_mu, rw_w0, rw_w2, rw_a0, rw_a2, rw_g2, rw_k_k, rw_k_a, rw_r_k, rw_lnx_w, rw_lnx_b, s5_lam_re, s5_lam_im, s5_log_dt, s5_b_re, s5_b_im, s5_c_re, s5_c_im, s5_d, s5_w_glu, s5_b_glu, m_conv_w, m_conv_b, m_dt_bias, m_a_log, m_d, m_norm_w, hg_lb_logits, hg_norm_w, w_out, ln2_w, moe_w_rg, moe_b_rg, moe_w_re, moe_b_re, moe_w_gate, moe_w_up, moe_w_down, lnf_w):
    bsz, seq, d = x.shape
    depth = w_in.shape[0]
    consts = _mask_consts()
    lbs = jax.nn.softmax(hg_lb_logits.astype(F32), axis=0)
    lbs = jnp.cumsum(lbs, axis=0) - lbs[0:1]
    row = lambda v: v.reshape(1, -1).astype(F32)
    n_dt = N_HEADS
    h = x.reshape(bsz * seq, d)
    delta = None
    wg16, wu16, wd16 = (w.astype(BF16) for w in (moe_w_gate, moe_w_up, moe_w_down))
    for l in range(depth):
        c_s5, c_m, c_hg = 4 * GW, 5 * GW, 8 * GW + n_dt
        w_rw = w_in[l, :, :c_s5].astype(BF16)
        w_s5 = w_in[l, :, c_s5:c_m].astype(BF16)
        w_m = jnp.pad(w_in[l, :, c_m:c_hg].astype(BF16), ((0, 0), (0, LANES - n_dt)))
        w_hg = w_in[l, :, c_hg:].astype(BF16)
        h, xn1 = _prenorm(h, delta, row(ln1_w[l]))
        xn1 = xn1.reshape(bsz, seq, d)
        rw = {"mu": row(rw_mu[l]), "w0": row(rw_w0[l]), "w2": rw_w2[l], "a0": row(rw_a0[l]),
              "a2": rw_a2[l], "g2": rw_g2[l], "k_k": row(rw_k_k[l]), "k_a": row(rw_k_a[l]),
              "r_k": row(rw_r_k[l]), "lnx_w": row(rw_lnx_w[l]), "lnx_b": row(rw_lnx_b[l])}
        y_rw = _rwkv(xn1, w_rw, rw, consts)
        y_s5 = _s5(xn1, w_s5, _s5_params(s5_lam_re[l], s5_lam_im[l], s5_log_dt[l], s5_b_re[l], s5_b_im[l],
                                    s5_c_re[l], s5_c_im[l], s5_d[l], s5_w_glu[l], s5_b_glu[l]))
        pad_h = lambda v: jnp.pad(v.astype(F32), (0, LANES - n_dt)).reshape(1, LANES)
        mp = {"conv_w": m_conv_w[l], "conv_b": row(m_conv_b[l]), "dt_bias": pad_h(m_dt_bias[l]),
              "a_neg": pad_h(-jnp.exp(m_a_log[l].astype(F32))),
              "d_exp": row(jnp.repeat(m_d[l], HEAD_DIM)), "norm_w": row(m_norm_w[l])}
        y_m = _mamba(xn1, w_m, mp, consts)
        y_hg = _hgrn(xn1, w_hg, row(lbs[l]), row(hg_norm_w[l]), consts)
        ys = [y.reshape(bsz * seq, GW) for y in (y_rw, y_s5, y_m, y_hg)]
        n_route = N_EXPERT_GROUPS + N_EXPERTS
        wr = jnp.pad(jnp.concatenate([moe_w_rg[l], moe_w_re[l]], axis=1), ((0, 0), (0, ROUTE_LANES - n_route)))
        wr_hi = wr.astype(BF16)
        wr_lo = (wr - wr_hi.astype(F32)).astype(BF16)
        br = jnp.pad(jnp.concatenate([moe_b_rg[l], moe_b_re[l]]), (0, ROUTE_LANES - n_route)).reshape(1, -1)
        h, xn, comb = _outproj(h, ys, w_out[l].astype(BF16), row(ln2_w[l]),
                               jnp.concatenate([wr_hi, wr_lo], axis=1), br)
        delta = _moe(xn, comb, wg16, wu16, wd16, l)
    return _final_norm(h, delta, row(lnf_w)).reshape(bsz, seq, d)
```
